```python
import math
import jax, jax.numpy as jnp
from jax import lax
import numpy as np

D_MODEL = 2048
BATCH = 8
SEQ = 2048
DEPTH = 1

GRID_W = 64
CTX_LEN = 256
HEAD_DIM = 128
N_HEADS = 8
N_KV_HEADS = 2
Q_PER_KV = N_HEADS // N_KV_HEADS
ATTN_W = N_HEADS * HEAD_DIM
KV_W = N_KV_HEADS * HEAD_DIM
POOL_WINDOWS = (2, 4, 8, 16)
N_POOL_GROUPS = len(POOL_WINDOWS)
POOL_W = D_MODEL // 2
POOL_GROUP_W = POOL_W // N_POOL_GROUPS
MIX_W = POOL_W + ATTN_W
PROJ_W = POOL_W + ATTN_W + 2 * KV_W
D_FF = ((8 * D_MODEL // 3 + 255) // 256) * 256
ROPE_THETA = 10000.0
AXIS_ROT = HEAD_DIM // 2
Q_BLOCK = 128
EPS = 1e-6
N_MOD = 6

kernel_name = "hybrid_pool_gqa_dit_block"


def _rmsnorm(x, g):
    xf = x.astype(jnp.float32)
    y = xf * lax.rsqrt(jnp.mean(xf * xf, axis=-1, keepdims=True) + EPS)
    return (y * g.astype(jnp.float32)).astype(x.dtype)


def _split_proj(p):
    pool = p[..., :POOL_W]
    q = p[..., POOL_W:POOL_W + ATTN_W]
    k = p[..., POOL_W + ATTN_W:POOL_W + ATTN_W + KV_W]
    v = p[..., POOL_W + ATTN_W + KV_W:]
    return pool, q, k, v


def _heads_q(q):
    B, T, _ = q.shape
    return q.reshape(B, T, N_KV_HEADS, Q_PER_KV, HEAD_DIM)


def _heads_kv(k):
    B, T, _ = k.shape
    return k.reshape(B, T, N_KV_HEADS, HEAD_DIM)


def _axial_angles(T):
    n_rows = T // GRID_W
    rows = jnp.repeat(jnp.arange(n_rows, dtype=jnp.float32), GRID_W)
    cols = jnp.tile(jnp.arange(GRID_W, dtype=jnp.float32), n_rows)
    freqs = ROPE_THETA ** (-jnp.arange(0, AXIS_ROT, 2, dtype=jnp.float32) / AXIS_ROT)
    ang = jnp.concatenate([rows[:, None] * freqs, cols[:, None] * freqs], axis=-1)
    return jnp.cos(ang), jnp.sin(ang)


def _rope_2d(x, cos, sin):
    extra = x.ndim - 3
    cos = cos.reshape(cos.shape[0], *([1] * extra), cos.shape[-1])
    sin = sin.reshape(sin.shape[0], *([1] * extra), sin.shape[-1])
    xf = x.astype(jnp.float32).reshape(*x.shape[:-1], HEAD_DIM // 2, 2)
    x1, x2 = xf[..., 0], xf[..., 1]
    out = jnp.stack([x1 * cos - x2 * sin, x1 * sin + x2 * cos], axis=-1)
    return out.reshape(x.shape).astype(x.dtype)


def _attend(q, k, v):
    B, T = q.shape[0], q.shape[1]
    nb = T // Q_BLOCK
    scale = 1.0 / math.sqrt(HEAD_DIM)
    qb = q.reshape(B, nb, Q_BLOCK, N_KV_HEADS, Q_PER_KV, HEAD_DIM).transpose(1, 0, 2, 3, 4, 5)

    def one_block(q_blk):
        s = jnp.einsum('bqkgd,bskd->bkgqs', q_blk, k,
                       preferred_element_type=jnp.float32) * scale
        p = jax.nn.softmax(s, axis=-1).astype(v.dtype)
        return jnp.einsum('bkgqs,bskd->bqkgd', p, v)

    o = lax.map(one_block, qb)
    return o.transpose(1, 0, 2, 3, 4, 5).reshape(B, T, ATTN_W)


def _pool_mixer(u, w_grp, scale):
    B, T, _ = u.shape
    uf = u.astype(jnp.float32).reshape(B, T, N_POOL_GROUPS, POOL_GROUP_W)
    cs = jnp.concatenate([jnp.zeros_like(uf[:, :1]), jnp.cumsum(uf, axis=1)], axis=1)
    t = jnp.arange(T)[:, None]
    half = jnp.array(POOL_WINDOWS, dtype=jnp.int32)[None, :] // 2
    lo = jnp.clip(t - half, 0, T)
    hi = jnp.clip(t + half, 0, T)
    gi = jnp.arange(N_POOL_GROUPS)[None, :]
    win_mean = (cs[:, hi, gi] - cs[:, lo, gi]) / (hi - lo).astype(jnp.float32)[None, :, :, None]
    pooled = (win_mean - uf).astype(u.dtype)
    mixed = jnp.einsum('btgc,gcd->btgd', pooled, w_grp).reshape(B, T, POOL_W)
    return mixed * scale


def _swiglu(h, w_gate, w_up, w_down):
    return (jax.nn.silu(h @ w_gate) * (h @ w_up)) @ w_down


def setup_inputs(seed: int = 0) -> dict:
    key = jax.random.key(seed)
    ks = jax.random.split(key, 18)
    f32 = jnp.float32

    def nrm(k, shape, fan_in, mult=1.0):
        return jax.random.normal(k, shape, f32) * (mult * fan_in ** -0.5)

    return {
        "x": jax.random.normal(ks[0], (BATCH, SEQ, D_MODEL), f32),
        "c": jax.random.normal(ks[1], (BATCH, D_MODEL), f32),
        "ctx": jax.random.normal(ks[2], (BATCH, CTX_LEN, D_MODEL), f32),
        "c_ctx": jax.random.normal(ks[3], (D_MODEL,), f32),
        "w_ada": nrm(ks[4], (DEPTH, D_MODEL, N_MOD * D_MODEL), D_MODEL, 0.5),
        "b_ada": 0.02 * jax.random.normal(ks[5], (DEPTH, N_MOD * D_MODEL), f32),
        "norm_mix": 1.0 + 0.05 * jax.random.normal(ks[6], (DEPTH, D_MODEL), f32),
        "norm_ffn": 1.0 + 0.05 * jax.random.normal(ks[7], (DEPTH, D_MODEL), f32),
        "w_in": nrm(ks[8], (DEPTH, D_MODEL, PROJ_W), D_MODEL),
        "pool_w": nrm(ks[9], (DEPTH, N_POOL_GROUPS, POOL_GROUP_W, POOL_GROUP_W), POOL_GROUP_W),
        "pool_scale": 1.0 + 0.1 * jax.random.normal(ks[10], (DEPTH, POOL_W), f32),
        "q_norm": 1.0 + 0.05 * jax.random.normal(ks[11], (DEPTH, HEAD_DIM), f32),
        "k_norm": 1.0 + 0.05 * jax.random.normal(ks[12], (DEPTH, HEAD_DIM), f32),
        "w_out": nrm(ks[13], (DEPTH, MIX_W, D_MODEL), MIX_W),
        "w_gate": nrm(ks[14], (DEPTH, D_MODEL, D_FF), D_MODEL),
        "w_up": nrm(ks[15], (DEPTH, D_MODEL, D_FF), D_MODEL),
        "w_down": nrm(ks[16], (DEPTH, D_FF, D_MODEL), D_FF),
        "final_norm": 1.0 + 0.05 * jax.random.normal(ks[17], (D_MODEL,), f32),
    }


def reference(x, c, ctx, c_ctx, w_ada, b_ada, norm_mix, norm_ffn, w_in, pool_w,
              pool_scale, q_norm, k_norm, w_out, w_gate, w_up, w_down, final_norm):
    T = x.shape[1]
    cos, sin = _axial_angles(T)

    for layer in range(DEPTH):
        mod_lat = jax.nn.silu(c) @ w_ada[layer] + b_ada[layer]
        mod_ctx = jax.nn.silu(c_ctx) @ w_ada[layer] + b_ada[layer]
        sh_m, sc_m, g_m, sh_f, sc_f, g_f = jnp.split(mod_lat, N_MOD, axis=-1)
        csh_m, csc_m, cg_m, csh_f, csc_f, cg_f = jnp.split(mod_ctx, N_MOD, axis=-1)
        last = layer == DEPTH - 1

        hc = _rmsnorm(ctx, norm_mix[layer]) * (1.0 + csc_m) + csh_m
        if last:
            kv_c = hc @ w_in[layer][:, POOL_W + ATTN_W:]
            k_c, v_c = kv_c[..., :KV_W], kv_c[..., KV_W:]
        else:
            p_c, q_c, k_c, v_c = _split_proj(hc @ w_in[layer])
        k_c = _rmsnorm(_heads_kv(k_c), k_norm[layer])
        v_c = _heads_kv(v_c)

        hx = _rmsnorm(x, norm_mix[layer]) * (1.0 + sc_m[:, None]) + sh_m[:, None]
        p_x, q_x, k_x, v_x = _split_proj(hx @ w_in[layer])
        q_x = _rope_2d(_rmsnorm(_heads_q(q_x), q_norm[layer]), cos, sin)
        k_x = _rope_2d(_rmsnorm(_heads_kv(k_x), k_norm[layer]), cos, sin)
        k_all = jnp.concatenate([k_c, k_x], axis=1)
        v_all = jnp.concatenate([v_c, _heads_kv(v_x)], axis=1)
        attn_x = _attend(q_x, k_all, v_all)
        pool_x = _pool_mixer(p_x, pool_w[layer], pool_scale[layer])
        mix_x = jnp.concatenate([pool_x, attn_x], axis=-1) @ w_out[layer]

        if not last:
            q_c = _rmsnorm(_heads_q(q_c), q_norm[layer])
            attn_c = _attend(q_c, k_c, v_c)
            pool_c = _pool_mixer(p_c, pool_w[layer], pool_scale[layer])
            ctx = ctx + cg_m * (jnp.concatenate([pool_c, attn_c], axis=-1) @ w_out[layer])
            hfc = _rmsnorm(ctx, norm_ffn[layer]) * (1.0 + csc_f) + csh_f
            ctx = ctx + cg_f * _swiglu(hfc, w_gate[layer], w_up[layer], w_down[layer])

        x = x + g_m[:, None] * mix_x
        hf = _rmsnorm(x, norm_ffn[layer]) * (1.0 + sc_f[:, None]) + sh_f[:, None]
        x = x + g_f[:, None] * _swiglu(hf, w_gate[layer], w_up[layer], w_down[layer])

    return _rmsnorm(x, final_norm)
```

```python
import functools
import math

import jax
import jax.numpy as jnp
from jax import lax
from jax.experimental import pallas as pl
from jax.experimental.pallas import tpu as pltpu

D_MODEL = 2048
GRID_W = 64
HEAD_DIM = 128
N_HEADS = 8
N_KV_HEADS = 2
Q_PER_KV = N_HEADS // N_KV_HEADS
ATTN_W = N_HEADS * HEAD_DIM
KV_W = N_KV_HEADS * HEAD_DIM
POOL_WINDOWS = (2, 4, 8, 16)
N_POOL_GROUPS = len(POOL_WINDOWS)
POOL_W = D_MODEL // 2
POOL_GROUP_W = POOL_W // N_POOL_GROUPS
MIX_W = POOL_W + ATTN_W
PROJ_W = POOL_W + ATTN_W + 2 * KV_W
ROPE_THETA = 10000.0
AXIS_ROT = HEAD_DIM // 2
EPS = 1e-6
N_MOD = 6
MOD_ROWS = 16
POOL_PAD = 8

F32 = jnp.float32
BF16 = jnp.bfloat16

VMEM_LIMIT_BYTES = 60 * 1024 * 1024


def _params(*sem):
    return pltpu.CompilerParams(dimension_semantics=sem, vmem_limit_bytes=VMEM_LIMIT_BYTES)


def _rms(x):
    return x * lax.rsqrt(jnp.mean(x * x, axis=-1, keepdims=True) + EPS)


def _adaln_kernel(c_ref, w_ref, b_ref, o_ref):
    c = c_ref[...]
    a = c * jax.nn.sigmoid(c)
    o_ref[...] = jnp.dot(a, w_ref[...], preferred_element_type=F32) + b_ref[...]


def _adaln_mod(cc, w_ada, b_ada, tn=1024):
    n = w_ada.shape[1]
    return pl.pallas_call(
        _adaln_kernel,
        grid=(n // tn,),
        in_specs=[
            pl.BlockSpec((MOD_ROWS, D_MODEL), lambda j: (0, 0)),
            pl.BlockSpec((D_MODEL, tn), lambda j: (0, j)),
            pl.BlockSpec((1, tn), lambda j: (0, j)),
        ],
        out_specs=pl.BlockSpec((MOD_ROWS, tn), lambda j: (0, j)),
        out_shape=jax.ShapeDtypeStruct((MOD_ROWS, n), F32),
        compiler_params=_params("arbitrary"),
        name="adaln_mod",
    )(cc, w_ada, b_ada)


def _norm_modulate(x, nw, mod_ref, shift_idx):
    sh = mod_ref[0, shift_idx:shift_idx + 1, :]
    sc = mod_ref[0, shift_idx + 1:shift_idx + 2, :]
    return (_rms(x) * nw) * (1.0 + sc) + sh


def _rope(x, cos, sin_signed, even_lane):
    partner = jnp.where(even_lane, pltpu.roll(x, HEAD_DIM - 1, axis=1), pltpu.roll(x, 1, axis=1))
    return x * cos + partner * sin_signed


def _ctx_kv_kernel(x_ref, mod_ref, nw_ref, w_ref, kn_ref, k_ref, v_ref):
    h = _norm_modulate(x_ref[0], nw_ref[...], mod_ref, 0)
    kv = jnp.dot(h.astype(BF16), w_ref[...], preferred_element_type=F32)
    kn = kn_ref[...]
    for j in range(N_KV_HEADS):
        kh = kv[:, j * HEAD_DIM:(j + 1) * HEAD_DIM]
        k_ref[0, :, j * HEAD_DIM:(j + 1) * HEAD_DIM] = (_rms(kh) * kn).astype(BF16)
    v_ref[0] = kv[:, KV_W:].astype(BF16)


def _ctx_kv(ctx, mod3, norm_w, w_in_bf, k_norm):
    b, s, _ = ctx.shape
    kv_block = (POOL_W + ATTN_W) // (2 * KV_W)
    return pl.pallas_call(
        _ctx_kv_kernel,
        grid=(b,),
        in_specs=[
            pl.BlockSpec((1, s, D_MODEL), lambda i: (i, 0, 0)),
            pl.BlockSpec((1, N_MOD, D_MODEL), lambda i: (MOD_ROWS // 2, 0, 0)),
            pl.BlockSpec((1, D_MODEL), lambda i: (0, 0)),
            pl.BlockSpec((D_MODEL, 2 * KV_W), lambda i: (0, kv_block)),
            pl.BlockSpec((1, HEAD_DIM), lambda i: (0, 0)),
        ],
        out_specs=[
            pl.BlockSpec((1, s, KV_W), lambda i: (i, 0, 0)),
            pl.BlockSpec((1, s, KV_W), lambda i: (i, 0, 0)),
        ],
        out_shape=[jax.ShapeDtypeStruct((b, s, KV_W), BF16)] * 2,
        compiler_params=_params("arbitrary"),
        name="ctx_kv",
    )(ctx, mod3, norm_w, w_in_bf, k_norm)


def _in_proj_kernel(x_ref, mod_ref, nw_ref, w_ref, qn_ref, kn_ref, cos_ref, sin_ref,
                    p_ref, q_ref, k_ref, v_ref):
    h = _norm_modulate(x_ref[0], nw_ref[...], mod_ref, 0)
    proj = jnp.dot(h.astype(BF16), w_ref[...], preferred_element_type=F32)
    p_ref[0] = proj[:, :POOL_W]
    cos = cos_ref[...]
    sin = sin_ref[...]
    even_lane = (lax.broadcasted_iota(jnp.int32, cos.shape, 1) % 2) == 0
    qn = qn_ref[...] * (1.0 / math.sqrt(HEAD_DIM))
    kn = kn_ref[...]
    for j in range(N_HEADS):
        lo = POOL_W + j * HEAD_DIM
        qh = _rms(proj[:, lo:lo + HEAD_DIM]) * qn
        q_ref[0, :, j * HEAD_DIM:(j + 1) * HEAD_DIM] = _rope(qh, cos, sin, even_lane).astype(BF16)
    for j in range(N_KV_HEADS):
        lo = POOL_W + ATTN_W + j * HEAD_DIM
        kh = _rms(proj[:, lo:lo + HEAD_DIM]) * kn
        k_ref[0, :, j * HEAD_DIM:(j + 1) * HEAD_DIM] = _rope(kh, cos, sin, even_lane).astype(BF16)
    v_ref[0] = proj[:, POOL_W + ATTN_W + KV_W:].astype(BF16)


def _in_proj(x, mod3, norm_w, w_in_bf, q_norm, k_norm, cos_full, sin_signed, tm=512):
    b, t, _ = x.shape
    row = lambda i, j: (i, j, 0)
    const2 = lambda i, j: (0, 0)
    return pl.pallas_call(
        _in_proj_kernel,
        grid=(b, t // tm),
        in_specs=[
            pl.BlockSpec((1, tm, D_MODEL), row),
            pl.BlockSpec((1, N_MOD, D_MODEL), lambda i, j: (i, 0, 0)),
            pl.BlockSpec((1, D_MODEL), const2),
            pl.BlockSpec((D_MODEL, PROJ_W), const2),
            pl.BlockSpec((1, HEAD_DIM), const2),
            pl.BlockSpec((1, HEAD_DIM), const2),
            pl.BlockSpec((tm, HEAD_DIM), lambda i, j: (j, 0)),
            pl.BlockSpec((tm, HEAD_DIM), lambda i, j: (j, 0)),
        ],
        out_specs=[
            pl.BlockSpec((1, tm, POOL_W), row),
            pl.BlockSpec((1, tm, ATTN_W), row),
            pl.BlockSpec((1, tm, KV_W), row),
            pl.BlockSpec((1, tm, KV_W), row),
        ],
        out_shape=[
            jax.ShapeDtypeStruct((b, t, POOL_W), F32),
            jax.ShapeDtypeStruct((b, t, ATTN_W), BF16),
            jax.ShapeDtypeStruct((b, t, KV_W), BF16),
            jax.ShapeDtypeStruct((b, t, KV_W), BF16),
        ],
        compiler_params=_params("arbitrary", "arbitrary"),
        name="in_proj",
    )(x, mod3, norm_w, w_in_bf, q_norm, k_norm, cos_full, sin_signed)


def _pool_kernel(p_ref, w_ref, s_ref, o_ref):
    t = p_ref.shape[1]
    n = t + 2 * POOL_PAD
    pos = lax.broadcasted_iota(jnp.int32, (t, 1), 0)
    zeros = jnp.zeros((POOL_PAD, POOL_GROUP_W), F32)
    for g, win in enumerate(POOL_WINDOWS):
        half = win // 2
        u = p_ref[0, :, g * POOL_GROUP_W:(g + 1) * POOL_GROUP_W]
        f = jnp.concatenate([zeros, u, zeros], axis=0)
        step = 1
        while step < win:
            f = f + pltpu.roll(f, n - step, axis=0)
            step *= 2
        if half != POOL_PAD:
            f = pltpu.roll(f, n - (POOL_PAD - half), axis=0)
        count = jnp.minimum(pos + half, t) - jnp.maximum(pos - half, 0)
        pooled = f[:t] * (1.0 / count.astype(F32)) - u
        mixed = jnp.dot(pooled.astype(BF16), w_ref[g], preferred_element_type=F32)
        o_ref[0, :, g * POOL_GROUP_W:(g + 1) * POOL_GROUP_W] = (
            mixed * s_ref[:, g * POOL_GROUP_W:(g + 1) * POOL_GROUP_W]).astype(BF16)


def _pool_mix(p, pool_w_bf, pool_scale):
    b, t, _ = p.shape
    return pl.pallas_call(
        _pool_kernel,
        grid=(b,),
        in_specs=[
            pl.BlockSpec((1, t, POOL_W), lambda i: (i, 0, 0)),
            pl.BlockSpec((N_POOL_GROUPS, POOL_GROUP_W, POOL_GROUP_W), lambda i: (0, 0, 0)),
            pl.BlockSpec((1, POOL_W), lambda i: (0, 0)),
        ],
        out_specs=pl.BlockSpec((1, t, POOL_W), lambda i: (i, 0, 0)),
        out_shape=jax.ShapeDtypeStruct((b, t, POOL_W), BF16),
        compiler_params=_params("arbitrary"),
        name="pool_mix",
    )(p, pool_w_bf, pool_scale)


def _attn_kernel(q_ref, kx_ref, kc_ref, vx_ref, vc_ref, o_ref):
    kx = kx_ref[0]
    kc = kc_ref[0]
    vx = vx_ref[0]
    vc = vc_ref[0]
    nt = (((1,), (1,)), ((), ()))
    for g in range(Q_PER_KV):
        q = q_ref[0, :, g * HEAD_DIM:(g + 1) * HEAD_DIM]
        sx = lax.dot_general(q, kx, nt, preferred_element_type=F32)
        sc = lax.dot_general(q, kc, nt, preferred_element_type=F32)
        m = jnp.maximum(jnp.max(sx, axis=-1, keepdims=True), jnp.max(sc, axis=-1, keepdims=True))
        px = jnp.exp(sx - m)
        pc = jnp.exp(sc - m)
        denom = jnp.sum(px, axis=-1, keepdims=True) + jnp.sum(pc, axis=-1, keepdims=True)
        o = (jnp.dot(px.astype(BF16), vx, preferred_element_type=F32)
             + jnp.dot(pc.astype(BF16), vc, preferred_element_type=F32))
        o_ref[0, :, g * HEAD_DIM:(g + 1) * HEAD_DIM] = (o * (1.0 / denom)).astype(BF16)


def _attention(q, k_x, v_x, k_c, v_c, tq=512):
    b, t, _ = q.shape
    s = k_c.shape[1]
    qw = Q_PER_KV * HEAD_DIM
    kv_map = lambda i, h, j: (i, 0, h)
    return pl.pallas_call(
        _attn_kernel,
        grid=(b, N_KV_HEADS, t // tq),
        in_specs=[
            pl.BlockSpec((1, tq, qw), lambda i, h, j: (i, j, h)),
            pl.BlockSpec((1, t, HEAD_DIM), kv_map),
            pl.BlockSpec((1, s, HEAD_DIM), kv_map),
            pl.BlockSpec((1, t, HEAD_DIM), kv_map),
            pl.BlockSpec((1, s, HEAD_DIM), kv_map),
        ],
        out_specs=pl.BlockSpec((1, tq, qw), lambda i, h, j: (i, j, h)),
        out_shape=jax.ShapeDtypeStruct((b, t, ATTN_W), BF16),
        compiler_params=_params("arbitrary", "arbitrary", "arbitrary"),
        name="attention",
    )(q, k_x, k_c, v_x, v_c)


def _out_proj_kernel(x_ref, pm_ref, at_ref, mod_ref, w_ref, o_ref):
    mix = (jnp.dot(pm_ref[0], w_ref[:POOL_W, :], preferred_element_type=F32)
           + jnp.dot(at_ref[0], w_ref[POOL_W:, :], preferred_element_type=F32))
    o_ref[0] = x_ref[0] + mod_ref[0, 2:3, :] * mix


def _out_proj(x, pool_mix, attn, mod3, w_out_bf, tm=512):
    b, t, _ = x.shape
    row = lambda i, j: (i, j, 0)
    return pl.pallas_call(
        _out_proj_kernel,
        grid=(b, t // tm),
        in_specs=[
            pl.BlockSpec((1, tm, D_MODEL), row),
            pl.BlockSpec((1, tm, POOL_W), row),
            pl.BlockSpec((1, tm, ATTN_W), row),
            pl.BlockSpec((1, N_MOD, D_MODEL), lambda i, j: (i, 0, 0)),
            pl.BlockSpec((MIX_W, D_MODEL), lambda i, j: (0, 0)),
        ],
        out_specs=pl.BlockSpec((1, tm, D_MODEL), row),
        out_shape=jax.ShapeDtypeStruct(x.shape, F32),
        compiler_params=_params("arbitrary", "arbitrary"),
        name="out_proj",
    )(x, pool_mix, attn, mod3, w_out_bf)


def _ffn_kernel(x_ref, mod_ref, nw_ref, wg_ref, wu_ref, wd_ref, fn_ref, o_ref, h_ref):
    f = pl.program_id(2)

    @pl.when(f == 0)
    def _():
        h_ref[...] = _norm_modulate(x_ref[0], nw_ref[...], mod_ref, 3).astype(BF16)

    h = h_ref[...]
    gate = jnp.dot(h, wg_ref[...], preferred_element_type=F32)
    up = jnp.dot(h, wu_ref[...], preferred_element_type=F32)
    act = (gate * jax.nn.sigmoid(gate) * up).astype(BF16)
    part = jnp.dot(act, wd_ref[...], preferred_element_type=F32)

    @pl.when(f == 0)
    def _():
        o_ref[0] = part

    @pl.when(f > 0)
    def _():
        o_ref[0] += part

    @pl.when(f == pl.num_programs(2) - 1)
    def _():
        y = x_ref[0] + mod_ref[0, 5:6, :] * o_ref[0]
        o_ref[0] = _rms(y) * fn_ref[...]


def _ffn(x, mod3, norm_w, wg_bf, wu_bf, wd_bf, final_norm, tm=1024, tf=256):
    b, t, _ = x.shape
    d_ff = wg_bf.shape[1]
    row = lambda i, j, f: (i, j, 0)
    const2 = lambda i, j, f: (0, 0)
    return pl.pallas_call(
        _ffn_kernel,
        grid=(b, t // tm, d_ff // tf),
        in_specs=[
            pl.BlockSpec((1, tm, D_MODEL), row),
            pl.BlockSpec((1, N_MOD, D_MODEL), lambda i, j, f: (i, 0, 0)),
            pl.BlockSpec((1, D_MODEL), const2),
            pl.BlockSpec((D_MODEL, tf), lambda i, j, f: (0, f)),
            pl.BlockSpec((D_MODEL, tf), lambda i, j, f: (0, f)),
            pl.BlockSpec((tf, D_MODEL), lambda i, j, f: (f, 0)),
            pl.BlockSpec((1, D_MODEL), const2),
        ],
        out_specs=pl.BlockSpec((1, tm, D_MODEL), row),
        out_shape=jax.ShapeDtypeStruct(x.shape, F32),
        scratch_shapes=[pltpu.VMEM((tm, D_MODEL), BF16)],
        compiler_params=_params("arbitrary", "arbitrary", "arbitrary"),
        name="ffn",
    )(x, mod3, norm_w, wg_bf, wu_bf, wd_bf, final_norm)


def _rope_tables(t):
    n_rows = t // GRID_W
    rows = jnp.repeat(jnp.arange(n_rows, dtype=F32), GRID_W)
    cols = jnp.tile(jnp.arange(GRID_W, dtype=F32), n_rows)
    freqs = ROPE_THETA ** (-jnp.arange(0, AXIS_ROT, 2, dtype=F32) / AXIS_ROT)
    ang = jnp.concatenate([rows[:, None] * freqs, cols[:, None] * freqs], axis=-1)
    cos_full = jnp.repeat(jnp.cos(ang), 2, axis=-1)
    sin = jnp.sin(ang)
    sin_signed = jnp.stack([-sin, sin], axis=-1).reshape(t, HEAD_DIM)
    return cos_full, sin_signed


def kernel(x, c, ctx, c_ctx, w_ada, b_ada, norm_mix, norm_ffn, w_in, pool_w, pool_scale,
           q_norm, k_norm, w_out, w_gate, w_up, w_down, final_norm):
    depth = w_ada.shape[0]
    assert depth == 1, "context tokens are only updated between layers; one layer is implemented"
    b, t, _ = x.shape
    cos_full, sin_signed = _rope_tables(t)

    cc = jnp.zeros((MOD_ROWS, D_MODEL), F32).at[:b].set(c).at[MOD_ROWS // 2].set(c_ctx)
    mod3 = _adaln_mod(cc, w_ada[0], b_ada).reshape(MOD_ROWS, N_MOD, D_MODEL)

    w_in_bf = w_in[0].astype(BF16)
    k_c, v_c = _ctx_kv(ctx, mod3, norm_mix, w_in_bf, k_norm)
    p, q, k_x, v_x = _in_proj(x, mod3, norm_mix, w_in_bf, q_norm, k_norm, cos_full, sin_signed)
    pooled = _pool_mix(p, pool_w[0].astype(BF16), pool_scale)
    attn = _attention(q, k_x, v_x, k_c, v_c)
    x1 = _out_proj(x, pooled, attn, mod3, w_out[0].astype(BF16))
    return _ffn(x1, mod3, norm_ffn, w_gate[0].astype(BF16), w_up[0].astype(BF16),
                w_down[0].astype(BF16), final_norm.reshape(1, D_MODEL))
```

```python
import functools
import math

import jax
import jax.numpy as jnp
from jax import lax
from jax.experimental import pallas as pl
from jax.experimental.pallas import tpu as pltpu

D_MODEL = 2048
GRID_W = 64
HEAD_DIM = 128
N_HEADS = 8
N_KV_HEADS = 2
Q_PER_KV = N_HEADS // N_KV_HEADS
ATTN_W = N_HEADS * HEAD_DIM
KV_W = N_KV_HEADS * HEAD_DIM
POOL_WINDOWS = (2, 4, 8, 16)
N_POOL_GROUPS = len(POOL_WINDOWS)
POOL_W = D_MODEL // 2
POOL_GROUP_W = POOL_W // N_POOL_GROUPS
MIX_W = POOL_W + ATTN_W
PROJ_W = POOL_W + ATTN_W + 2 * KV_W
ROPE_THETA = 10000.0
AXIS_ROT = HEAD_DIM // 2
EPS = 1e-6
N_MOD = 6
MOD_ROWS = 16
POOL_PAD = 8

F32 = jnp.float32
BF16 = jnp.bfloat16

VMEM_LIMIT_BYTES = 60 * 1024 * 1024


def _params(*sem):
    return pltpu.CompilerParams(dimension_semantics=sem, vmem_limit_bytes=VMEM_LIMIT_BYTES)


def _rms(x):
    return x * lax.rsqrt(jnp.mean(x * x, axis=-1, keepdims=True) + EPS)


def _adaln_kernel(c_ref, w_ref, b_ref, o_ref):
    c = c_ref[...]
    a = c * jax.nn.sigmoid(c)
    o_ref[...] = jnp.dot(a, w_ref[...], preferred_element_type=F32) + b_ref[...]


def _adaln_mod(cc, w_ada, b_ada, tn=1024):
    n = w_ada.shape[1]
    return pl.pallas_call(
        _adaln_kernel,
        grid=(n // tn,),
        in_specs=[
            pl.BlockSpec((MOD_ROWS, D_MODEL), lambda j: (0, 0)),
            pl.BlockSpec((D_MODEL, tn), lambda j: (0, j)),
            pl.BlockSpec((1, tn), lambda j: (0, j)),
        ],
        out_specs=pl.BlockSpec((MOD_ROWS, tn), lambda j: (0, j)),
        out_shape=jax.ShapeDtypeStruct((MOD_ROWS, n), F32),
        compiler_params=_params("arbitrary"),
        name="adaln_mod",
    )(cc, w_ada, b_ada)


def _norm_modulate(x, nw, mod_ref, shift_idx):
    sh = mod_ref[0, shift_idx:shift_idx + 1, :]
    sc = mod_ref[0, shift_idx + 1:shift_idx + 2, :]
    return (_rms(x) * nw) * (1.0 + sc) + sh


def _rope(x, cos, sin_signed, even_lane):
    partner = jnp.where(even_lane, pltpu.roll(x, HEAD_DIM - 1, axis=1), pltpu.roll(x, 1, axis=1))
    return x * cos + partner * sin_signed


def _ctx_kv_kernel(x_ref, mod_ref, nw_ref, w_ref, kn_ref, k_ref, v_ref):
    h = _norm_modulate(x_ref[0], nw_ref[...], mod_ref, 0)
    kv = jnp.dot(h.astype(BF16), w_ref[...], preferred_element_type=F32)
    kn = kn_ref[...]
    for j in range(N_KV_HEADS):
        kh = kv[:, j * HEAD_DIM:(j + 1) * HEAD_DIM]
        k_ref[0, :, j * HEAD_DIM:(j + 1) * HEAD_DIM] = (_rms(kh) * kn).astype(BF16)
    v_ref[0] = kv[:, KV_W:].astype(BF16)


def _ctx_kv(ctx, mod3, norm_w, w_in_bf, k_norm):
    b, s, _ = ctx.shape
    kv_block = (POOL_W + ATTN_W) // (2 * KV_W)
    return pl.pallas_call(
        _ctx_kv_kernel,
        grid=(b,),
        in_specs=[
            pl.BlockSpec((1, s, D_MODEL), lambda i: (i, 0, 0)),
            pl.BlockSpec((1, N_MOD, D_MODEL), lambda i: (MOD_ROWS // 2, 0, 0)),
            pl.BlockSpec((1, D_MODEL), lambda i: (0, 0)),
            pl.BlockSpec((D_MODEL, 2 * KV_W), lambda i: (0, kv_block)),
            pl.BlockSpec((1, HEAD_DIM), lambda i: (0, 0)),
        ],
        out_specs=[
            pl.BlockSpec((1, s, KV_W), lambda i: (i, 0, 0)),
            pl.BlockSpec((1, s, KV_W), lambda i: (i, 0, 0)),
        ],
        out_shape=[jax.ShapeDtypeStruct((b, s, KV_W), BF16)] * 2,
        compiler_params=_params("arbitrary"),
        name="ctx_kv",
    )(ctx, mod3, norm_w, w_in_bf, k_norm)


def _in_proj_kernel(x_ref, mod_ref, nw_ref, w_ref, qn_ref, kn_ref, cos_ref, sin_ref,
                    p_ref, q_ref, k_ref, v_ref):
    h = _norm_modulate(x_ref[0], nw_ref[...], mod_ref, 0)
    proj = jnp.dot(h.astype(BF16), w_ref[...], preferred_element_type=F32)
    p_ref[0] = proj[:, :POOL_W]
    cos = cos_ref[...]
    sin = sin_ref[...]
    even_lane = (lax.broadcasted_iota(jnp.int32, cos.shape, 1) % 2) == 0
    qn = qn_ref[...] * (1.0 / math.sqrt(HEAD_DIM))
    kn = kn_ref[...]
    for j in range(N_HEADS):
        lo = POOL_W + j * HEAD_DIM
        qh = _rms(proj[:, lo:lo + HEAD_DIM]) * qn
        q_ref[0, :, j * HEAD_DIM:(j + 1) * HEAD_DIM] = _rope(qh, cos, sin, even_lane).astype(BF16)
    for j in range(N_KV_HEADS):
        lo = POOL_W + ATTN_W + j * HEAD_DIM
        kh = _rms(proj[:, lo:lo + HEAD_DIM]) * kn
        k_ref[0, :, j * HEAD_DIM:(j + 1) * HEAD_DIM] = _rope(kh, cos, sin, even_lane).astype(BF16)
    v_ref[0] = proj[:, POOL_W + ATTN_W + KV_W:].astype(BF16)


def _in_proj(x, mod3, norm_w, w_in_bf, q_norm, k_norm, cos_full, sin_signed, tm=512):
    b, t, _ = x.shape
    row = lambda i, j: (i, j, 0)
    const2 = lambda i, j: (0, 0)
    return pl.pallas_call(
        _in_proj_kernel,
        grid=(b, t // tm),
        in_specs=[
            pl.BlockSpec((1, tm, D_MODEL), row),
            pl.BlockSpec((1, N_MOD, D_MODEL), lambda i, j: (i, 0, 0)),
            pl.BlockSpec((1, D_MODEL), const2),
            pl.BlockSpec((D_MODEL, PROJ_W), const2),
            pl.BlockSpec((1, HEAD_DIM), const2),
            pl.BlockSpec((1, HEAD_DIM), const2),
            pl.BlockSpec((tm, HEAD_DIM), lambda i, j: (j, 0)),
            pl.BlockSpec((tm, HEAD_DIM), lambda i, j: (j, 0)),
        ],
        out_specs=[
            pl.BlockSpec((1, tm, POOL_W), row),
            pl.BlockSpec((1, tm, ATTN_W), row),
            pl.BlockSpec((1, tm, KV_W), row),
            pl.BlockSpec((1, tm, KV_W), row),
        ],
        out_shape=[
            jax.ShapeDtypeStruct((b, t, POOL_W), F32),
            jax.ShapeDtypeStruct((b, t, ATTN_W), BF16),
            jax.ShapeDtypeStruct((b, t, KV_W), BF16),
            jax.ShapeDtypeStruct((b, t, KV_W), BF16),
        ],
        compiler_params=_params("arbitrary", "arbitrary"),
        name="in_proj",
    )(x, mod3, norm_w, w_in_bf, q_norm, k_norm, cos_full, sin_signed)


def _pool_kernel(p_ref, w_ref, s_ref, o_ref):
    t = p_ref.shape[1]
    n = t + 2 * POOL_PAD
    pos = lax.broadcasted_iota(jnp.int32, (t, 1), 0)
    zeros = jnp.zeros((POOL_PAD, POOL_GROUP_W), F32)
    for g, win in enumerate(POOL_WINDOWS):
        half = win // 2
        u = p_ref[0, :, g * POOL_GROUP_W:(g + 1) * POOL_GROUP_W]
        f = jnp.concatenate([zeros, u, zeros], axis=0)
        step = 1
        while step < win:
            f = f + pltpu.roll(f, n - step, axis=0)
            step *= 2
        if half != POOL_PAD:
            f = pltpu.roll(f, n - (POOL_PAD - half), axis=0)
        count = jnp.minimum(pos + half, t) - jnp.maximum(pos - half, 0)
        pooled = f[:t] * (1.0 / count.astype(F32)) - u
        mixed = jnp.dot(pooled.astype(BF16), w_ref[g], preferred_element_type=F32)
        o_ref[0, :, g * POOL_GROUP_W:(g + 1) * POOL_GROUP_W] = (
            mixed * s_ref[:, g * POOL_GROUP_W:(g + 1) * POOL_GROUP_W]).astype(BF16)


def _pool_mix(p, pool_w_bf, pool_scale):
    b, t, _ = p.shape
    return pl.pallas_call(
        _pool_kernel,
        grid=(b,),
        in_specs=[
            pl.BlockSpec((1, t, POOL_W), lambda i: (i, 0, 0)),
            pl.BlockSpec((N_POOL_GROUPS, POOL_GROUP_W, POOL_GROUP_W), lambda i: (0, 0, 0)),
            pl.BlockSpec((1, POOL_W), lambda i: (0, 0)),
        ],
        out_specs=pl.BlockSpec((1, t, POOL_W), lambda i: (i, 0, 0)),
        out_shape=jax.ShapeDtypeStruct((b, t, POOL_W), BF16),
        compiler_params=_params("arbitrary"),
        name="pool_mix",
    )(p, pool_w_bf, pool_scale)


def _attn_kernel(q_ref, kx_ref, kc_ref, vx_ref, vc_ref, o_ref):
    kx = kx_ref[0]
    kc = kc_ref[0]
    vx = vx_ref[0]
    vc = vc_ref[0]
    nt = (((1,), (1,)), ((), ()))
    for g in range(Q_PER_KV):
        q = q_ref[0, :, g * HEAD_DIM:(g + 1) * HEAD_DIM]
        sx = lax.dot_general(q, kx, nt, preferred_element_type=F32)
        sc = lax.dot_general(q, kc, nt, preferred_element_type=F32)
        m = jnp.maximum(jnp.max(sx, axis=-1, keepdims=True), jnp.max(sc, axis=-1, keepdims=True))
        px = jnp.exp(sx - m)
        pc = jnp.exp(sc - m)
        denom = jnp.sum(px, axis=-1, keepdims=True) + jnp.sum(pc, axis=-1, keepdims=True)
        o = (jnp.dot(px.astype(BF16), vx, preferred_element_type=F32)
             + jnp.dot(pc.astype(BF16), vc, preferred_element_type=F32))
        o_ref[0, :, g * HEAD_DIM:(g + 1) * HEAD_DIM] = (o * (1.0 / denom)).astype(BF16)


def _attention(q, k_x, v_x, k_c, v_c, tq=512):
    b, t, _ = q.shape
    s = k_c.shape[1]
    qw = Q_PER_KV * HEAD_DIM
    kv_map = lambda i, h, j: (i, 0, h)
    return pl.pallas_call(
        _attn_kernel,
        grid=(b, N_KV_HEADS, t // tq),
        in_specs=[
            pl.BlockSpec((1, tq, qw), lambda i, h, j: (i, j, h)),
            pl.BlockSpec((1, t, HEAD_DIM), kv_map),
            pl.BlockSpec((1, s, HEAD_DIM), kv_map),
            pl.BlockSpec((1, t, HEAD_DIM), kv_map),
            pl.BlockSpec((1, s, HEAD_DIM), kv_map),
        ],
        out_specs=pl.BlockSpec((1, tq, qw), lambda i, h, j: (i, j, h)),
        out_shape=jax.ShapeDtypeStruct((b, t, ATTN_W), BF16),
        compiler_params=_params("arbitrary", "arbitrary", "arbitrary"),
        name="attention",
    )(q, k_x, k_c, v_x, v_c)


OUT_SUB = 256


def _out_proj_kernel(x_ref, pm_ref, at_ref, mod_ref, nw_ref, w_ref, o_ref, h_ref):
    gate = mod_ref[0, 2:3, :]
    nw = nw_ref[...]
    for r in range(x_ref.shape[1] // OUT_SUB):
        rows = slice(r * OUT_SUB, (r + 1) * OUT_SUB)
        mix = (jnp.dot(pm_ref[0, rows, :], w_ref[:POOL_W, :], preferred_element_type=F32)
               + jnp.dot(at_ref[0, rows, :], w_ref[POOL_W:, :], preferred_element_type=F32))
        x1 = x_ref[0, rows, :] + gate * mix
        o_ref[0, rows, :] = x1
        h_ref[0, rows, :] = _norm_modulate(x1, nw, mod_ref, 3).astype(BF16)


def _out_proj(x, pool_mix, attn, mod3, norm_w, w_out_bf, tm=512):
    b, t, _ = x.shape
    row = lambda i, j: (i, j, 0)
    return pl.pallas_call(
        _out_proj_kernel,
        grid=(b, t // tm),
        in_specs=[
            pl.BlockSpec((1, tm, D_MODEL), row),
            pl.BlockSpec((1, tm, POOL_W), row),
            pl.BlockSpec((1, tm, ATTN_W), row),
            pl.BlockSpec((1, N_MOD, D_MODEL), lambda i, j: (i, 0, 0)),
            pl.BlockSpec((1, D_MODEL), lambda i, j: (0, 0)),
            pl.BlockSpec((MIX_W, D_MODEL), lambda i, j: (0, 0)),
        ],
        out_specs=[pl.BlockSpec((1, tm, D_MODEL), row), pl.BlockSpec((1, tm, D_MODEL), row)],
        out_shape=[jax.ShapeDtypeStruct(x.shape, F32), jax.ShapeDtypeStruct(x.shape, BF16)],
        compiler_params=_params("arbitrary", "arbitrary"),
        name="out_proj",
    )(x, pool_mix, attn, mod3, norm_w, w_out_bf)


def _ffn_kernel(h_ref, x_ref, mod_ref, wg_ref, wu_ref, wd_ref, fn_ref, o_ref, act_ref, *, n_f, tf, tn):
    s = pl.program_id(2)

    @pl.when(s < n_f)
    def _():
        h = h_ref[0]
        gate = jnp.dot(h, wg_ref[...], preferred_element_type=F32)
        up = jnp.dot(h, wu_ref[...], preferred_element_type=F32)
        act_ref[s] = (gate * jax.nn.sigmoid(gate) * up).astype(BF16)

    @pl.when(s >= n_f)
    def _():
        y = jnp.dot(act_ref[0], wd_ref[0:tf, :], preferred_element_type=F32)
        for f in range(1, n_f):
            y += jnp.dot(act_ref[f], wd_ref[f * tf:(f + 1) * tf, :], preferred_element_type=F32)
        x1 = x_ref[0]
        for j in range(D_MODEL // tn):
            @pl.when(s - n_f == j)
            def _():
                cols = slice(j * tn, (j + 1) * tn)
                o_ref[0, :, cols] = x1 + mod_ref[0, 5:6, cols] * y

    @pl.when(s == pl.num_programs(2) - 1)
    def _():
        o_ref[0] = _rms(o_ref[0]) * fn_ref[...]


def _ffn(h, x1, mod3, wg_bf, wu_bf, wd_bf, final_norm, tm=1024, tf=512, tn=256):
    b, t, _ = h.shape
    d_ff = wg_bf.shape[1]
    n_f = d_ff // tf
    n_n = D_MODEL // tn
    row = lambda i, j, s: (i, j, 0)
    up_blk = lambda i, j, s: (0, jnp.minimum(s, n_f - 1))
    down_blk = lambda i, j, s: (0, jnp.maximum(s - n_f, 0))
    return pl.pallas_call(
        functools.partial(_ffn_kernel, n_f=n_f, tf=tf, tn=tn),
        grid=(b, t // tm, n_f + n_n),
        in_specs=[
            pl.BlockSpec((1, tm, D_MODEL), row),
            pl.BlockSpec((1, tm, tn), lambda i, j, s: (i, j, jnp.maximum(s - n_f, 0))),
            pl.BlockSpec((1, N_MOD, D_MODEL), lambda i, j, s: (i, 0, 0)),
            pl.BlockSpec((D_MODEL, tf), up_blk),
            pl.BlockSpec((D_MODEL, tf), up_blk),
            pl.BlockSpec((d_ff, tn), down_blk),
            pl.BlockSpec((1, D_MODEL), lambda i, j, s: (0, 0)),
        ],
        out_specs=pl.BlockSpec((1, tm, D_MODEL), row),
        out_shape=jax.ShapeDtypeStruct(x1.shape, F32),
        scratch_shapes=[pltpu.VMEM((n_f, tm, tf), BF16)],
        compiler_params=_params("arbitrary", "arbitrary", "arbitrary"),
        name="ffn",
    )(h, x1, mod3, wg_bf, wu_bf, wd_bf, final_norm)


def _rope_tables(t):
    n_rows = t // GRID_W
    rows = jnp.repeat(jnp.arange(n_rows, dtype=F32), GRID_W)
    cols = jnp.tile(jnp.arange(GRID_W, dtype=F32), n_rows)
    freqs = ROPE_THETA ** (-jnp.arange(0, AXIS_ROT, 2, dtype=F32) / AXIS_ROT)
    ang = jnp.concatenate([rows[:, None] * freqs, cols[:, None] * freqs], axis=-1)
    cos_full = jnp.repeat(jnp.cos(ang), 2, axis=-1)
    sin = jnp.sin(ang)
    sin_signed = jnp.stack([-sin, sin], axis=-1).reshape(t, HEAD_DIM)
    return cos_full, sin_signed


def kernel(x, c, ctx, c_ctx, w_ada, b_ada, norm_mix, norm_ffn, w_in, pool_w, pool_scale,
           q_norm, k_norm, w_out, w_gate, w_up, w_down, final_norm):
    depth = w_ada.shape[0]
    assert depth == 1, "context tokens are only updated between layers; one layer is implemented"
    b, t, _ = x.shape
    cos_full, sin_signed = _rope_tables(t)

    cc = jnp.zeros((MOD_ROWS, D_MODEL), F32).at[:b].set(c).at[MOD_ROWS // 2].set(c_ctx)
    mod3 = _adaln_mod(cc, w_ada[0], b_ada).reshape(MOD_ROWS, N_MOD, D_MODEL)

    w_in_bf = w_in[0].astype(BF16)
    k_c, v_c = _ctx_kv(ctx, mod3, norm_mix, w_in_bf, k_norm)
    p, q, k_x, v_x = _in_proj(x, mod3, norm_mix, w_in_bf, q_norm, k_norm, cos_full, sin_signed)
    pooled = _pool_mix(p, pool_w[0].astype(BF16), pool_scale)
    attn = _attention(q, k_x, v_x, k_c, v_c)
    x1, hf = _out_proj(x, pooled, attn, mod3, norm_ffn, w_out[0].astype(BF16))
    return _ffn(hf, x1, mod3, w_gate[0].astype(BF16), w_up[0].astype(BF16),
                w_down[0].astype(BF16), final_norm.reshape(1, D_MODEL))
```

```python
import functools
import math

import jax
import jax.numpy as jnp
from jax import lax
from jax.experimental import pallas as pl
from jax.experimental.pallas import tpu as pltpu

D_MODEL = 2048
GRID_W = 64
HEAD_DIM = 128
N_HEADS = 8
N_KV_HEADS = 2
Q_PER_KV = N_HEADS // N_KV_HEADS
ATTN_W = N_HEADS * HEAD_DIM
KV_W = N_KV_HEADS * HEAD_DIM
POOL_WINDOWS = (2, 4, 8, 16)
N_POOL_GROUPS = len(POOL_WINDOWS)
POOL_W = D_MODEL // 2
POOL_GROUP_W = POOL_W // N_POOL_GROUPS
MIX_W = POOL_W + ATTN_W
PROJ_W = POOL_W + ATTN_W + 2 * KV_W
ROPE_THETA = 10000.0
AXIS_ROT = HEAD_DIM // 2
EPS = 1e-6
N_MOD = 6
MOD_ROWS = 16
POOL_PAD = 8
ROW_SUB = 256
ATTN_SUB = 256
Q_SCALE = math.log2(math.e) / math.sqrt(HEAD_DIM)

F32 = jnp.float32
BF16 = jnp.bfloat16

VMEM_LIMIT_BYTES = 60 * 1024 * 1024


def _params(*sem):
    return pltpu.CompilerParams(dimension_semantics=sem, vmem_limit_bytes=VMEM_LIMIT_BYTES)


def _rms(x):
    return x * lax.rsqrt(jnp.mean(x * x, axis=-1, keepdims=True) + EPS)


def _adaln_kernel(c_ref, w_ref, b_ref, o_ref):
    c = c_ref[...]
    a = c * jax.nn.sigmoid(c)
    o_ref[...] = jnp.dot(a, w_ref[...], preferred_element_type=F32) + b_ref[...]


def _adaln_mod(cc, w_ada, b_ada, tn=1024):
    n = w_ada.shape[1]
    return pl.pallas_call(
        _adaln_kernel,
        grid=(n // tn,),
        in_specs=[
            pl.BlockSpec((MOD_ROWS, D_MODEL), lambda j: (0, 0)),
            pl.BlockSpec((D_MODEL, tn), lambda j: (0, j)),
            pl.BlockSpec((1, tn), lambda j: (0, j)),
        ],
        out_specs=pl.BlockSpec((MOD_ROWS, tn), lambda j: (0, j)),
        out_shape=jax.ShapeDtypeStruct((MOD_ROWS, n), F32),
        compiler_params=_params("arbitrary"),
        name="adaln_mod",
    )(cc, w_ada, b_ada)


def _norm_modulate(x, nw, mod_ref, shift_idx):
    sh = mod_ref[0, shift_idx:shift_idx + 1, :]
    sc = mod_ref[0, shift_idx + 1:shift_idx + 2, :]
    return (_rms(x) * nw) * (1.0 + sc) + sh


def _rope(x, cos, sin_signed, even_lane):
    partner = jnp.where(even_lane, pltpu.roll(x, HEAD_DIM - 1, axis=1), pltpu.roll(x, 1, axis=1))
    return x * cos + partner * sin_signed


def _ctx_kv_kernel(x_ref, mod_ref, nw_ref, w_ref, kn_ref, k_ref, v_ref):
    h = _norm_modulate(x_ref[0], nw_ref[...], mod_ref, 0)
    kv = jnp.dot(h.astype(BF16), w_ref[...], preferred_element_type=F32)
    kn = kn_ref[...]
    for j in range(N_KV_HEADS):
        kh = kv[:, j * HEAD_DIM:(j + 1) * HEAD_DIM]
        k_ref[0, :, j * HEAD_DIM:(j + 1) * HEAD_DIM] = (_rms(kh) * kn).astype(BF16)
    v_ref[0] = kv[:, KV_W:].astype(BF16)


def _ctx_kv(ctx, mod3, norm_w, w_in_bf, k_norm):
    b, s, _ = ctx.shape
    kv_block = (POOL_W + ATTN_W) // (2 * KV_W)
    return pl.pallas_call(
        _ctx_kv_kernel,
        grid=(b,),
        in_specs=[
            pl.BlockSpec((1, s, D_MODEL), lambda i: (i, 0, 0)),
            pl.BlockSpec((1, N_MOD, D_MODEL), lambda i: (MOD_ROWS // 2, 0, 0)),
            pl.BlockSpec((1, D_MODEL), lambda i: (0, 0)),
            pl.BlockSpec((D_MODEL, 2 * KV_W), lambda i: (0, kv_block)),
            pl.BlockSpec((1, HEAD_DIM), lambda i: (0, 0)),
        ],
        out_specs=[
            pl.BlockSpec((1, s, KV_W), lambda i: (i, 0, 0)),
            pl.BlockSpec((1, s, KV_W), lambda i: (i, 0, 0)),
        ],
        out_shape=[jax.ShapeDtypeStruct((b, s, KV_W), BF16)] * 2,
        compiler_params=_params("arbitrary"),
        name="ctx_kv",
    )(ctx, mod3, norm_w, w_in_bf, k_norm)


def _in_proj_kernel(x_ref, mod_ref, nw_ref, w_ref, qn_ref, kn_ref, cos_ref, sin_ref,
                    p_ref, q_ref, k_ref, v_ref):
    nw = nw_ref[...]
    qn = qn_ref[...] * Q_SCALE
    kn = kn_ref[...]
    even_lane = (lax.broadcasted_iota(jnp.int32, (ROW_SUB, HEAD_DIM), 1) % 2) == 0
    for r in range(x_ref.shape[1] // ROW_SUB):
        rows = slice(r * ROW_SUB, (r + 1) * ROW_SUB)
        h = _norm_modulate(x_ref[0, rows, :], nw, mod_ref, 0)
        proj = jnp.dot(h.astype(BF16), w_ref[...], preferred_element_type=F32)
        p_ref[0, rows, :] = proj[:, :POOL_W]
        cos = cos_ref[rows, :]
        sin = sin_ref[rows, :]
        for j in range(N_HEADS):
            lo = POOL_W + j * HEAD_DIM
            qh = _rms(proj[:, lo:lo + HEAD_DIM]) * qn
            q_ref[0, rows, j * HEAD_DIM:(j + 1) * HEAD_DIM] = _rope(qh, cos, sin, even_lane).astype(BF16)
        for j in range(N_KV_HEADS):
            lo = POOL_W + ATTN_W + j * HEAD_DIM
            kh = _rms(proj[:, lo:lo + HEAD_DIM]) * kn
            k_ref[0, rows, j * HEAD_DIM:(j + 1) * HEAD_DIM] = _rope(kh, cos, sin, even_lane).astype(BF16)
        v_ref[0, rows, :] = proj[:, POOL_W + ATTN_W + KV_W:].astype(BF16)


def _in_proj(x, mod3, norm_w, w_in_bf, q_norm, k_norm, cos_full, sin_signed, tm=512):
    b, t, _ = x.shape
    row = lambda i, j: (i, j, 0)
    const2 = lambda i, j: (0, 0)
    return pl.pallas_call(
        _in_proj_kernel,
        grid=(b, t // tm),
        in_specs=[
            pl.BlockSpec((1, tm, D_MODEL), row),
            pl.BlockSpec((1, N_MOD, D_MODEL), lambda i, j: (i, 0, 0)),
            pl.BlockSpec((1, D_MODEL), const2),
            pl.BlockSpec((D_MODEL, PROJ_W), const2),
            pl.BlockSpec((1, HEAD_DIM), const2),
            pl.BlockSpec((1, HEAD_DIM), const2),
            pl.BlockSpec((tm, HEAD_DIM), lambda i, j: (j, 0)),
            pl.BlockSpec((tm, HEAD_DIM), lambda i, j: (j, 0)),
        ],
        out_specs=[
            pl.BlockSpec((1, tm, POOL_W), row),
            pl.BlockSpec((1, tm, ATTN_W), row),
            pl.BlockSpec((1, tm, KV_W), row),
            pl.BlockSpec((1, tm, KV_W), row),
        ],
        out_shape=[
            jax.ShapeDtypeStruct((b, t, POOL_W), F32),
            jax.ShapeDtypeStruct((b, t, ATTN_W), BF16),
            jax.ShapeDtypeStruct((b, t, KV_W), BF16),
            jax.ShapeDtypeStruct((b, t, KV_W), BF16),
        ],
        compiler_params=_params("arbitrary", "arbitrary"),
        name="in_proj",
    )(x, mod3, norm_w, w_in_bf, q_norm, k_norm, cos_full, sin_signed)


def _pool_kernel(p_ref, w_ref, s_ref, o_ref):
    t = p_ref.shape[1]
    n = t + 2 * POOL_PAD
    pos = lax.broadcasted_iota(jnp.int32, (t, 1), 0)
    zeros = jnp.zeros((POOL_PAD, POOL_GROUP_W), F32)
    for g, win in enumerate(POOL_WINDOWS):
        half = win // 2
        u = p_ref[0, :, g * POOL_GROUP_W:(g + 1) * POOL_GROUP_W]
        f = jnp.concatenate([zeros, u, zeros], axis=0)
        step = 1
        while step < win:
            f = f + pltpu.roll(f, n - step, axis=0)
            step *= 2
        if half != POOL_PAD:
            f = pltpu.roll(f, n - (POOL_PAD - half), axis=0)
        count = jnp.minimum(pos + half, t) - jnp.maximum(pos - half, 0)
        pooled = f[:t] * (1.0 / count.astype(F32)) - u
        mixed = jnp.dot(pooled.astype(BF16), w_ref[g], preferred_element_type=F32)
        o_ref[0, :, g * POOL_GROUP_W:(g + 1) * POOL_GROUP_W] = (
            mixed * s_ref[:, g * POOL_GROUP_W:(g + 1) * POOL_GROUP_W]).astype(BF16)


def _pool_mix(p, pool_w_bf, pool_scale):
    b, t, _ = p.shape
    return pl.pallas_call(
        _pool_kernel,
        grid=(b,),
        in_specs=[
            pl.BlockSpec((1, t, POOL_W), lambda i: (i, 0, 0)),
            pl.BlockSpec((N_POOL_GROUPS, POOL_GROUP_W, POOL_GROUP_W), lambda i: (0, 0, 0)),
            pl.BlockSpec((1, POOL_W), lambda i: (0, 0)),
        ],
        out_specs=pl.BlockSpec((1, t, POOL_W), lambda i: (i, 0, 0)),
        out_shape=jax.ShapeDtypeStruct((b, t, POOL_W), BF16),
        compiler_params=_params("arbitrary"),
        name="pool_mix",
    )(p, pool_w_bf, pool_scale)


def _attn_kernel(q_ref, kx_ref, kc_ref, vx_ref, vc_ref, o_ref):
    kx = kx_ref[0]
    kc = kc_ref[0]
    vx = vx_ref[0]
    vc = vc_ref[0]
    nt = (((1,), (1,)), ((), ()))
    for r in range(q_ref.shape[1] // ATTN_SUB):
        rows = slice(r * ATTN_SUB, (r + 1) * ATTN_SUB)
        for g in range(Q_PER_KV):
            cols = slice(g * HEAD_DIM, (g + 1) * HEAD_DIM)
            q = q_ref[0, rows, cols]
            sx = lax.dot_general(q, kx, nt, preferred_element_type=F32)
            sc = lax.dot_general(q, kc, nt, preferred_element_type=F32)
            m = jnp.maximum(jnp.max(sx, axis=-1, keepdims=True), jnp.max(sc, axis=-1, keepdims=True))
            px = jnp.exp2(sx - m)
            pc = jnp.exp2(sc - m)
            denom = jnp.sum(px, axis=-1, keepdims=True) + jnp.sum(pc, axis=-1, keepdims=True)
            o = (jnp.dot(px.astype(BF16), vx, preferred_element_type=F32)
                 + jnp.dot(pc.astype(BF16), vc, preferred_element_type=F32))
            o_ref[0, rows, cols] = (o * (1.0 / denom)).astype(BF16)


def _attention(q, k_x, v_x, k_c, v_c, tq=1024):
    b, t, _ = q.shape
    s = k_c.shape[1]
    qw = Q_PER_KV * HEAD_DIM
    kv_map = lambda i, h, j: (i, 0, h)
    return pl.pallas_call(
        _attn_kernel,
        grid=(b, N_KV_HEADS, t // tq),
        in_specs=[
            pl.BlockSpec((1, tq, qw), lambda i, h, j: (i, j, h)),
            pl.BlockSpec((1, t, HEAD_DIM), kv_map),
            pl.BlockSpec((1, s, HEAD_DIM), kv_map),
            pl.BlockSpec((1, t, HEAD_DIM), kv_map),
            pl.BlockSpec((1, s, HEAD_DIM), kv_map),
        ],
        out_specs=pl.BlockSpec((1, tq, qw), lambda i, h, j: (i, j, h)),
        out_shape=jax.ShapeDtypeStruct((b, t, ATTN_W), BF16),
        compiler_params=_params("arbitrary", "arbitrary", "arbitrary"),
        name="attention",
    )(q, k_x, k_c, v_x, v_c)


def _out_proj_kernel(x_ref, pm_ref, at_ref, mod_ref, nw_ref, w_ref, o_ref, h_ref):
    gate = mod_ref[0, 2:3, :]
    nw = nw_ref[...]
    for r in range(x_ref.shape[1] // ROW_SUB):
        rows = slice(r * ROW_SUB, (r + 1) * ROW_SUB)
        mix = (jnp.dot(pm_ref[0, rows, :], w_ref[:POOL_W, :], preferred_element_type=F32)
               + jnp.dot(at_ref[0, rows, :], w_ref[POOL_W:, :], preferred_element_type=F32))
        x1 = x_ref[0, rows, :] + gate * mix
        o_ref[0, rows, :] = x1
        h_ref[0, rows, :] = _norm_modulate(x1, nw, mod_ref, 3).astype(BF16)


def _out_proj(x, pool_mix, attn, mod3, norm_w, w_out_bf, tm=512):
    b, t, _ = x.shape
    row = lambda i, j: (i, j, 0)
    return pl.pallas_call(
        _out_proj_kernel,
        grid=(b, t // tm),
        in_specs=[
            pl.BlockSpec((1, tm, D_MODEL), row),
            pl.BlockSpec((1, tm, POOL_W), row),
            pl.BlockSpec((1, tm, ATTN_W), row),
            pl.BlockSpec((1, N_MOD, D_MODEL), lambda i, j: (i, 0, 0)),
            pl.BlockSpec((1, D_MODEL), lambda i, j: (0, 0)),
            pl.BlockSpec((MIX_W, D_MODEL), lambda i, j: (0, 0)),
        ],
        out_specs=[pl.BlockSpec((1, tm, D_MODEL), row), pl.BlockSpec((1, tm, D_MODEL), row)],
        out_shape=[jax.ShapeDtypeStruct(x.shape, F32), jax.ShapeDtypeStruct(x.shape, BF16)],
        compiler_params=_params("arbitrary", "arbitrary"),
        name="out_proj",
    )(x, pool_mix, attn, mod3, norm_w, w_out_bf)


def _ffn_kernel(h_ref, x_ref, mod_ref, wg_ref, wu_ref, wd_ref, fn_ref, o_ref, act_ref, *, n_f, tf, tn):
    s = pl.program_id(2)

    @pl.when(s < n_f)
    def _():
        h = h_ref[0]
        gate = jnp.dot(h, wg_ref[...], preferred_element_type=F32)
        up = jnp.dot(h, wu_ref[...], preferred_element_type=F32)
        act_ref[s] = (gate * jax.nn.sigmoid(gate) * up).astype(BF16)

    @pl.when(s >= n_f)
    def _():
        y = jnp.dot(act_ref[0], wd_ref[0:tf, :], preferred_element_type=F32)
        for f in range(1, n_f):
            y += jnp.dot(act_ref[f], wd_ref[f * tf:(f + 1) * tf, :], preferred_element_type=F32)
        x1 = x_ref[0]
        for j in range(D_MODEL // tn):
            @pl.when(s - n_f == j)
            def _():
                cols = slice(j * tn, (j + 1) * tn)
                o_ref[0, :, cols] = x1 + mod_ref[0, 5:6, cols] * y

    @pl.when(s == pl.num_programs(2) - 1)
    def _():
        o_ref[0] = _rms(o_ref[0]) * fn_ref[...]


def _ffn(h, x1, mod3, wg_bf, wu_bf, wd_bf, final_norm, tm=1024, tf=512, tn=256):
    b, t, _ = h.shape
    d_ff = wg_bf.shape[1]
    n_f = d_ff // tf
    n_n = D_MODEL // tn
    row = lambda i, j, s: (i, j, 0)
    up_blk = lambda i, j, s: (0, jnp.minimum(s, n_f - 1))
    down_blk = lambda i, j, s: (0, jnp.maximum(s - n_f, 0))
    return pl.pallas_call(
        functools.partial(_ffn_kernel, n_f=n_f, tf=tf, tn=tn),
        grid=(b, t // tm, n_f + n_n),
        in_specs=[
            pl.BlockSpec((1, tm, D_MODEL), row),
            pl.BlockSpec((1, tm, tn), lambda i, j, s: (i, j, jnp.maximum(s - n_f, 0))),
            pl.BlockSpec((1, N_MOD, D_MODEL), lambda i, j, s: (i, 0, 0)),
            pl.BlockSpec((D_MODEL, tf), up_blk),
            pl.BlockSpec((D_MODEL, tf), up_blk),
            pl.BlockSpec((d_ff, tn), down_blk),
            pl.BlockSpec((1, D_MODEL), lambda i, j, s: (0, 0)),
        ],
        out_specs=pl.BlockSpec((1, tm, D_MODEL), row),
        out_shape=jax.ShapeDtypeStruct(x1.shape, F32),
        scratch_shapes=[pltpu.VMEM((n_f, tm, tf), BF16)],
        compiler_params=_params("arbitrary", "arbitrary", "arbitrary"),
        name="ffn",
    )(h, x1, mod3, wg_bf, wu_bf, wd_bf, final_norm)


def _rope_tables(t):
    n_rows = t // GRID_W
    rows = jnp.repeat(jnp.arange(n_rows, dtype=F32), GRID_W)
    cols = jnp.tile(jnp.arange(GRID_W, dtype=F32), n_rows)
    freqs = ROPE_THETA ** (-jnp.arange(0, AXIS_ROT, 2, dtype=F32) / AXIS_ROT)
    ang = jnp.concatenate([rows[:, None] * freqs, cols[:, None] * freqs], axis=-1)
    cos_full = jnp.repeat(jnp.cos(ang), 2, axis=-1)
    sin = jnp.sin(ang)
    sin_signed = jnp.stack([-sin, sin], axis=-1).reshape(t, HEAD_DIM)
    return cos_full, sin_signed


def kernel(x, c, ctx, c_ctx, w_ada, b_ada, norm_mix, norm_ffn, w_in, pool_w, pool_scale,
           q_norm, k_norm, w_out, w_gate, w_up, w_down, final_norm):
    depth = w_ada.shape[0]
    assert depth == 1, "context tokens are only updated between layers; one layer is implemented"
    b, t, _ = x.shape
    cos_full, sin_signed = _rope_tables(t)

    cc = jnp.zeros((MOD_ROWS, D_MODEL), F32).at[:b].set(c).at[MOD_ROWS // 2].set(c_ctx)
    mod3 = _adaln_mod(cc, w_ada[0], b_ada).reshape(MOD_ROWS, N_MOD, D_MODEL)

    w_in_bf = w_in[0].astype(BF16)
    k_c, v_c = _ctx_kv(ctx, mod3, norm_mix, w_in_bf, k_norm)
    p, q, k_x, v_x = _in_proj(x, mod3, norm_mix, w_in_bf, q_norm, k_norm, cos_full, sin_signed)
    pooled = _pool_mix(p, pool_w[0].astype(BF16), pool_scale)
    attn = _attention(q, k_x, v_x, k_c, v_c)
    x1, hf = _out_proj(x, pooled, attn, mod3, norm_ffn, w_out[0].astype(BF16))
    return _ffn(hf, x1, mod3, w_gate[0].astype(BF16), w_up[0].astype(BF16),
                w_down[0].astype(BF16), final_norm.reshape(1, D_MODEL))
```

```python
import functools
import math

import jax
import jax.numpy as jnp
from jax import lax
from jax.experimental import pallas as pl
from jax.experimental.pallas import tpu as pltpu

D_MODEL = 2048
GRID_W = 64
HEAD_DIM = 128
N_HEADS = 8
N_KV_HEADS = 2
Q_PER_KV = N_HEADS // N_KV_HEADS
ATTN_W = N_HEADS * HEAD_DIM
KV_W = N_KV_HEADS * HEAD_DIM
POOL_WINDOWS = (2, 4, 8, 16)
N_POOL_GROUPS = len(POOL_WINDOWS)
POOL_W = D_MODEL // 2
POOL_GROUP_W = POOL_W // N_POOL_GROUPS
MIX_W = POOL_W + ATTN_W
PROJ_W = POOL_W + ATTN_W + 2 * KV_W
ROPE_THETA = 10000.0
AXIS_ROT = HEAD_DIM // 2
EPS = 1e-6
N_MOD = 6
MOD_ROWS = 16
POOL_PAD = 8
ROW_SUB = 256
MXU_WINDOW_W = 512
ATTN_SUB = 256
Q_SCALE = math.log2(math.e) / math.sqrt(HEAD_DIM)

F32 = jnp.float32
BF16 = jnp.bfloat16

VMEM_LIMIT_BYTES = 60 * 1024 * 1024


def _params(*sem):
    return pltpu.CompilerParams(dimension_semantics=sem, vmem_limit_bytes=VMEM_LIMIT_BYTES)


def _rms(x):
    return x * lax.rsqrt(jnp.mean(x * x, axis=-1, keepdims=True) + EPS)


def _adaln_kernel(c_ref, w_ref, b_ref, o_ref):
    c = c_ref[...]
    a = c * jax.nn.sigmoid(c)
    o_ref[...] = jnp.dot(a, w_ref[...], preferred_element_type=F32) + b_ref[...]


def _adaln_mod(cc, w_ada, b_ada, tn=1024):
    n = w_ada.shape[1]
    return pl.pallas_call(
        _adaln_kernel,
        grid=(n // tn,),
        in_specs=[
            pl.BlockSpec((MOD_ROWS, D_MODEL), lambda j: (0, 0)),
            pl.BlockSpec((D_MODEL, tn), lambda j: (0, j)),
            pl.BlockSpec((1, tn), lambda j: (0, j)),
        ],
        out_specs=pl.BlockSpec((MOD_ROWS, tn), lambda j: (0, j)),
        out_shape=jax.ShapeDtypeStruct((MOD_ROWS, n), F32),
        compiler_params=_params("arbitrary"),
        name="adaln_mod",
    )(cc, w_ada, b_ada)


def _norm_modulate(x, nw, mod_ref, shift_idx):
    sh = mod_ref[0, shift_idx:shift_idx + 1, :]
    sc = mod_ref[0, shift_idx + 1:shift_idx + 2, :]
    return (_rms(x) * nw) * (1.0 + sc) + sh


def _rope(x, cos, sin_signed, even_lane):
    partner = jnp.where(even_lane, pltpu.roll(x, HEAD_DIM - 1, axis=1), pltpu.roll(x, 1, axis=1))
    return x * cos + partner * sin_signed


def _ctx_kv_kernel(x_ref, mod_ref, nw_ref, w_ref, kn_ref, k_ref, v_ref):
    h = _norm_modulate(x_ref[0], nw_ref[...], mod_ref, 0)
    kv = jnp.dot(h.astype(BF16), w_ref[...], preferred_element_type=F32)
    kn = kn_ref[...]
    for j in range(N_KV_HEADS):
        kh = kv[:, j * HEAD_DIM:(j + 1) * HEAD_DIM]
        k_ref[0, :, j * HEAD_DIM:(j + 1) * HEAD_DIM] = (_rms(kh) * kn).astype(BF16)
    v_ref[0] = kv[:, KV_W:].astype(BF16)


def _ctx_kv(ctx, mod3, norm_w, w_in_bf, k_norm):
    b, s, _ = ctx.shape
    kv_block = (POOL_W + ATTN_W) // (2 * KV_W)
    return pl.pallas_call(
        _ctx_kv_kernel,
        grid=(b,),
        in_specs=[
            pl.BlockSpec((1, s, D_MODEL), lambda i: (i, 0, 0)),
            pl.BlockSpec((1, N_MOD, D_MODEL), lambda i: (MOD_ROWS // 2, 0, 0)),
            pl.BlockSpec((1, D_MODEL), lambda i: (0, 0)),
            pl.BlockSpec((D_MODEL, 2 * KV_W), lambda i: (0, kv_block)),
            pl.BlockSpec((1, HEAD_DIM), lambda i: (0, 0)),
        ],
        out_specs=[
            pl.BlockSpec((1, s, KV_W), lambda i: (i, 0, 0)),
            pl.BlockSpec((1, s, KV_W), lambda i: (i, 0, 0)),
        ],
        out_shape=[jax.ShapeDtypeStruct((b, s, KV_W), BF16)] * 2,
        compiler_params=_params("arbitrary"),
        name="ctx_kv",
    )(ctx, mod3, norm_w, w_in_bf, k_norm)


def _in_proj_kernel(x_ref, mod_ref, nw_ref, w_ref, qn_ref, kn_ref, cos_ref, sin_ref,
                    p_ref, q_ref, k_ref, v_ref):
    nw = nw_ref[...]
    qn = qn_ref[...] * Q_SCALE
    kn = kn_ref[...]
    even_lane = (lax.broadcasted_iota(jnp.int32, (ROW_SUB, HEAD_DIM), 1) % 2) == 0
    for r in range(x_ref.shape[1] // ROW_SUB):
        rows = slice(r * ROW_SUB, (r + 1) * ROW_SUB)
        h = _norm_modulate(x_ref[0, rows, :], nw, mod_ref, 0)
        proj = jnp.dot(h.astype(BF16), w_ref[...], preferred_element_type=F32)
        p_ref[0, rows, :] = proj[:, :POOL_W]
        cos = cos_ref[rows, :]
        sin = sin_ref[rows, :]
        for j in range(N_HEADS):
            lo = POOL_W + j * HEAD_DIM
            qh = _rms(proj[:, lo:lo + HEAD_DIM]) * qn
            q_ref[0, rows, j * HEAD_DIM:(j + 1) * HEAD_DIM] = _rope(qh, cos, sin, even_lane).astype(BF16)
        for j in range(N_KV_HEADS):
            lo = POOL_W + ATTN_W + j * HEAD_DIM
            kh = _rms(proj[:, lo:lo + HEAD_DIM]) * kn
            k_ref[0, rows, j * HEAD_DIM:(j + 1) * HEAD_DIM] = _rope(kh, cos, sin, even_lane).astype(BF16)
        v_ref[0, rows, :] = proj[:, POOL_W + ATTN_W + KV_W:].astype(BF16)


def _in_proj(x, mod3, norm_w, w_in_bf, q_norm, k_norm, cos_full, sin_signed, tm=1024):
    b, t, _ = x.shape
    row = lambda i, j: (i, j, 0)
    const2 = lambda i, j: (0, 0)
    return pl.pallas_call(
        _in_proj_kernel,
        grid=(b, t // tm),
        in_specs=[
            pl.BlockSpec((1, tm, D_MODEL), row),
            pl.BlockSpec((1, N_MOD, D_MODEL), lambda i, j: (i, 0, 0)),
            pl.BlockSpec((1, D_MODEL), const2),
            pl.BlockSpec((D_MODEL, PROJ_W), const2, pipeline_mode=pl.Buffered(1)),
            pl.BlockSpec((1, HEAD_DIM), const2),
            pl.BlockSpec((1, HEAD_DIM), const2),
            pl.BlockSpec((tm, HEAD_DIM), lambda i, j: (j, 0)),
            pl.BlockSpec((tm, HEAD_DIM), lambda i, j: (j, 0)),
        ],
        out_specs=[
            pl.BlockSpec((1, tm, POOL_W), row),
            pl.BlockSpec((1, tm, ATTN_W), row),
            pl.BlockSpec((1, tm, KV_W), row),
            pl.BlockSpec((1, tm, KV_W), row),
        ],
        out_shape=[
            jax.ShapeDtypeStruct((b, t, POOL_W), F32),
            jax.ShapeDtypeStruct((b, t, ATTN_W), BF16),
            jax.ShapeDtypeStruct((b, t, KV_W), BF16),
            jax.ShapeDtypeStruct((b, t, KV_W), BF16),
        ],
        compiler_params=_params("arbitrary", "arbitrary"),
        name="in_proj",
    )(x, mod3, norm_w, w_in_bf, q_norm, k_norm, cos_full, sin_signed)


def _pool_kernel(p_ref, w_ref, s_ref, o_ref):
    t = p_ref.shape[1]
    n = t + 2 * POOL_PAD
    pos = lax.broadcasted_iota(jnp.int32, (t, 1), 0)
    zeros = jnp.zeros((POOL_PAD, POOL_GROUP_W), F32)
    for g, win in enumerate(POOL_WINDOWS):
        half = win // 2
        u = p_ref[0, :, g * POOL_GROUP_W:(g + 1) * POOL_GROUP_W]
        f = jnp.concatenate([zeros, u, zeros], axis=0)
        step = 1
        while step < win:
            f = f + pltpu.roll(f, n - step, axis=0)
            step *= 2
        if half != POOL_PAD:
            f = pltpu.roll(f, n - (POOL_PAD - half), axis=0)
        count = jnp.minimum(pos + half, t) - jnp.maximum(pos - half, 0)
        pooled = f[:t] * (1.0 / count.astype(F32)) - u
        mixed = jnp.dot(pooled.astype(BF16), w_ref[g], preferred_element_type=F32)
        o_ref[0, :, g * POOL_GROUP_W:(g + 1) * POOL_GROUP_W] = (
            mixed * s_ref[:, g * POOL_GROUP_W:(g + 1) * POOL_GROUP_W]).astype(BF16)


def _pool_mix(p, pool_w_bf, pool_scale):
    b, t, _ = p.shape
    return pl.pallas_call(
        _pool_kernel,
        grid=(b,),
        in_specs=[
            pl.BlockSpec((1, t, POOL_W), lambda i: (i, 0, 0)),
            pl.BlockSpec((N_POOL_GROUPS, POOL_GROUP_W, POOL_GROUP_W), lambda i: (0, 0, 0)),
            pl.BlockSpec((1, POOL_W), lambda i: (0, 0)),
        ],
        out_specs=pl.BlockSpec((1, t, POOL_W), lambda i: (i, 0, 0)),
        out_shape=jax.ShapeDtypeStruct((b, t, POOL_W), BF16),
        compiler_params=_params("arbitrary"),
        name="pool_mix",
    )(p, pool_w_bf, pool_scale)


def _attn_kernel(q_ref, kx_ref, kc_ref, vx_ref, vc_ref, o_ref):
    kx = kx_ref[0]
    kc = kc_ref[0]
    vx = jnp.concatenate([vx_ref[0], jnp.ones(vx_ref.shape[1:], BF16)], axis=1)
    vc = jnp.concatenate([vc_ref[0], jnp.ones(vc_ref.shape[1:], BF16)], axis=1)
    nt = (((1,), (1,)), ((), ()))
    for r in range(q_ref.shape[1] // ATTN_SUB):
        rows = slice(r * ATTN_SUB, (r + 1) * ATTN_SUB)
        for g in range(Q_PER_KV):
            cols = slice(g * HEAD_DIM, (g + 1) * HEAD_DIM)
            q = q_ref[0, rows, cols]
            sx = lax.dot_general(q, kx, nt, preferred_element_type=F32)
            sc = lax.dot_general(q, kc, nt, preferred_element_type=F32)
            m = jnp.maximum(jnp.max(sx, axis=-1, keepdims=True), jnp.max(sc, axis=-1, keepdims=True))
            px = jnp.exp2(sx - m)
            pc = jnp.exp2(sc - m)
            o = (jnp.dot(px.astype(BF16), vx, preferred_element_type=F32)
                 + jnp.dot(pc.astype(BF16), vc, preferred_element_type=F32))
            o_ref[0, rows, cols] = (o[:, :HEAD_DIM] * (1.0 / o[:, HEAD_DIM:])).astype(BF16)


def _attention(q, k_x, v_x, k_c, v_c, tq=1024):
    b, t, _ = q.shape
    s = k_c.shape[1]
    qw = Q_PER_KV * HEAD_DIM
    kv_map = lambda i, h, j: (i, 0, h)
    return pl.pallas_call(
        _attn_kernel,
        grid=(b, N_KV_HEADS, t // tq),
        in_specs=[
            pl.BlockSpec((1, tq, qw), lambda i, h, j: (i, j, h)),
            pl.BlockSpec((1, t, HEAD_DIM), kv_map),
            pl.BlockSpec((1, s, HEAD_DIM), kv_map),
            pl.BlockSpec((1, t, HEAD_DIM), kv_map),
            pl.BlockSpec((1, s, HEAD_DIM), kv_map),
        ],
        out_specs=pl.BlockSpec((1, tq, qw), lambda i, h, j: (i, j, h)),
        out_shape=jax.ShapeDtypeStruct((b, t, ATTN_W), BF16),
        compiler_params=_params("arbitrary", "arbitrary", "arbitrary"),
        name="attention",
    )(q, k_x, k_c, v_x, v_c)


def _out_proj_kernel(x_ref, pm0_ref, pm1_ref, at0_ref, at1_ref, mod_ref, nw_ref, w_ref, o_ref, h_ref):
    gate = mod_ref[0, 2:3, :]
    nw = nw_ref[...]
    for r in range(x_ref.shape[1] // ROW_SUB):
        rows = slice(r * ROW_SUB, (r + 1) * ROW_SUB)
        lhs = jnp.concatenate([ref[0, rows, :] for ref in (pm0_ref, pm1_ref, at0_ref, at1_ref)], axis=1)
        mix = jnp.concatenate([jnp.dot(lhs, w_ref[n], preferred_element_type=F32)
                               for n in range(w_ref.shape[0])], axis=1)
        x1 = x_ref[0, rows, :] + gate * mix
        o_ref[0, rows, :] = x1
        h_ref[0, rows, :] = _norm_modulate(x1, nw, mod_ref, 3).astype(BF16)


def _out_proj(x, pool_mix, attn, mod3, norm_w, w_out_chunks, tm=1024):
    b, t, _ = x.shape
    row = lambda i, j: (i, j, 0)
    half = lambda c: pl.BlockSpec((1, tm, MXU_WINDOW_W), lambda i, j: (i, j, c))
    return pl.pallas_call(
        _out_proj_kernel,
        grid=(b, t // tm),
        in_specs=[
            pl.BlockSpec((1, tm, D_MODEL), row),
            half(0), half(1), half(0), half(1),
            pl.BlockSpec((1, N_MOD, D_MODEL), lambda i, j: (i, 0, 0)),
            pl.BlockSpec((1, D_MODEL), lambda i, j: (0, 0)),
            pl.BlockSpec(w_out_chunks.shape, lambda i, j: (0, 0, 0), pipeline_mode=pl.Buffered(1)),
        ],
        out_specs=[pl.BlockSpec((1, tm, D_MODEL), row), pl.BlockSpec((1, tm, D_MODEL), row)],
        out_shape=[jax.ShapeDtypeStruct(x.shape, F32), jax.ShapeDtypeStruct(x.shape, BF16)],
        compiler_params=_params("arbitrary", "arbitrary"),
        name="out_proj",
    )(x, pool_mix, pool_mix, attn, attn, mod3, norm_w, w_out_chunks)


def _ffn_kernel(h_ref, x_ref, mod_ref, wg_ref, wu_ref, wd_ref, fn_ref, o_ref, act_ref, *, n_f, tf, tn):
    s = pl.program_id(2)

    @pl.when(s < n_f)
    def _():
        h = h_ref[0]
        gate = jnp.dot(h, wg_ref[...], preferred_element_type=F32)
        up = jnp.dot(h, wu_ref[...], preferred_element_type=F32)
        act_ref[s] = (gate * jax.nn.sigmoid(gate) * up).astype(BF16)

    @pl.when(s >= n_f)
    def _():
        y = jnp.dot(act_ref[0], wd_ref[0:tf, :], preferred_element_type=F32)
        for f in range(1, n_f):
            y += jnp.dot(act_ref[f], wd_ref[f * tf:(f + 1) * tf, :], preferred_element_type=F32)
        x1 = x_ref[0]
        for j in range(D_MODEL // tn):
            @pl.when(s - n_f == j)
            def _():
                cols = slice(j * tn, (j + 1) * tn)
                o_ref[0, :, cols] = x1 + mod_ref[0, 5:6, cols] * y

    @pl.when(s == pl.num_programs(2) - 1)
    def _():
        o_ref[0] = _rms(o_ref[0]) * fn_ref[...]


def _ffn(h, x1, mod3, wg_bf, wu_bf, wd_bf, final_norm, tm=1024, tf=512, tn=256):
    b, t, _ = h.shape
    d_ff = wg_bf.shape[1]
    n_f = d_ff // tf
    n_n = D_MODEL // tn
    row = lambda i, j, s: (i, j, 0)
    up_blk = lambda i, j, s: (0, jnp.minimum(s, n_f - 1))
    down_blk = lambda i, j, s: (0, jnp.maximum(s - n_f, 0))
    return pl.pallas_call(
        functools.partial(_ffn_kernel, n_f=n_f, tf=tf, tn=tn),
        grid=(b, t // tm, n_f + n_n),
        in_specs=[
            pl.BlockSpec((1, tm, D_MODEL), row),
            pl.BlockSpec((1, tm, tn), lambda i, j, s: (i, j, jnp.maximum(s - n_f, 0))),
            pl.BlockSpec((1, N_MOD, D_MODEL), lambda i, j, s: (i, 0, 0)),
            pl.BlockSpec((D_MODEL, tf), up_blk),
            pl.BlockSpec((D_MODEL, tf), up_blk),
            pl.BlockSpec((d_ff, tn), down_blk),
            pl.BlockSpec((1, D_MODEL), lambda i, j, s: (0, 0)),
        ],
        out_specs=pl.BlockSpec((1, tm, D_MODEL), row),
        out_shape=jax.ShapeDtypeStruct(x1.shape, F32),
        scratch_shapes=[pltpu.VMEM((n_f, tm, tf), BF16)],
        compiler_params=_params("arbitrary", "arbitrary", "arbitrary"),
        name="ffn",
    )(h, x1, mod3, wg_bf, wu_bf, wd_bf, final_norm)


def _rope_tables(t):
    n_rows = t // GRID_W
    rows = jnp.repeat(jnp.arange(n_rows, dtype=F32), GRID_W)
    cols = jnp.tile(jnp.arange(GRID_W, dtype=F32), n_rows)
    freqs = ROPE_THETA ** (-jnp.arange(0, AXIS_ROT, 2, dtype=F32) / AXIS_ROT)
    ang = jnp.concatenate([rows[:, None] * freqs, cols[:, None] * freqs], axis=-1)
    cos_full = jnp.repeat(jnp.cos(ang), 2, axis=-1)
    sin = jnp.sin(ang)
    sin_signed = jnp.stack([-sin, sin], axis=-1).reshape(t, HEAD_DIM)
    return cos_full, sin_signed


def kernel(x, c, ctx, c_ctx, w_ada, b_ada, norm_mix, norm_ffn, w_in, pool_w, pool_scale,
           q_norm, k_norm, w_out, w_gate, w_up, w_down, final_norm):
    depth = w_ada.shape[0]
    assert depth == 1, "context tokens are only updated between layers; one layer is implemented"
    b, t, _ = x.shape
    cos_full, sin_signed = _rope_tables(t)

    cc = jnp.zeros((MOD_ROWS, D_MODEL), F32).at[:b].set(c).at[MOD_ROWS // 2].set(c_ctx)
    mod3 = _adaln_mod(cc, w_ada[0], b_ada).reshape(MOD_ROWS, N_MOD, D_MODEL)

    w_in_bf = w_in[0].astype(BF16)
    k_c, v_c = _ctx_kv(ctx, mod3, norm_mix, w_in_bf, k_norm)
    p, q, k_x, v_x = _in_proj(x, mod3, norm_mix, w_in_bf, q_norm, k_norm, cos_full, sin_signed)
    pooled = _pool_mix(p, pool_w[0].astype(BF16), pool_scale)
    attn = _attention(q, k_x, v_x, k_c, v_c)
    w_out_chunks = (w_out[0].astype(BF16)
                    .reshape(MIX_W, D_MODEL // MXU_WINDOW_W, MXU_WINDOW_W).transpose(1, 0, 2))
    x1, hf = _out_proj(x, pooled, attn, mod3, norm_ffn, w_out_chunks)
    return _ffn(hf, x1, mod3, w_gate[0].astype(BF16), w_up[0].astype(BF16),
                w_down[0].astype(BF16), final_norm.reshape(1, D_MODEL))
```

```python
import functools
import math

import jax
import jax.numpy as jnp
from jax import lax
from jax.experimental import pallas as pl
from jax.experimental.pallas import tpu as pltpu

D_MODEL = 2048
GRID_W = 64
HEAD_DIM = 128
N_HEADS = 8
N_KV_HEADS = 2
Q_PER_KV = N_HEADS // N_KV_HEADS
ATTN_W = N_HEADS * HEAD_DIM
KV_W = N_KV_HEADS * HEAD_DIM
POOL_WINDOWS = (2, 4, 8, 16)
N_POOL_GROUPS = len(POOL_WINDOWS)
POOL_W = D_MODEL // 2
POOL_GROUP_W = POOL_W // N_POOL_GROUPS
MIX_W = POOL_W + ATTN_W
PROJ_W = POOL_W + ATTN_W + 2 * KV_W
ROPE_THETA = 10000.0
AXIS_ROT = HEAD_DIM // 2
EPS = 1e-6
N_MOD = 6
MOD_ROWS = 16
POOL_PAD = 8
ROW_SUB = 256
MXU_WINDOW_W = 512
ATTN_SUB = 256
Q_SCALE = math.log2(math.e) / math.sqrt(HEAD_DIM)

F32 = jnp.float32
BF16 = jnp.bfloat16

VMEM_LIMIT_BYTES = 60 * 1024 * 1024


def _params(*sem):
    return pltpu.CompilerParams(dimension_semantics=sem, vmem_limit_bytes=VMEM_LIMIT_BYTES)


def _rms(x):
    return x * lax.rsqrt(jnp.mean(x * x, axis=-1, keepdims=True) + EPS)


def _adaln_kernel(c_ref, w_ref, b_ref, o_ref):
    c = c_ref[...]
    a = c * jax.nn.sigmoid(c)
    o_ref[...] = jnp.dot(a, w_ref[...], preferred_element_type=F32) + b_ref[...]


def _adaln_mod(cc, w_ada, b_ada, tn=1024):
    n = w_ada.shape[1]
    return pl.pallas_call(
        _adaln_kernel,
        grid=(n // tn,),
        in_specs=[
            pl.BlockSpec((MOD_ROWS, D_MODEL), lambda j: (0, 0)),
            pl.BlockSpec((D_MODEL, tn), lambda j: (0, j)),
            pl.BlockSpec((1, tn), lambda j: (0, j)),
        ],
        out_specs=pl.BlockSpec((MOD_ROWS, tn), lambda j: (0, j)),
        out_shape=jax.ShapeDtypeStruct((MOD_ROWS, n), F32),
        compiler_params=_params("arbitrary"),
        name="adaln_mod",
    )(cc, w_ada, b_ada)


def _norm_modulate(x, nw, mod_ref, shift_idx):
    sh = mod_ref[0, shift_idx:shift_idx + 1, :]
    sc = mod_ref[0, shift_idx + 1:shift_idx + 2, :]
    return (_rms(x) * nw) * (1.0 + sc) + sh


def _rope(x, cos, sin_signed, even_lane):
    partner = jnp.where(even_lane, pltpu.roll(x, HEAD_DIM - 1, axis=1), pltpu.roll(x, 1, axis=1))
    return x * cos + partner * sin_signed


def _ctx_kv_kernel(x_ref, mod_ref, nw_ref, w_ref, kn_ref, k_ref, v_ref):
    h = _norm_modulate(x_ref[0], nw_ref[...], mod_ref, 0)
    kv = jnp.dot(h.astype(BF16), w_ref[...], preferred_element_type=F32)
    kn = kn_ref[...]
    for j in range(N_KV_HEADS):
        kh = kv[:, j * HEAD_DIM:(j + 1) * HEAD_DIM]
        k_ref[0, :, j * HEAD_DIM:(j + 1) * HEAD_DIM] = (_rms(kh) * kn).astype(BF16)
    v_ref[0] = kv[:, KV_W:].astype(BF16)


def _ctx_kv(ctx, mod3, norm_w, w_in_bf, k_norm):
    b, s, _ = ctx.shape
    kv_block = (POOL_W + ATTN_W) // (2 * KV_W)
    return pl.pallas_call(
        _ctx_kv_kernel,
        grid=(b,),
        in_specs=[
            pl.BlockSpec((1, s, D_MODEL), lambda i: (i, 0, 0)),
            pl.BlockSpec((1, N_MOD, D_MODEL), lambda i: (MOD_ROWS // 2, 0, 0)),
            pl.BlockSpec((1, D_MODEL), lambda i: (0, 0)),
            pl.BlockSpec((D_MODEL, 2 * KV_W), lambda i: (0, kv_block)),
            pl.BlockSpec((1, HEAD_DIM), lambda i: (0, 0)),
        ],
        out_specs=[
            pl.BlockSpec((1, s, KV_W), lambda i: (i, 0, 0)),
            pl.BlockSpec((1, s, KV_W), lambda i: (i, 0, 0)),
        ],
        out_shape=[jax.ShapeDtypeStruct((b, s, KV_W), BF16)] * 2,
        compiler_params=_params("arbitrary"),
        name="ctx_kv",
    )(ctx, mod3, norm_w, w_in_bf, k_norm)


def _in_proj_kernel(x_ref, mod_ref, nw_ref, w_ref, qn_ref, kn_ref, cos_ref, sin_ref,
                    p_ref, q_ref, k_ref, v_ref):
    nw = nw_ref[...]
    qn = qn_ref[...] * Q_SCALE
    kn = kn_ref[...]
    even_lane = (lax.broadcasted_iota(jnp.int32, (ROW_SUB, HEAD_DIM), 1) % 2) == 0
    for r in range(x_ref.shape[1] // ROW_SUB):
        rows = slice(r * ROW_SUB, (r + 1) * ROW_SUB)
        h = _norm_modulate(x_ref[0, rows, :], nw, mod_ref, 0)
        proj = jnp.dot(h.astype(BF16), w_ref[...], preferred_element_type=F32)
        p_ref[0, rows, :] = proj[:, :POOL_W]
        cos = cos_ref[rows, :]
        sin = sin_ref[rows, :]
        for j in range(N_HEADS):
            lo = POOL_W + j * HEAD_DIM
            qh = _rms(proj[:, lo:lo + HEAD_DIM]) * qn
            q_ref[0, rows, j * HEAD_DIM:(j + 1) * HEAD_DIM] = _rope(qh, cos, sin, even_lane).astype(BF16)
        for j in range(N_KV_HEADS):
            lo = POOL_W + ATTN_W + j * HEAD_DIM
            kh = _rms(proj[:, lo:lo + HEAD_DIM]) * kn
            k_ref[0, rows, j * HEAD_DIM:(j + 1) * HEAD_DIM] = _rope(kh, cos, sin, even_lane).astype(BF16)
        v_ref[0, rows, :] = proj[:, POOL_W + ATTN_W + KV_W:].astype(BF16)


def _in_proj(x, mod3, norm_w, w_in_bf, q_norm, k_norm, cos_full, sin_signed, tm=1024):
    b, t, _ = x.shape
    row = lambda i, j: (i, j, 0)
    const2 = lambda i, j: (0, 0)
    return pl.pallas_call(
        _in_proj_kernel,
        grid=(b, t // tm),
        in_specs=[
            pl.BlockSpec((1, tm, D_MODEL), row),
            pl.BlockSpec((1, N_MOD, D_MODEL), lambda i, j: (i, 0, 0)),
            pl.BlockSpec((1, D_MODEL), const2),
            pl.BlockSpec((D_MODEL, PROJ_W), const2, pipeline_mode=pl.Buffered(1)),
            pl.BlockSpec((1, HEAD_DIM), const2),
            pl.BlockSpec((1, HEAD_DIM), const2),
            pl.BlockSpec((tm, HEAD_DIM), lambda i, j: (j, 0)),
            pl.BlockSpec((tm, HEAD_DIM), lambda i, j: (j, 0)),
        ],
        out_specs=[
            pl.BlockSpec((1, tm, POOL_W), row),
            pl.BlockSpec((1, tm, ATTN_W), row),
            pl.BlockSpec((1, tm, KV_W), row),
            pl.BlockSpec((1, tm, KV_W), row),
        ],
        out_shape=[
            jax.ShapeDtypeStruct((b, t, POOL_W), F32),
            jax.ShapeDtypeStruct((b, t, ATTN_W), BF16),
            jax.ShapeDtypeStruct((b, t, KV_W), BF16),
            jax.ShapeDtypeStruct((b, t, KV_W), BF16),
        ],
        compiler_params=_params("arbitrary", "arbitrary"),
        name="in_proj",
    )(x, mod3, norm_w, w_in_bf, q_norm, k_norm, cos_full, sin_signed)


def _pool_kernel(p_ref, w_ref, s_ref, o_ref):
    t = p_ref.shape[1]
    n = t + 2 * POOL_PAD
    pos = lax.broadcasted_iota(jnp.int32, (t, 1), 0)
    zeros = jnp.zeros((POOL_PAD, POOL_GROUP_W), F32)
    for g, win in enumerate(POOL_WINDOWS):
        half = win // 2
        u = p_ref[0, :, g * POOL_GROUP_W:(g + 1) * POOL_GROUP_W]
        f = jnp.concatenate([zeros, u, zeros], axis=0)
        step = 1
        while step < win:
            f = f + pltpu.roll(f, n - step, axis=0)
            step *= 2
        if half != POOL_PAD:
            f = pltpu.roll(f, n - (POOL_PAD - half), axis=0)
        count = jnp.minimum(pos + half, t) - jnp.maximum(pos - half, 0)
        pooled = f[:t] * (1.0 / count.astype(F32)) - u
        mixed = jnp.dot(pooled.astype(BF16), w_ref[g], preferred_element_type=F32)
        o_ref[0, :, g * POOL_GROUP_W:(g + 1) * POOL_GROUP_W] = (
            mixed * s_ref[:, g * POOL_GROUP_W:(g + 1) * POOL_GROUP_W]).astype(BF16)


def _pool_mix(p, pool_w_bf, pool_scale):
    b, t, _ = p.shape
    return pl.pallas_call(
        _pool_kernel,
        grid=(b,),
        in_specs=[
            pl.BlockSpec((1, t, POOL_W), lambda i: (i, 0, 0)),
            pl.BlockSpec((N_POOL_GROUPS, POOL_GROUP_W, POOL_GROUP_W), lambda i: (0, 0, 0)),
            pl.BlockSpec((1, POOL_W), lambda i: (0, 0)),
        ],
        out_specs=pl.BlockSpec((1, t, POOL_W), lambda i: (i, 0, 0)),
        out_shape=jax.ShapeDtypeStruct((b, t, POOL_W), BF16),
        compiler_params=_params("arbitrary"),
        name="pool_mix",
    )(p, pool_w_bf, pool_scale)


def _attn_kernel(q_ref, kx_ref, kc_ref, vx_ref, vc_ref, *refs):
    n_cast = (len(refs) - 1) // 2
    o_ref = refs[n_cast]
    for src_ref, dst_ref in zip(refs[:n_cast], refs[n_cast + 1:]):
        dst_ref[...] = src_ref[...].astype(BF16)
    kx = kx_ref[0]
    kc = kc_ref[0]
    vx = jnp.concatenate([vx_ref[0], jnp.ones(vx_ref.shape[1:], BF16)], axis=1)
    vc = jnp.concatenate([vc_ref[0], jnp.ones(vc_ref.shape[1:], BF16)], axis=1)
    nt = (((1,), (1,)), ((), ()))
    for r in range(q_ref.shape[1] // ATTN_SUB):
        rows = slice(r * ATTN_SUB, (r + 1) * ATTN_SUB)
        for g in range(Q_PER_KV):
            cols = slice(g * HEAD_DIM, (g + 1) * HEAD_DIM)
            q = q_ref[0, rows, cols]
            sx = lax.dot_general(q, kx, nt, preferred_element_type=F32)
            sc = lax.dot_general(q, kc, nt, preferred_element_type=F32)
            m = jnp.maximum(jnp.max(sx, axis=-1, keepdims=True), jnp.max(sc, axis=-1, keepdims=True))
            px = jnp.exp2(sx - m)
            pc = jnp.exp2(sc - m)
            o = (jnp.dot(px.astype(BF16), vx, preferred_element_type=F32)
                 + jnp.dot(pc.astype(BF16), vc, preferred_element_type=F32))
            o_ref[0, rows, cols] = (o[:, :HEAD_DIM] * (1.0 / o[:, HEAD_DIM:])).astype(BF16)


def _attention(q, k_x, v_x, k_c, v_c, cast_f32, tq=1024):
    b, t, _ = q.shape
    s = k_c.shape[1]
    qw = Q_PER_KV * HEAD_DIM
    n_t = t // tq
    n_steps = b * N_KV_HEADS * n_t
    kv_map = lambda i, h, j: (i, 0, h)
    slab = lambda a: pl.BlockSpec((a.shape[0] // n_steps, a.shape[1]),
                                  lambda i, h, j: ((i * N_KV_HEADS + h) * n_t + j, 0))
    assert all(a.shape[0] % (16 * n_steps) == 0 for a in cast_f32)
    out = pl.pallas_call(
        _attn_kernel,
        grid=(b, N_KV_HEADS, n_t),
        in_specs=[
            pl.BlockSpec((1, tq, qw), lambda i, h, j: (i, j, h)),
            pl.BlockSpec((1, t, HEAD_DIM), kv_map),
            pl.BlockSpec((1, s, HEAD_DIM), kv_map),
            pl.BlockSpec((1, t, HEAD_DIM), kv_map),
            pl.BlockSpec((1, s, HEAD_DIM), kv_map),
            *[slab(a) for a in cast_f32],
        ],
        out_specs=[pl.BlockSpec((1, tq, qw), lambda i, h, j: (i, j, h)), *[slab(a) for a in cast_f32]],
        out_shape=[jax.ShapeDtypeStruct((b, t, ATTN_W), BF16),
                   *[jax.ShapeDtypeStruct(a.shape, BF16) for a in cast_f32]],
        compiler_params=_params("arbitrary", "arbitrary", "arbitrary"),
        name="attention",
    )(q, k_x, k_c, v_x, v_c, *cast_f32)
    return out[0], out[1:]


def _out_proj_kernel(x_ref, *refs):
    lhs_refs, (mod_ref, nw_ref), w_refs, (o_ref, h_ref) = refs[:4], refs[4:6], refs[6:-2], refs[-2:]
    gate = mod_ref[0, 2:3, :]
    nw = nw_ref[...]
    for r in range(x_ref.shape[1] // ROW_SUB):
        rows = slice(r * ROW_SUB, (r + 1) * ROW_SUB)
        lhs = jnp.concatenate([ref[0, rows, :] for ref in lhs_refs], axis=1)
        mix = jnp.concatenate([jnp.dot(lhs, w_ref[...], preferred_element_type=F32)
                               for w_ref in w_refs], axis=1)
        x1 = x_ref[0, rows, :] + gate * mix
        o_ref[0, rows, :] = x1
        h_ref[0, rows, :] = _norm_modulate(x1, nw, mod_ref, 3).astype(BF16)


def _out_proj(x, pool_mix, attn, mod3, norm_w, w_out_bf, tm=1024):
    b, t, _ = x.shape
    row = lambda i, j: (i, j, 0)
    lhs_window = lambda c: pl.BlockSpec((1, tm, MXU_WINDOW_W), lambda i, j: (i, j, c))
    w_window = lambda c: pl.BlockSpec((MIX_W, MXU_WINDOW_W), lambda i, j: (0, c),
                                      pipeline_mode=pl.Buffered(1))
    n_w = D_MODEL // MXU_WINDOW_W
    return pl.pallas_call(
        _out_proj_kernel,
        grid=(b, t // tm),
        in_specs=[
            pl.BlockSpec((1, tm, D_MODEL), row),
            lhs_window(0), lhs_window(1), lhs_window(0), lhs_window(1),
            pl.BlockSpec((1, N_MOD, D_MODEL), lambda i, j: (i, 0, 0)),
            pl.BlockSpec((1, D_MODEL), lambda i, j: (0, 0)),
            *[w_window(c) for c in range(n_w)],
        ],
        out_specs=[pl.BlockSpec((1, tm, D_MODEL), row), pl.BlockSpec((1, tm, D_MODEL), row)],
        out_shape=[jax.ShapeDtypeStruct(x.shape, F32), jax.ShapeDtypeStruct(x.shape, BF16)],
        compiler_params=_params("arbitrary", "arbitrary"),
        name="out_proj",
    )(x, pool_mix, pool_mix, attn, attn, mod3, norm_w, *([w_out_bf] * n_w))


def _ffn_kernel(h_ref, x_ref, gate_ref, wg_ref, wu_ref, wd_ref, fn_ref, o_ref, act_ref, *, n_f, tf, tn):
    s = pl.program_id(2)

    @pl.when(s < n_f)
    def _():
        h = h_ref[0]
        gate = jnp.dot(h, wg_ref[...], preferred_element_type=F32)
        up = jnp.dot(h, wu_ref[...], preferred_element_type=F32)
        act_ref[s] = (gate * jax.nn.sigmoid(gate) * up).astype(BF16)

    @pl.when(s >= n_f)
    def _():
        y = jnp.dot(act_ref[0], wd_ref[0:tf, :], preferred_element_type=F32)
        for f in range(1, n_f):
            y += jnp.dot(act_ref[f], wd_ref[f * tf:(f + 1) * tf, :], preferred_element_type=F32)
        cols = pl.ds(pl.multiple_of((s - n_f) * tn, tn), tn)
        o_ref[0, :, cols] = x_ref[0] + gate_ref[0, 5:6, :] * y

    @pl.when(s == pl.num_programs(2) - 1)
    def _():
        o_ref[0] = _rms(o_ref[0]) * fn_ref[...]


def _ffn(h, x1, mod3, wg_bf, wu_bf, wd_bf, final_norm, tm=1024, tf=512, tn=256):
    b, t, _ = h.shape
    d_ff = wg_bf.shape[1]
    n_f = d_ff // tf
    n_n = D_MODEL // tn
    row = lambda i, j, s: (i, j, 0)
    up_blk = lambda i, j, s: (0, jnp.minimum(s, n_f - 1))
    down_blk = lambda i, j, s: (0, jnp.maximum(s - n_f, 0))
    return pl.pallas_call(
        functools.partial(_ffn_kernel, n_f=n_f, tf=tf, tn=tn),
        grid=(b, t // tm, n_f + n_n),
        in_specs=[
            pl.BlockSpec((1, tm, D_MODEL), row),
            pl.BlockSpec((1, tm, tn), lambda i, j, s: (i, j, jnp.maximum(s - n_f, 0))),
            pl.BlockSpec((1, N_MOD, tn), lambda i, j, s: (i, 0, jnp.maximum(s - n_f, 0))),
            pl.BlockSpec((D_MODEL, tf), up_blk),
            pl.BlockSpec((D_MODEL, tf), up_blk),
            pl.BlockSpec((d_ff, tn), down_blk),
            pl.BlockSpec((1, D_MODEL), lambda i, j, s: (0, 0)),
        ],
        out_specs=pl.BlockSpec((1, tm, D_MODEL), row),
        out_shape=jax.ShapeDtypeStruct(x1.shape, F32),
        scratch_shapes=[pltpu.VMEM((n_f, tm, tf), BF16)],
        compiler_params=_params("arbitrary", "arbitrary", "arbitrary"),
        name="ffn",
    )(h, x1, mod3, wg_bf, wu_bf, wd_bf, final_norm)


def _rope_tables(t):
    n_rows = t // GRID_W
    rows = jnp.repeat(jnp.arange(n_rows, dtype=F32), GRID_W)
    cols = jnp.tile(jnp.arange(GRID_W, dtype=F32), n_rows)
    freqs = ROPE_THETA ** (-jnp.arange(0, AXIS_ROT, 2, dtype=F32) / AXIS_ROT)
    ang = jnp.concatenate([rows[:, None] * freqs, cols[:, None] * freqs], axis=-1)
    cos_full = jnp.repeat(jnp.cos(ang), 2, axis=-1)
    sin = jnp.sin(ang)
    sin_signed = jnp.stack([-sin, sin], axis=-1).reshape(t, HEAD_DIM)
    return cos_full, sin_signed


def kernel(x, c, ctx, c_ctx, w_ada, b_ada, norm_mix, norm_ffn, w_in, pool_w, pool_scale,
           q_norm, k_norm, w_out, w_gate, w_up, w_down, final_norm):
    depth = w_ada.shape[0]
    assert depth == 1, "context tokens are only updated between layers; one layer is implemented"
    b, t, _ = x.shape
    cos_full, sin_signed = _rope_tables(t)

    cc = jnp.zeros((MOD_ROWS, D_MODEL), F32).at[:b].set(c).at[MOD_ROWS // 2].set(c_ctx)
    mod3 = _adaln_mod(cc, w_ada[0], b_ada).reshape(MOD_ROWS, N_MOD, D_MODEL)

    w_in_bf = w_in[0].astype(BF16)
    k_c, v_c = _ctx_kv(ctx, mod3, norm_mix, w_in_bf, k_norm)
    p, q, k_x, v_x = _in_proj(x, mod3, norm_mix, w_in_bf, q_norm, k_norm, cos_full, sin_signed)
    pooled = _pool_mix(p, pool_w[0].astype(BF16), pool_scale)
    attn, (wg_bf, wu_bf, wd_bf) = _attention(q, k_x, v_x, k_c, v_c, (w_gate[0], w_up[0], w_down[0]))
    x1, hf = _out_proj(x, pooled, attn, mod3, norm_ffn, w_out[0].astype(BF16))
    return _ffn(hf, x1, mod3, wg_bf, wu_bf, wd_bf, final_norm.reshape(1, D_MODEL))
```

```python
import functools
import math

import jax
import jax.numpy as jnp
from jax import lax
from jax.experimental import pallas as pl
from jax.experimental.pallas import tpu as pltpu

D_MODEL = 2048
GRID_W = 64
HEAD_DIM = 128
N_HEADS = 8
N_KV_HEADS = 2
Q_PER_KV = N_HEADS // N_KV_HEADS
ATTN_W = N_HEADS * HEAD_DIM
KV_W = N_KV_HEADS * HEAD_DIM
POOL_WINDOWS = (2, 4, 8, 16)
N_POOL_GROUPS = len(POOL_WINDOWS)
POOL_W = D_MODEL // 2
POOL_GROUP_W = POOL_W // N_POOL_GROUPS
MIX_W = POOL_W + ATTN_W
PROJ_W = POOL_W + ATTN_W + 2 * KV_W
ROPE_THETA = 10000.0
AXIS_ROT = HEAD_DIM // 2
EPS = 1e-6
N_MOD = 6
MOD_ROWS = 16
POOL_PAD = 8
ROW_SUB = 256
MXU_WINDOW_W = 512
ATTN_SUB = 256
Q_SCALE = math.log2(math.e) / math.sqrt(HEAD_DIM)

F32 = jnp.float32
BF16 = jnp.bfloat16

VMEM_LIMIT_BYTES = 60 * 1024 * 1024


def _params(*sem):
    return pltpu.CompilerParams(dimension_semantics=sem, vmem_limit_bytes=VMEM_LIMIT_BYTES)


def _rms(x):
    return x * lax.rsqrt(jnp.mean(x * x, axis=-1, keepdims=True) + EPS)


def _adaln_kernel(c_ref, w_ref, b_ref, o_ref):
    c = c_ref[...]
    a = c * jax.nn.sigmoid(c)
    o_ref[...] = jnp.dot(a, w_ref[...], preferred_element_type=F32) + b_ref[...]


def _adaln_mod(cc, w_ada, b_ada, tn=1024):
    n = w_ada.shape[1]
    return pl.pallas_call(
        _adaln_kernel,
        grid=(n // tn,),
        in_specs=[
            pl.BlockSpec((MOD_ROWS, D_MODEL), lambda j: (0, 0)),
            pl.BlockSpec((D_MODEL, tn), lambda j: (0, j)),
            pl.BlockSpec((1, tn), lambda j: (0, j)),
        ],
        out_specs=pl.BlockSpec((MOD_ROWS, tn), lambda j: (0, j)),
        out_shape=jax.ShapeDtypeStruct((MOD_ROWS, n), F32),
        compiler_params=_params("arbitrary"),
        name="adaln_mod",
    )(cc, w_ada, b_ada)


def _norm_modulate(x, nw, mod_ref, shift_idx):
    sh = mod_ref[0, shift_idx:shift_idx + 1, :]
    sc = mod_ref[0, shift_idx + 1:shift_idx + 2, :]
    return (_rms(x) * nw) * (1.0 + sc) + sh


def _rope(x, cos, sin_signed, even_lane):
    partner = jnp.where(even_lane, pltpu.roll(x, HEAD_DIM - 1, axis=1), pltpu.roll(x, 1, axis=1))
    return x * cos + partner * sin_signed


def _ctx_kv_kernel(x_ref, mod_ref, nw_ref, w_ref, kn_ref, k_ref, v_ref):
    nw = nw_ref[...]
    kn = kn_ref[...]
    for r in range(x_ref.shape[0] // ROW_SUB):
        rows = slice(r * ROW_SUB, (r + 1) * ROW_SUB)
        h = _norm_modulate(x_ref[rows, :], nw, mod_ref, 0)
        kv = jnp.dot(h.astype(BF16), w_ref[...], preferred_element_type=F32)
        for j in range(N_KV_HEADS):
            cols = slice(j * HEAD_DIM, (j + 1) * HEAD_DIM)
            k_ref[rows, cols] = (_rms(kv[:, cols]) * kn).astype(BF16)
        v_ref[rows, :] = kv[:, KV_W:].astype(BF16)


def _ctx_kv(ctx, mod3, norm_w, w_in_bf, k_norm, tm=1024):
    b, s, _ = ctx.shape
    kv_block = (POOL_W + ATTN_W) // (2 * KV_W)
    k, v = pl.pallas_call(
        _ctx_kv_kernel,
        grid=(b * s // tm,),
        in_specs=[
            pl.BlockSpec((tm, D_MODEL), lambda i: (i, 0)),
            pl.BlockSpec((1, N_MOD, D_MODEL), lambda i: (MOD_ROWS // 2, 0, 0)),
            pl.BlockSpec((1, D_MODEL), lambda i: (0, 0)),
            pl.BlockSpec((D_MODEL, 2 * KV_W), lambda i: (0, kv_block)),
            pl.BlockSpec((1, HEAD_DIM), lambda i: (0, 0)),
        ],
        out_specs=[pl.BlockSpec((tm, KV_W), lambda i: (i, 0))] * 2,
        out_shape=[jax.ShapeDtypeStruct((b * s, KV_W), BF16)] * 2,
        compiler_params=_params("arbitrary"),
        name="ctx_kv",
    )(ctx.reshape(b * s, D_MODEL), mod3, norm_w, w_in_bf, k_norm)
    return k.reshape(b, s, KV_W), v.reshape(b, s, KV_W)


def _in_proj_kernel(x_ref, mod_ref, nw_ref, w_ref, qn_ref, kn_ref, cos_ref, sin_ref,
                    p_ref, q_ref, k_ref, v_ref):
    nw = nw_ref[...]
    qn = qn_ref[...] * Q_SCALE
    kn = kn_ref[...]
    even_lane = (lax.broadcasted_iota(jnp.int32, (ROW_SUB, HEAD_DIM), 1) % 2) == 0
    for r in range(x_ref.shape[1] // ROW_SUB):
        rows = slice(r * ROW_SUB, (r + 1) * ROW_SUB)
        h = _norm_modulate(x_ref[0, rows, :], nw, mod_ref, 0)
        proj = jnp.dot(h.astype(BF16), w_ref[...], preferred_element_type=F32)
        p_ref[0, rows, :] = proj[:, :POOL_W]
        cos = cos_ref[rows, :]
        sin = sin_ref[rows, :]
        for j in range(N_HEADS):
            lo = POOL_W + j * HEAD_DIM
            qh = _rms(proj[:, lo:lo + HEAD_DIM]) * qn
            q_ref[0, rows, j * HEAD_DIM:(j + 1) * HEAD_DIM] = _rope(qh, cos, sin, even_lane).astype(BF16)
        for j in range(N_KV_HEADS):
            lo = POOL_W + ATTN_W + j * HEAD_DIM
            kh = _rms(proj[:, lo:lo + HEAD_DIM]) * kn
            k_ref[0, rows, j * HEAD_DIM:(j + 1) * HEAD_DIM] = _rope(kh, cos, sin, even_lane).astype(BF16)
        v_ref[0, rows, :] = proj[:, POOL_W + ATTN_W + KV_W:].astype(BF16)


def _in_proj(x, mod3, norm_w, w_in_bf, q_norm, k_norm, cos_full, sin_signed, tm=1024):
    b, t, _ = x.shape
    row = lambda i, j: (i, j, 0)
    const2 = lambda i, j: (0, 0)
    return pl.pallas_call(
        _in_proj_kernel,
        grid=(b, t // tm),
        in_specs=[
            pl.BlockSpec((1, tm, D_MODEL), row),
            pl.BlockSpec((1, N_MOD, D_MODEL), lambda i, j: (i, 0, 0)),
            pl.BlockSpec((1, D_MODEL), const2),
            pl.BlockSpec((D_MODEL, PROJ_W), const2, pipeline_mode=pl.Buffered(1)),
            pl.BlockSpec((1, HEAD_DIM), const2),
            pl.BlockSpec((1, HEAD_DIM), const2),
            pl.BlockSpec((tm, HEAD_DIM), lambda i, j: (j, 0)),
            pl.BlockSpec((tm, HEAD_DIM), lambda i, j: (j, 0)),
        ],
        out_specs=[
            pl.BlockSpec((1, tm, POOL_W), row),
            pl.BlockSpec((1, tm, ATTN_W), row),
            pl.BlockSpec((1, tm, KV_W), row),
            pl.BlockSpec((1, tm, KV_W), row),
        ],
        out_shape=[
            jax.ShapeDtypeStruct((b, t, POOL_W), F32),
            jax.ShapeDtypeStruct((b, t, ATTN_W), BF16),
            jax.ShapeDtypeStruct((b, t, KV_W), BF16),
            jax.ShapeDtypeStruct((b, t, KV_W), BF16),
        ],
        compiler_params=_params("arbitrary", "arbitrary"),
        name="in_proj",
    )(x, mod3, norm_w, w_in_bf, q_norm, k_norm, cos_full, sin_signed)


def _pool_inv_counts(t):
    pos = jnp.arange(t, dtype=jnp.int32)[None, :]
    half = jnp.array(POOL_WINDOWS, dtype=jnp.int32)[:, None] // 2
    count = jnp.minimum(pos + half, t) - jnp.maximum(pos - half, 0)
    return jnp.broadcast_to((1.0 / count.astype(F32))[:, :, None], (N_POOL_GROUPS, t, HEAD_DIM))


def _pool_mix_rows(p_ref, prev_ref, next_ref, inv_ref, w_ref, s_ref, o_ref, has_prev, has_next):
    t = p_ref.shape[1]
    n = t + 2 * POOL_PAD
    for g, win in enumerate(POOL_WINDOWS):
        half = win // 2
        cols = slice(g * POOL_GROUP_W, (g + 1) * POOL_GROUP_W)
        u = p_ref[0, :, cols]
        top = jnp.where(has_prev, prev_ref[0, :, cols], 0.0)
        bottom = jnp.where(has_next, next_ref[0, :, cols], 0.0)
        f = jnp.concatenate([top, u, bottom], axis=0)
        step = 1
        while step < win:
            f = f + pltpu.roll(f, n - step, axis=0)
            step *= 2
        if half != POOL_PAD:
            f = pltpu.roll(f, n - (POOL_PAD - half), axis=0)
        inv = inv_ref[g]
        pooled = f[:t] * jnp.concatenate([inv] * (POOL_GROUP_W // inv.shape[1]), axis=1) - u
        mixed = jnp.dot(pooled.astype(BF16), w_ref[g].astype(BF16), preferred_element_type=F32)
        o_ref[0, :, cols] = (mixed * s_ref[:, cols]).astype(BF16)


def _attn_kernel(q_ref, kx_ref, kc_ref, vx_ref, vc_ref,
                 p_ref, prev_ref, next_ref, inv_ref, pw_ref, ps_ref, *refs, n_parts):
    n_cast = (len(refs) - 2) // 2
    o_ref, pool_ref = refs[n_cast], refs[n_cast + 1]
    part = pl.program_id(1) * pl.num_programs(2) + pl.program_id(2)
    _pool_mix_rows(p_ref, prev_ref, next_ref, inv_ref, pw_ref, ps_ref, pool_ref,
                   part > 0, part < n_parts - 1)
    for src_ref, dst_ref in zip(refs[:n_cast], refs[n_cast + 2:]):
        dst_ref[...] = src_ref[...].astype(BF16)
    kx = kx_ref[0]
    kc = kc_ref[0]
    vx = jnp.concatenate([vx_ref[0], jnp.ones(vx_ref.shape[1:], BF16)], axis=1)
    vc = jnp.concatenate([vc_ref[0], jnp.ones(vc_ref.shape[1:], BF16)], axis=1)
    nt = (((1,), (1,)), ((), ()))
    for r in range(q_ref.shape[1] // ATTN_SUB):
        rows = slice(r * ATTN_SUB, (r + 1) * ATTN_SUB)
        for g in range(Q_PER_KV):
            cols = slice(g * HEAD_DIM, (g + 1) * HEAD_DIM)
            q = q_ref[0, rows, cols]
            sx = lax.dot_general(q, kx, nt, preferred_element_type=F32)
            sc = lax.dot_general(q, kc, nt, preferred_element_type=F32)
            m = jnp.maximum(jnp.max(sx, axis=-1, keepdims=True), jnp.max(sc, axis=-1, keepdims=True))
            px = jnp.exp2(sx - m)
            pc = jnp.exp2(sc - m)
            o = (jnp.dot(px.astype(BF16), vx, preferred_element_type=F32)
                 + jnp.dot(pc.astype(BF16), vc, preferred_element_type=F32))
            o_ref[0, rows, cols] = (o[:, :HEAD_DIM] * (1.0 / o[:, HEAD_DIM:])).astype(BF16)


def _attention(q, k_x, v_x, k_c, v_c, p, pool_w, pool_scale, cast_f32, tq=1024):
    b, t, _ = q.shape
    s = k_c.shape[1]
    qw = Q_PER_KV * HEAD_DIM
    n_t = t // tq
    n_parts = N_KV_HEADS * n_t
    n_steps = b * n_parts
    pool_rows = t // n_parts
    halo_blocks = pool_rows // POOL_PAD
    assert pool_rows % POOL_PAD == 0 and all(a.shape[0] % (16 * n_steps) == 0 for a in cast_f32)
    part = lambda h, j: h * n_t + j
    kv_map = lambda i, h, j: (i, 0, h)
    pool_map = lambda i, h, j: (i, part(h, j), 0)
    slab = lambda a: pl.BlockSpec((a.shape[0] // n_steps, a.shape[1]),
                                  lambda i, h, j: (i * n_parts + part(h, j), 0))
    out = pl.pallas_call(
        functools.partial(_attn_kernel, n_parts=n_parts),
        grid=(b, N_KV_HEADS, n_t),
        in_specs=[
            pl.BlockSpec((1, tq, qw), lambda i, h, j: (i, j, h)),
            pl.BlockSpec((1, t, HEAD_DIM), kv_map),
            pl.BlockSpec((1, s, HEAD_DIM), kv_map),
            pl.BlockSpec((1, t, HEAD_DIM), kv_map),
            pl.BlockSpec((1, s, HEAD_DIM), kv_map),
            pl.BlockSpec((1, pool_rows, POOL_W), pool_map),
            pl.BlockSpec((1, POOL_PAD, POOL_W),
                         lambda i, h, j: (i, jnp.maximum(part(h, j) * halo_blocks - 1, 0), 0)),
            pl.BlockSpec((1, POOL_PAD, POOL_W),
                         lambda i, h, j: (i, jnp.minimum((part(h, j) + 1) * halo_blocks,
                                                         n_parts * halo_blocks - 1), 0)),
            pl.BlockSpec((N_POOL_GROUPS, pool_rows, HEAD_DIM), lambda i, h, j: (0, part(h, j), 0)),
            pl.BlockSpec((N_POOL_GROUPS, POOL_GROUP_W, POOL_GROUP_W), lambda i, h, j: (0, 0, 0)),
            pl.BlockSpec((1, POOL_W), lambda i, h, j: (0, 0)),
            *[slab(a) for a in cast_f32],
        ],
        out_specs=[pl.BlockSpec((1, tq, qw), lambda i, h, j: (i, j, h)),
                   pl.BlockSpec((1, pool_rows, POOL_W), pool_map),
                   *[slab(a) for a in cast_f32]],
        out_shape=[jax.ShapeDtypeStruct((b, t, ATTN_W), BF16),
                   jax.ShapeDtypeStruct((b, t, POOL_W), BF16),
                   *[jax.ShapeDtypeStruct(a.shape, BF16) for a in cast_f32]],
        compiler_params=_params("arbitrary", "arbitrary", "arbitrary"),
        name="attention",
    )(q, k_x, k_c, v_x, v_c, p, p, p, _pool_inv_counts(t), pool_w, pool_scale, *cast_f32)
    return out[0], out[1], out[2:]


def _out_proj_kernel(x_ref, *refs):
    lhs_refs, (mod_ref, nw_ref), w_refs, (o_ref, h_ref) = refs[:4], refs[4:6], refs[6:-2], refs[-2:]
    gate = mod_ref[0, 2:3, :]
    nw = nw_ref[...]
    for r in range(x_ref.shape[1] // ROW_SUB):
        rows = slice(r * ROW_SUB, (r + 1) * ROW_SUB)
        lhs = jnp.concatenate([ref[0, rows, :] for ref in lhs_refs], axis=1)
        mix = jnp.concatenate([jnp.dot(lhs, w_ref[...], preferred_element_type=F32)
                               for w_ref in w_refs], axis=1)
        x1 = x_ref[0, rows, :] + gate * mix
        o_ref[0, rows, :] = x1
        h_ref[0, rows, :] = _norm_modulate(x1, nw, mod_ref, 3).astype(BF16)


def _out_proj(x, pool_mix, attn, mod3, norm_w, w_out_bf, tm=1024):
    b, t, _ = x.shape
    row = lambda i, j: (i, j, 0)
    lhs_window = lambda c: pl.BlockSpec((1, tm, MXU_WINDOW_W), lambda i, j: (i, j, c))
    w_window = lambda c: pl.BlockSpec((MIX_W, MXU_WINDOW_W), lambda i, j: (0, c),
                                      pipeline_mode=pl.Buffered(1))
    n_w = D_MODEL // MXU_WINDOW_W
    return pl.pallas_call(
        _out_proj_kernel,
        grid=(b, t // tm),
        in_specs=[
            pl.BlockSpec((1, tm, D_MODEL), row),
            lhs_window(0), lhs_window(1), lhs_window(0), lhs_window(1),
            pl.BlockSpec((1, N_MOD, D_MODEL), lambda i, j: (i, 0, 0)),
            pl.BlockSpec((1, D_MODEL), lambda i, j: (0, 0)),
            *[w_window(c) for c in range(n_w)],
        ],
        out_specs=[pl.BlockSpec((1, tm, D_MODEL), row), pl.BlockSpec((1, tm, D_MODEL), row)],
        out_shape=[jax.ShapeDtypeStruct(x.shape, F32), jax.ShapeDtypeStruct(x.shape, BF16)],
        compiler_params=_params("arbitrary", "arbitrary"),
        name="out_proj",
    )(x, pool_mix, pool_mix, attn, attn, mod3, norm_w, *([w_out_bf] * n_w))


def _ffn_kernel(h_ref, x_ref, gate_ref, wg_ref, wu_ref, wd_ref, fn_ref, o_ref, act_ref, *, n_f, tf, tn):
    s = pl.program_id(2)

    @pl.when(s < n_f)
    def _():
        h = h_ref[0]
        gate = jnp.dot(h, wg_ref[...], preferred_element_type=F32)
        up = jnp.dot(h, wu_ref[...], preferred_element_type=F32)
        act_ref[s] = (gate * jax.nn.sigmoid(gate) * up).astype(BF16)

    @pl.when(s >= n_f)
    def _():
        y = jnp.dot(act_ref[0], wd_ref[0:tf, :], preferred_element_type=F32)
        for f in range(1, n_f):
            y += jnp.dot(act_ref[f], wd_ref[f * tf:(f + 1) * tf, :], preferred_element_type=F32)
        cols = pl.ds(pl.multiple_of((s - n_f) * tn, tn), tn)
        o_ref[0, :, cols] = x_ref[0] + gate_ref[0, 5:6, :] * y

    @pl.when(s == pl.num_programs(2) - 1)
    def _():
        o_ref[0] = _rms(o_ref[0]) * fn_ref[...]


def _ffn(h, x1, mod3, wg_bf, wu_bf, wd_bf, final_norm, tm=1024, tf=512, tn=256):
    b, t, _ = h.shape
    d_ff = wg_bf.shape[1]
    n_f = d_ff // tf
    n_n = D_MODEL // tn
    row = lambda i, j, s: (i, j, 0)
    up_blk = lambda i, j, s: (0, jnp.minimum(s, n_f - 1))
    down_blk = lambda i, j, s: (0, jnp.maximum(s - n_f, 0))
    return pl.pallas_call(
        functools.partial(_ffn_kernel, n_f=n_f, tf=tf, tn=tn),
        grid=(b, t // tm, n_f + n_n),
        in_specs=[
            pl.BlockSpec((1, tm, D_MODEL), row),
            pl.BlockSpec((1, tm, tn), lambda i, j, s: (i, j, jnp.maximum(s - n_f, 0))),
            pl.BlockSpec((1, N_MOD, tn), lambda i, j, s: (i, 0, jnp.maximum(s - n_f, 0))),
            pl.BlockSpec((D_MODEL, tf), up_blk),
            pl.BlockSpec((D_MODEL, tf), up_blk),
            pl.BlockSpec((d_ff, tn), down_blk),
            pl.BlockSpec((1, D_MODEL), lambda i, j, s: (0, 0)),
        ],
        out_specs=pl.BlockSpec((1, tm, D_MODEL), row),
        out_shape=jax.ShapeDtypeStruct(x1.shape, F32),
        scratch_shapes=[pltpu.VMEM((n_f, tm, tf), BF16)],
        compiler_params=_params("arbitrary", "arbitrary", "arbitrary"),
        name="ffn",
    )(h, x1, mod3, wg_bf, wu_bf, wd_bf, final_norm)


def _rope_tables(t):
    n_rows = t // GRID_W
    rows = jnp.repeat(jnp.arange(n_rows, dtype=F32), GRID_W)
    cols = jnp.tile(jnp.arange(GRID_W, dtype=F32), n_rows)
    freqs = ROPE_THETA ** (-jnp.arange(0, AXIS_ROT, 2, dtype=F32) / AXIS_ROT)
    ang = jnp.concatenate([rows[:, None] * freqs, cols[:, None] * freqs], axis=-1)
    cos_full = jnp.repeat(jnp.cos(ang), 2, axis=-1)
    sin = jnp.sin(ang)
    sin_signed = jnp.stack([-sin, sin], axis=-1).reshape(t, HEAD_DIM)
    return cos_full, sin_signed


def kernel(x, c, ctx, c_ctx, w_ada, b_ada, norm_mix, norm_ffn, w_in, pool_w, pool_scale,
           q_norm, k_norm, w_out, w_gate, w_up, w_down, final_norm):
    depth = w_ada.shape[0]
    assert depth == 1, "context tokens are only updated between layers; one layer is implemented"
    b, t, _ = x.shape
    cos_full, sin_signed = _rope_tables(t)

    cc = jnp.zeros((MOD_ROWS, D_MODEL), F32).at[:b].set(c).at[MOD_ROWS // 2].set(c_ctx)
    mod3 = _adaln_mod(cc, w_ada[0], b_ada).reshape(MOD_ROWS, N_MOD, D_MODEL)

    w_in_bf = w_in[0].astype(BF16)
    k_c, v_c = _ctx_kv(ctx, mod3, norm_mix, w_in_bf, k_norm)
    p, q, k_x, v_x = _in_proj(x, mod3, norm_mix, w_in_bf, q_norm, k_norm, cos_full, sin_signed)
    attn, pooled, (wo_bf, wg_bf, wu_bf, wd_bf) = _attention(
        q, k_x, v_x, k_c, v_c, p, pool_w[0], pool_scale, (w_out[0], w_gate[0], w_up[0], w_down[0]))
    x1, hf = _out_proj(x, pooled, attn, mod3, norm_ffn, wo_bf)
    return _ffn(hf, x1, mod3, wg_bf, wu_bf, wd_bf, final_norm.reshape(1, D_MODEL))
```

```python
import functools
import math

import jax
import jax.numpy as jnp
from jax import lax
from jax.experimental import pallas as pl
from jax.experimental.pallas import tpu as pltpu

D_MODEL = 2048
GRID_W = 64
HEAD_DIM = 128
N_HEADS = 8
N_KV_HEADS = 2
Q_PER_KV = N_HEADS // N_KV_HEADS
ATTN_W = N_HEADS * HEAD_DIM
KV_W = N_KV_HEADS * HEAD_DIM
POOL_WINDOWS = (2, 4, 8, 16)
N_POOL_GROUPS = len(POOL_WINDOWS)
POOL_W = D_MODEL // 2
POOL_GROUP_W = POOL_W // N_POOL_GROUPS
MIX_W = POOL_W + ATTN_W
PROJ_W = POOL_W + ATTN_W + 2 * KV_W
ROPE_THETA = 10000.0
AXIS_ROT = HEAD_DIM // 2
EPS = 1e-6
N_MOD = 6
MOD_ROWS = 16
POOL_PAD = 8
ROW_SUB = 256
MXU_WINDOW_W = 512
ATTN_SUB = 256
FFN_CHUNK_W = 256
Q_SCALE = math.log2(math.e) / math.sqrt(HEAD_DIM)

F32 = jnp.float32
BF16 = jnp.bfloat16

VMEM_LIMIT_BYTES = 60 * 1024 * 1024


def _params(*sem):
    return pltpu.CompilerParams(dimension_semantics=sem, vmem_limit_bytes=VMEM_LIMIT_BYTES)


def _rms(x):
    return x * lax.rsqrt(jnp.mean(x * x, axis=-1, keepdims=True) + EPS)


def _adaln_kernel(c_ref, w_ref, b_ref, o_ref):
    c = c_ref[...]
    a = c * jax.nn.sigmoid(c)
    o_ref[...] = jnp.dot(a, w_ref[...], preferred_element_type=F32) + b_ref[...]


def _adaln_mod(cc, w_ada, b_ada, tn=1024):
    n = w_ada.shape[1]
    return pl.pallas_call(
        _adaln_kernel,
        grid=(n // tn,),
        in_specs=[
            pl.BlockSpec((MOD_ROWS, D_MODEL), lambda j: (0, 0)),
            pl.BlockSpec((D_MODEL, tn), lambda j: (0, j)),
            pl.BlockSpec((1, tn), lambda j: (0, j)),
        ],
        out_specs=pl.BlockSpec((MOD_ROWS, tn), lambda j: (0, j)),
        out_shape=jax.ShapeDtypeStruct((MOD_ROWS, n), F32),
        compiler_params=_params("arbitrary"),
        name="adaln_mod",
    )(cc, w_ada, b_ada)


def _norm_modulate(x, nw, mod_ref, shift_idx):
    sh = mod_ref[0, shift_idx:shift_idx + 1, :]
    sc = mod_ref[0, shift_idx + 1:shift_idx + 2, :]
    return (_rms(x) * nw) * (1.0 + sc) + sh


def _rope(x, cos, sin_signed, even_lane):
    partner = jnp.where(even_lane, pltpu.roll(x, HEAD_DIM - 1, axis=1), pltpu.roll(x, 1, axis=1))
    return x * cos + partner * sin_signed


def _ctx_kv_kernel(x_ref, mod_ref, nw_ref, w_ref, kn_ref, k_ref, v_ref):
    nw = nw_ref[...]
    kn = kn_ref[...]
    for r in range(x_ref.shape[0] // ROW_SUB):
        rows = slice(r * ROW_SUB, (r + 1) * ROW_SUB)
        h = _norm_modulate(x_ref[rows, :], nw, mod_ref, 0)
        kv = jnp.dot(h.astype(BF16), w_ref[...], preferred_element_type=F32)
        for j in range(N_KV_HEADS):
            cols = slice(j * HEAD_DIM, (j + 1) * HEAD_DIM)
            k_ref[rows, cols] = (_rms(kv[:, cols]) * kn).astype(BF16)
        v_ref[rows, :] = kv[:, KV_W:].astype(BF16)


def _ctx_kv(ctx, mod3, norm_w, w_in_bf, k_norm, tm=1024):
    b, s, _ = ctx.shape
    kv_block = (POOL_W + ATTN_W) // (2 * KV_W)
    k, v = pl.pallas_call(
        _ctx_kv_kernel,
        grid=(b * s // tm,),
        in_specs=[
            pl.BlockSpec((tm, D_MODEL), lambda i: (i, 0)),
            pl.BlockSpec((1, N_MOD, D_MODEL), lambda i: (MOD_ROWS // 2, 0, 0)),
            pl.BlockSpec((1, D_MODEL), lambda i: (0, 0)),
            pl.BlockSpec((D_MODEL, 2 * KV_W), lambda i: (0, kv_block)),
            pl.BlockSpec((1, HEAD_DIM), lambda i: (0, 0)),
        ],
        out_specs=[pl.BlockSpec((tm, KV_W), lambda i: (i, 0))] * 2,
        out_shape=[jax.ShapeDtypeStruct((b * s, KV_W), BF16)] * 2,
        compiler_params=_params("arbitrary"),
        name="ctx_kv",
    )(ctx.reshape(b * s, D_MODEL), mod3, norm_w, w_in_bf, k_norm)
    return k.reshape(b, s, KV_W), v.reshape(b, s, KV_W)


def _in_proj_kernel(x_ref, mod_ref, nw_ref, w_ref, qn_ref, kn_ref, cos_ref, sin_ref,
                    p_ref, q_ref, k_ref, v_ref):
    nw = nw_ref[...]
    qn = qn_ref[...] * Q_SCALE
    kn = kn_ref[...]
    even_lane = (lax.broadcasted_iota(jnp.int32, (ROW_SUB, HEAD_DIM), 1) % 2) == 0
    for r in range(x_ref.shape[1] // ROW_SUB):
        rows = slice(r * ROW_SUB, (r + 1) * ROW_SUB)
        h = _norm_modulate(x_ref[0, rows, :], nw, mod_ref, 0)
        proj = jnp.dot(h.astype(BF16), w_ref[...], preferred_element_type=F32)
        p_ref[0, rows, :] = proj[:, :POOL_W]
        cos = cos_ref[rows, :]
        sin = sin_ref[rows, :]
        for j in range(N_HEADS):
            lo = POOL_W + j * HEAD_DIM
            qh = _rms(proj[:, lo:lo + HEAD_DIM]) * qn
            q_ref[0, rows, j * HEAD_DIM:(j + 1) * HEAD_DIM] = _rope(qh, cos, sin, even_lane).astype(BF16)
        for j in range(N_KV_HEADS):
            lo = POOL_W + ATTN_W + j * HEAD_DIM
            kh = _rms(proj[:, lo:lo + HEAD_DIM]) * kn
            k_ref[0, rows, j * HEAD_DIM:(j + 1) * HEAD_DIM] = _rope(kh, cos, sin, even_lane).astype(BF16)
        v_ref[0, rows, :] = proj[:, POOL_W + ATTN_W + KV_W:].astype(BF16)


def _in_proj(x, mod3, norm_w, w_in_bf, q_norm, k_norm, cos_full, sin_signed, tm=1024):
    b, t, _ = x.shape
    row = lambda i, j: (i, j, 0)
    const2 = lambda i, j: (0, 0)
    return pl.pallas_call(
        _in_proj_kernel,
        grid=(b, t // tm),
        in_specs=[
            pl.BlockSpec((1, tm, D_MODEL), row),
            pl.BlockSpec((1, N_MOD, D_MODEL), lambda i, j: (i, 0, 0)),
            pl.BlockSpec((1, D_MODEL), const2),
            pl.BlockSpec((D_MODEL, PROJ_W), const2, pipeline_mode=pl.Buffered(1)),
            pl.BlockSpec((1, HEAD_DIM), const2),
            pl.BlockSpec((1, HEAD_DIM), const2),
            pl.BlockSpec((tm, HEAD_DIM), lambda i, j: (j, 0)),
            pl.BlockSpec((tm, HEAD_DIM), lambda i, j: (j, 0)),
        ],
        out_specs=[
            pl.BlockSpec((1, tm, POOL_W), row),
            pl.BlockSpec((1, tm, ATTN_W), row),
            pl.BlockSpec((1, tm, KV_W), row),
            pl.BlockSpec((1, tm, KV_W), row),
        ],
        out_shape=[
            jax.ShapeDtypeStruct((b, t, POOL_W), F32),
            jax.ShapeDtypeStruct((b, t, ATTN_W), BF16),
            jax.ShapeDtypeStruct((b, t, KV_W), BF16),
            jax.ShapeDtypeStruct((b, t, KV_W), BF16),
        ],
        compiler_params=_params("arbitrary", "arbitrary"),
        name="in_proj",
    )(x, mod3, norm_w, w_in_bf, q_norm, k_norm, cos_full, sin_signed)


def _pool_inv_counts(t):
    pos = jnp.arange(t, dtype=jnp.int32)[None, :]
    half = jnp.array(POOL_WINDOWS, dtype=jnp.int32)[:, None] // 2
    count = jnp.minimum(pos + half, t) - jnp.maximum(pos - half, 0)
    return jnp.broadcast_to((1.0 / count.astype(F32))[:, :, None], (N_POOL_GROUPS, t, HEAD_DIM))


def _pool_mix_rows(p_ref, prev_ref, next_ref, inv_ref, w_ref, s_ref, o_ref, has_prev, has_next):
    t = p_ref.shape[1]
    n = t + 2 * POOL_PAD
    for g, win in enumerate(POOL_WINDOWS):
        half = win // 2
        cols = slice(g * POOL_GROUP_W, (g + 1) * POOL_GROUP_W)
        u = p_ref[0, :, cols]
        top = jnp.where(has_prev, prev_ref[0, :, cols], 0.0)
        bottom = jnp.where(has_next, next_ref[0, :, cols], 0.0)
        f = jnp.concatenate([top, u, bottom], axis=0)
        step = 1
        while step < win:
            f = f + pltpu.roll(f, n - step, axis=0)
            step *= 2
        if half != POOL_PAD:
            f = pltpu.roll(f, n - (POOL_PAD - half), axis=0)
        inv = inv_ref[g]
        pooled = f[:t] * jnp.concatenate([inv] * (POOL_GROUP_W // inv.shape[1]), axis=1) - u
        mixed = jnp.dot(pooled.astype(BF16), w_ref[g].astype(BF16), preferred_element_type=F32)
        o_ref[0, :, cols] = (mixed * s_ref[:, cols]).astype(BF16)


def _attn_kernel(q_ref, kx_ref, kc_ref, vx_ref, vc_ref,
                 p_ref, prev_ref, next_ref, inv_ref, pw_ref, ps_ref, *refs, n_parts):
    n_cast = (len(refs) - 2) // 2
    o_ref, pool_ref = refs[n_cast], refs[n_cast + 1]
    kx = kx_ref[0]
    kc = kc_ref[0]
    vx = jnp.concatenate([vx_ref[0], jnp.ones(vx_ref.shape[1:], BF16)], axis=1)
    vc = jnp.concatenate([vc_ref[0], jnp.ones(vc_ref.shape[1:], BF16)], axis=1)
    nt = (((1,), (1,)), ((), ()))
    for r in range(q_ref.shape[1] // ATTN_SUB):
        rows = slice(r * ATTN_SUB, (r + 1) * ATTN_SUB)
        for g in range(Q_PER_KV):
            cols = slice(g * HEAD_DIM, (g + 1) * HEAD_DIM)
            q = q_ref[0, rows, cols]
            sx = lax.dot_general(q, kx, nt, preferred_element_type=F32)
            sc = lax.dot_general(q, kc, nt, preferred_element_type=F32)
            m = jnp.maximum(jnp.max(sx, axis=-1, keepdims=True), jnp.max(sc, axis=-1, keepdims=True))
            px = jnp.exp2(sx - m)
            pc = jnp.exp2(sc - m)
            o = (jnp.dot(px.astype(BF16), vx, preferred_element_type=F32)
                 + jnp.dot(pc.astype(BF16), vc, preferred_element_type=F32))
            o_ref[0, rows, cols] = (o[:, :HEAD_DIM] * (1.0 / o[:, HEAD_DIM:])).astype(BF16)
    part = pl.program_id(1) * pl.num_programs(2) + pl.program_id(2)
    _pool_mix_rows(p_ref, prev_ref, next_ref, inv_ref, pw_ref, ps_ref, pool_ref,
                   part > 0, part < n_parts - 1)
    for src_ref, dst_ref in zip(refs[:n_cast], refs[n_cast + 2:]):
        dst_ref[...] = src_ref[...].astype(BF16)


def _attention(q, k_x, v_x, k_c, v_c, p, pool_w, pool_scale, cast_f32, tq=1024):
    b, t, _ = q.shape
    s = k_c.shape[1]
    qw = Q_PER_KV * HEAD_DIM
    n_t = t // tq
    n_parts = N_KV_HEADS * n_t
    n_steps = b * n_parts
    pool_rows = t // n_parts
    halo_blocks = pool_rows // POOL_PAD
    assert pool_rows % POOL_PAD == 0 and all(a.shape[0] % (16 * n_steps) == 0 for a in cast_f32)
    part = lambda h, j: h * n_t + j
    kv_map = lambda i, h, j: (i, 0, h)
    pool_map = lambda i, h, j: (i, part(h, j), 0)
    slab = lambda a: pl.BlockSpec((a.shape[0] // n_steps, a.shape[1]),
                                  lambda i, h, j: (i * n_parts + part(h, j), 0))
    out = pl.pallas_call(
        functools.partial(_attn_kernel, n_parts=n_parts),
        grid=(b, N_KV_HEADS, n_t),
        in_specs=[
            pl.BlockSpec((1, tq, qw), lambda i, h, j: (i, j, h)),
            pl.BlockSpec((1, t, HEAD_DIM), kv_map),
            pl.BlockSpec((1, s, HEAD_DIM), kv_map),
            pl.BlockSpec((1, t, HEAD_DIM), kv_map),
            pl.BlockSpec((1, s, HEAD_DIM), kv_map),
            pl.BlockSpec((1, pool_rows, POOL_W), pool_map),
            pl.BlockSpec((1, POOL_PAD, POOL_W),
                         lambda i, h, j: (i, jnp.maximum(part(h, j) * halo_blocks - 1, 0), 0)),
            pl.BlockSpec((1, POOL_PAD, POOL_W),
                         lambda i, h, j: (i, jnp.minimum((part(h, j) + 1) * halo_blocks,
                                                         n_parts * halo_blocks - 1), 0)),
            pl.BlockSpec((N_POOL_GROUPS, pool_rows, HEAD_DIM), lambda i, h, j: (0, part(h, j), 0)),
            pl.BlockSpec((N_POOL_GROUPS, POOL_GROUP_W, POOL_GROUP_W), lambda i, h, j: (0, 0, 0)),
            pl.BlockSpec((1, POOL_W), lambda i, h, j: (0, 0)),
            *[slab(a) for a in cast_f32],
        ],
        out_specs=[pl.BlockSpec((1, tq, qw), lambda i, h, j: (i, j, h)),
                   pl.BlockSpec((1, pool_rows, POOL_W), pool_map),
                   *[slab(a) for a in cast_f32]],
        out_shape=[jax.ShapeDtypeStruct((b, t, ATTN_W), BF16),
                   jax.ShapeDtypeStruct((b, t, POOL_W), BF16),
                   *[jax.ShapeDtypeStruct(a.shape, BF16) for a in cast_f32]],
        compiler_params=_params("arbitrary", "arbitrary", "arbitrary"),
        name="attention",
    )(q, k_x, k_c, v_x, v_c, p, p, p, _pool_inv_counts(t), pool_w, pool_scale, *cast_f32)
    return out[0], out[1], out[2:]


def _out_proj_kernel(x_ref, *refs):
    lhs_refs, (mod_ref, nw_ref), w_refs, (o_ref, h_ref) = refs[:4], refs[4:6], refs[6:-2], refs[-2:]
    gate = mod_ref[0, 2:3, :]
    nw = nw_ref[...]
    for r in range(x_ref.shape[1] // ROW_SUB):
        rows = slice(r * ROW_SUB, (r + 1) * ROW_SUB)
        lhs = jnp.concatenate([ref[0, rows, :] for ref in lhs_refs], axis=1)
        mix = jnp.concatenate([jnp.dot(lhs, w_ref[...], preferred_element_type=F32)
                               for w_ref in w_refs], axis=1)
        x1 = x_ref[0, rows, :] + gate * mix
        o_ref[0, rows, :] = x1
        h_ref[0, rows, :] = _norm_modulate(x1, nw, mod_ref, 3).astype(BF16)


def _out_proj(x, pool_mix, attn, mod3, norm_w, w_out_bf, tm=1024):
    b, t, _ = x.shape
    row = lambda i, j: (i, j, 0)
    lhs_window = lambda c: pl.BlockSpec((1, tm, MXU_WINDOW_W), lambda i, j: (i, j, c))
    w_window = lambda c: pl.BlockSpec((MIX_W, MXU_WINDOW_W), lambda i, j: (0, c),
                                      pipeline_mode=pl.Buffered(1))
    n_w = D_MODEL // MXU_WINDOW_W
    return pl.pallas_call(
        _out_proj_kernel,
        grid=(b, t // tm),
        in_specs=[
            pl.BlockSpec((1, tm, D_MODEL), row),
            lhs_window(0), lhs_window(1), lhs_window(0), lhs_window(1),
            pl.BlockSpec((1, N_MOD, D_MODEL), lambda i, j: (i, 0, 0)),
            pl.BlockSpec((1, D_MODEL), lambda i, j: (0, 0)),
            *[w_window(c) for c in range(n_w)],
        ],
        out_specs=[pl.BlockSpec((1, tm, D_MODEL), row), pl.BlockSpec((1, tm, D_MODEL), row)],
        out_shape=[jax.ShapeDtypeStruct(x.shape, F32), jax.ShapeDtypeStruct(x.shape, BF16)],
        compiler_params=_params("arbitrary", "arbitrary"),
        name="out_proj",
    )(x, pool_mix, pool_mix, attn, attn, mod3, norm_w, *([w_out_bf] * n_w))


def _ffn_kernel(h_ref, x_ref, gate_ref, wg_ref, wu_ref, wd_ref, fn_ref, o_ref, act_ref, *, n_f, tf, tn):
    s = pl.program_id(2)

    @pl.when(s < n_f)
    def _():
        h = h_ref[0]
        for c in range(tf // FFN_CHUNK_W):
            cols = slice(c * FFN_CHUNK_W, (c + 1) * FFN_CHUNK_W)
            gate = jnp.dot(h, wg_ref[:, cols], preferred_element_type=F32)
            up = jnp.dot(h, wu_ref[:, cols], preferred_element_type=F32)
            half_gate = 0.5 * gate
            act_ref[s, :, cols] = (half_gate * (1.0 + jnp.tanh(half_gate)) * up).astype(BF16)

    @pl.when(s >= n_f)
    def _():
        y = jnp.dot(act_ref[0], wd_ref[0:tf, :], preferred_element_type=F32)
        for f in range(1, n_f):
            y += jnp.dot(act_ref[f], wd_ref[f * tf:(f + 1) * tf, :], preferred_element_type=F32)
        cols = pl.ds(pl.multiple_of((s - n_f) * tn, tn), tn)
        o_ref[0, :, cols] = x_ref[0] + gate_ref[0, 5:6, :] * y

    @pl.when(s == pl.num_programs(2) - 1)
    def _():
        o_ref[0] = _rms(o_ref[0]) * fn_ref[...]


def _ffn(h, x1, mod3, wg_bf, wu_bf, wd_bf, final_norm, tm=1024, tf=512, tn=256):
    b, t, _ = h.shape
    d_ff = wg_bf.shape[1]
    n_f = d_ff // tf
    n_n = D_MODEL // tn
    row = lambda i, j, s: (i, j, 0)
    up_blk = lambda i, j, s: (0, jnp.minimum(s, n_f - 1))
    down_blk = lambda i, j, s: (0, jnp.maximum(s - n_f, 0))
    return pl.pallas_call(
        functools.partial(_ffn_kernel, n_f=n_f, tf=tf, tn=tn),
        grid=(b, t // tm, n_f + n_n),
        in_specs=[
            pl.BlockSpec((1, tm, D_MODEL), row),
            pl.BlockSpec((1, tm, tn), lambda i, j, s: (i, j, jnp.maximum(s - n_f, 0))),
            pl.BlockSpec((1, N_MOD, tn), lambda i, j, s: (i, 0, jnp.maximum(s - n_f, 0))),
            pl.BlockSpec((D_MODEL, tf), up_blk),
            pl.BlockSpec((D_MODEL, tf), up_blk),
            pl.BlockSpec((d_ff, tn), down_blk),
            pl.BlockSpec((1, D_MODEL), lambda i, j, s: (0, 0)),
        ],
        out_specs=pl.BlockSpec((1, tm, D_MODEL), row),
        out_shape=jax.ShapeDtypeStruct(x1.shape, F32),
        scratch_shapes=[pltpu.VMEM((n_f, tm, tf), BF16)],
        compiler_params=_params("arbitrary", "arbitrary", "arbitrary"),
        name="ffn",
    )(h, x1, mod3, wg_bf, wu_bf, wd_bf, final_norm)


def _rope_tables(t):
    n_rows = t // GRID_W
    rows = jnp.repeat(jnp.arange(n_rows, dtype=F32), GRID_W)
    cols = jnp.tile(jnp.arange(GRID_W, dtype=F32), n_rows)
    freqs = ROPE_THETA ** (-jnp.arange(0, AXIS_ROT, 2, dtype=F32) / AXIS_ROT)
    ang = jnp.concatenate([rows[:, None] * freqs, cols[:, None] * freqs], axis=-1)
    cos_full = jnp.repeat(jnp.cos(ang), 2, axis=-1)
    sin = jnp.sin(ang)
    sin_signed = jnp.stack([-sin, sin], axis=-1).reshape(t, HEAD_DIM)
    return cos_full, sin_signed


def kernel(x, c, ctx, c_ctx, w_ada, b_ada, norm_mix, norm_ffn, w_in, pool_w, pool_scale,
           q_norm, k_norm, w_out, w_gate, w_up, w_down, final_norm):
    depth = w_ada.shape[0]
    assert depth == 1, "context tokens are only updated between layers; one layer is implemented"
    b, t, _ = x.shape
    cos_full, sin_signed = _rope_tables(t)

    cc = jnp.zeros((MOD_ROWS, D_MODEL), F32).at[:b].set(c).at[MOD_ROWS // 2].set(c_ctx)
    mod3 = _adaln_mod(cc, w_ada[0], b_ada).reshape(MOD_ROWS, N_MOD, D_MODEL)

    w_in_bf = w_in[0].astype(BF16)
    k_c, v_c = _ctx_kv(ctx, mod3, norm_mix, w_in_bf, k_norm)
    p, q, k_x, v_x = _in_proj(x, mod3, norm_mix, w_in_bf, q_norm, k_norm, cos_full, sin_signed)
    attn, pooled, (wo_bf, wg_bf, wu_bf, wd_bf) = _attention(
        q, k_x, v_x, k_c, v_c, p, pool_w[0], pool_scale, (w_out[0], w_gate[0], w_up[0], w_down[0]))
    x1, hf = _out_proj(x, pooled, attn, mod3, norm_ffn, wo_bf)
    return _ffn(hf, x1, mod3, wg_bf, wu_bf, wd_bf, final_norm.reshape(1, D_MODEL))
```

```python
import functools
import math

import jax
import jax.numpy as jnp
import numpy as np
from jax import lax
from jax.experimental import pallas as pl
from jax.experimental.pallas import tpu as pltpu

D_MODEL = 2048
GRID_W = 64
HEAD_DIM = 128
N_HEADS = 8
N_KV_HEADS = 2
Q_PER_KV = N_HEADS // N_KV_HEADS
ATTN_W = N_HEADS * HEAD_DIM
KV_W = N_KV_HEADS * HEAD_DIM
POOL_WINDOWS = (2, 4, 8, 16)
N_POOL_GROUPS = len(POOL_WINDOWS)
POOL_W = D_MODEL // 2
POOL_GROUP_W = POOL_W // N_POOL_GROUPS
MIX_W = POOL_W + ATTN_W
PROJ_W = POOL_W + ATTN_W + 2 * KV_W
ROPE_THETA = 10000.0
AXIS_ROT = HEAD_DIM // 2
EPS = 1e-6
N_MOD = 6
MOD_ROWS = 16
MOD_TILE = 8
POOL_PAD = 8
ROW_SUB = 256
MXU_WINDOW_W = 512
ATTN_SUB = 256
FFN_CHUNK_W = 256
Q_SCALE = math.log2(math.e) / math.sqrt(HEAD_DIM)

F32 = jnp.float32
BF16 = jnp.bfloat16

VMEM_LIMIT_BYTES = 60 * 1024 * 1024


def _params(*sem):
    return pltpu.CompilerParams(dimension_semantics=sem, vmem_limit_bytes=VMEM_LIMIT_BYTES)


def _rms(x):
    return x * lax.rsqrt(jnp.mean(x * x, axis=-1, keepdims=True) + EPS)


def _adaln_kernel(c_ref, cctx_ref, w_ref, b_ref, o_ref):
    c = jnp.concatenate([c_ref[...], jnp.broadcast_to(cctx_ref[...], (MOD_ROWS - MOD_TILE, D_MODEL))], axis=0)
    a = c * jax.nn.sigmoid(c)
    o_ref[0] = jnp.dot(a, w_ref[...], preferred_element_type=F32) + b_ref[...]


def _adaln_mod(c, c_ctx, w_ada, b_ada):
    assert c.shape[0] == MOD_TILE
    return pl.pallas_call(
        _adaln_kernel,
        grid=(N_MOD,),
        in_specs=[
            pl.BlockSpec((MOD_TILE, D_MODEL), lambda j: (0, 0)),
            pl.BlockSpec((1, D_MODEL), lambda j: (0, 0)),
            pl.BlockSpec((D_MODEL, D_MODEL), lambda j: (0, j)),
            pl.BlockSpec((1, D_MODEL), lambda j: (0, j)),
        ],
        out_specs=pl.BlockSpec((1, MOD_ROWS, D_MODEL), lambda j: (j, 0, 0)),
        out_shape=jax.ShapeDtypeStruct((N_MOD, MOD_ROWS, D_MODEL), F32),
        compiler_params=_params("arbitrary"),
        name="adaln_mod",
    )(c, c_ctx.reshape(1, D_MODEL), w_ada, b_ada)


def _mod_row(mod_ref, chunk, row):
    return mod_ref[chunk, pl.ds(row, 1), :]


def _norm_modulate(x, nw, mod_ref, shift_idx, row):
    sh = _mod_row(mod_ref, shift_idx, row)
    sc = _mod_row(mod_ref, shift_idx + 1, row)
    return (_rms(x) * nw) * (1.0 + sc) + sh


def _rope(x, cos, sin_signed, even_lane):
    partner = jnp.where(even_lane, pltpu.roll(x, HEAD_DIM - 1, axis=1), pltpu.roll(x, 1, axis=1))
    return x * cos + partner * sin_signed


def _ctx_kv_kernel(x_ref, mod_ref, nw_ref, w_ref, kn_ref, k_ref, v_ref):
    nw = nw_ref[...]
    kn = kn_ref[...]
    for r in range(x_ref.shape[0] // ROW_SUB):
        rows = slice(r * ROW_SUB, (r + 1) * ROW_SUB)
        h = _norm_modulate(x_ref[rows, :], nw, mod_ref, 0, 0)
        kv = jnp.dot(h.astype(BF16), w_ref[...], preferred_element_type=F32)
        for j in range(N_KV_HEADS):
            cols = slice(j * HEAD_DIM, (j + 1) * HEAD_DIM)
            k_ref[rows, cols] = (_rms(kv[:, cols]) * kn).astype(BF16)
        v_ref[rows, :] = kv[:, KV_W:].astype(BF16)


def _ctx_kv(ctx, mod3, norm_w, w_in_bf, k_norm, tm=1024):
    b, s, _ = ctx.shape
    kv_block = (POOL_W + ATTN_W) // (2 * KV_W)
    k, v = pl.pallas_call(
        _ctx_kv_kernel,
        grid=(b * s // tm,),
        in_specs=[
            pl.BlockSpec((tm, D_MODEL), lambda i: (i, 0)),
            pl.BlockSpec((N_MOD, MOD_TILE, D_MODEL), lambda i: (0, 1, 0)),
            pl.BlockSpec((1, D_MODEL), lambda i: (0, 0)),
            pl.BlockSpec((D_MODEL, 2 * KV_W), lambda i: (0, kv_block)),
            pl.BlockSpec((1, HEAD_DIM), lambda i: (0, 0)),
        ],
        out_specs=[pl.BlockSpec((tm, KV_W), lambda i: (i, 0))] * 2,
        out_shape=[jax.ShapeDtypeStruct((b * s, KV_W), BF16)] * 2,
        compiler_params=_params("arbitrary"),
        name="ctx_kv",
    )(ctx.reshape(b * s, D_MODEL), mod3, norm_w, w_in_bf, k_norm)
    return k.reshape(b, s, KV_W), v.reshape(b, s, KV_W)


def _in_proj_kernel(x_ref, mod_ref, nw_ref, w_ref, qn_ref, kn_ref, cos_ref, sin_ref,
                    p_ref, q_ref, k_ref, v_ref):
    nw = nw_ref[...]
    qn = qn_ref[...] * Q_SCALE
    kn = kn_ref[...]
    even_lane = (lax.broadcasted_iota(jnp.int32, (ROW_SUB, HEAD_DIM), 1) % 2) == 0
    for r in range(x_ref.shape[1] // ROW_SUB):
        rows = slice(r * ROW_SUB, (r + 1) * ROW_SUB)
        h = _norm_modulate(x_ref[0, rows, :], nw, mod_ref, 0, pl.program_id(0))
        proj = jnp.dot(h.astype(BF16), w_ref[...], preferred_element_type=F32)
        p_ref[0, rows, :] = proj[:, :POOL_W]
        cos = cos_ref[rows, :]
        sin = sin_ref[rows, :]
        for j in range(N_HEADS):
            lo = POOL_W + j * HEAD_DIM
            qh = _rms(proj[:, lo:lo + HEAD_DIM]) * qn
            q_ref[0, rows, j * HEAD_DIM:(j + 1) * HEAD_DIM] = _rope(qh, cos, sin, even_lane).astype(BF16)
        for j in range(N_KV_HEADS):
            lo = POOL_W + ATTN_W + j * HEAD_DIM
            kh = _rms(proj[:, lo:lo + HEAD_DIM]) * kn
            k_ref[0, rows, j * HEAD_DIM:(j + 1) * HEAD_DIM] = _rope(kh, cos, sin, even_lane).astype(BF16)
        v_ref[0, rows, :] = proj[:, POOL_W + ATTN_W + KV_W:].astype(BF16)


def _in_proj(x, mod3, norm_w, w_in_bf, q_norm, k_norm, cos_full, sin_signed, tm=1024):
    b, t, _ = x.shape
    row = lambda i, j: (i, j, 0)
    const2 = lambda i, j: (0, 0)
    return pl.pallas_call(
        _in_proj_kernel,
        grid=(b, t // tm),
        in_specs=[
            pl.BlockSpec((1, tm, D_MODEL), row),
            pl.BlockSpec((N_MOD, MOD_TILE, D_MODEL), lambda i, j: (0, 0, 0)),
            pl.BlockSpec((1, D_MODEL), const2),
            pl.BlockSpec((D_MODEL, PROJ_W), const2, pipeline_mode=pl.Buffered(1)),
            pl.BlockSpec((1, HEAD_DIM), const2),
            pl.BlockSpec((1, HEAD_DIM), const2),
            pl.BlockSpec((tm, HEAD_DIM), lambda i, j: (j, 0)),
            pl.BlockSpec((tm, HEAD_DIM), lambda i, j: (j, 0)),
        ],
        out_specs=[
            pl.BlockSpec((1, tm, POOL_W), row),
            pl.BlockSpec((1, tm, ATTN_W), row),
            pl.BlockSpec((1, tm, KV_W), row),
            pl.BlockSpec((1, tm, KV_W), row),
        ],
        out_shape=[
            jax.ShapeDtypeStruct((b, t, POOL_W), F32),
            jax.ShapeDtypeStruct((b, t, ATTN_W), BF16),
            jax.ShapeDtypeStruct((b, t, KV_W), BF16),
            jax.ShapeDtypeStruct((b, t, KV_W), BF16),
        ],
        compiler_params=_params("arbitrary", "arbitrary"),
        name="in_proj",
    )(x, mod3, norm_w, w_in_bf, q_norm, k_norm, cos_full, sin_signed)


def _pool_inv_counts(t):
    pos = np.arange(t)[None, :]
    half = np.array(POOL_WINDOWS)[:, None] // 2
    count = np.minimum(pos + half, t) - np.maximum(pos - half, 0)
    inv = (np.float32(1.0) / count.astype(np.float32))[:, :, None]
    return jnp.asarray(np.broadcast_to(inv, (N_POOL_GROUPS, t, HEAD_DIM)))


def _pool_mix_rows(p_ref, prev_ref, next_ref, inv_ref, w_ref, s_ref, o_ref, has_prev, has_next):
    t = p_ref.shape[1]
    n = t + 2 * POOL_PAD
    for g, win in enumerate(POOL_WINDOWS):
        half = win // 2
        cols = slice(g * POOL_GROUP_W, (g + 1) * POOL_GROUP_W)
        u = p_ref[0, :, cols]
        top = jnp.where(has_prev, prev_ref[0, :, cols], 0.0)
        bottom = jnp.where(has_next, next_ref[0, :, cols], 0.0)
        f = jnp.concatenate([top, u, bottom], axis=0)
        step = 1
        while step < win:
            f = f + pltpu.roll(f, n - step, axis=0)
            step *= 2
        if half != POOL_PAD:
            f = pltpu.roll(f, n - (POOL_PAD - half), axis=0)
        inv = inv_ref[g]
        pooled = f[:t] * jnp.concatenate([inv] * (POOL_GROUP_W // inv.shape[1]), axis=1) - u
        mixed = jnp.dot(pooled.astype(BF16), w_ref[g].astype(BF16), preferred_element_type=F32)
        o_ref[0, :, cols] = (mixed * s_ref[:, cols]).astype(BF16)


def _attn_kernel(q_ref, kx_ref, kc_ref, vx_ref, vc_ref,
                 p_ref, prev_ref, next_ref, inv_ref, pw_ref, ps_ref, *refs, n_parts):
    n_cast = (len(refs) - 2) // 2
    o_ref, pool_ref = refs[n_cast], refs[n_cast + 1]
    kx = kx_ref[0]
    kc = kc_ref[0]
    vx = jnp.concatenate([vx_ref[0], jnp.ones(vx_ref.shape[1:], BF16)], axis=1)
    vc = jnp.concatenate([vc_ref[0], jnp.ones(vc_ref.shape[1:], BF16)], axis=1)
    nt = (((1,), (1,)), ((), ()))
    for r in range(q_ref.shape[1] // ATTN_SUB):
        rows = slice(r * ATTN_SUB, (r + 1) * ATTN_SUB)
        for g in range(Q_PER_KV):
            cols = slice(g * HEAD_DIM, (g + 1) * HEAD_DIM)
            q = q_ref[0, rows, cols]
            sx = lax.dot_general(q, kx, nt, preferred_element_type=F32)
            sc = lax.dot_general(q, kc, nt, preferred_element_type=F32)
            m = jnp.maximum(jnp.max(sx, axis=-1, keepdims=True), jnp.max(sc, axis=-1, keepdims=True))
            px = jnp.exp2(sx - m)
            pc = jnp.exp2(sc - m)
            o = (jnp.dot(px.astype(BF16), vx, preferred_element_type=F32)
                 + jnp.dot(pc.astype(BF16), vc, preferred_element_type=F32))
            o_ref[0, rows, cols] = (o[:, :HEAD_DIM] * (1.0 / o[:, HEAD_DIM:])).astype(BF16)
    part = pl.program_id(1) * pl.num_programs(2) + pl.program_id(2)
    _pool_mix_rows(p_ref, prev_ref, next_ref, inv_ref, pw_ref, ps_ref, pool_ref,
                   part > 0, part < n_parts - 1)
    for src_ref, dst_ref in zip(refs[:n_cast], refs[n_cast + 2:]):
        dst_ref[...] = src_ref[...].astype(BF16)


def _attention(q, k_x, v_x, k_c, v_c, p, pool_w, pool_scale, cast_f32, tq=1024):
    b, t, _ = q.shape
    s = k_c.shape[1]
    qw = Q_PER_KV * HEAD_DIM
    n_t = t // tq
    n_parts = N_KV_HEADS * n_t
    n_steps = b * n_parts
    pool_rows = t // n_parts
    halo_blocks = pool_rows // POOL_PAD
    assert pool_rows % POOL_PAD == 0 and all(a.shape[0] % (16 * n_steps) == 0 for a in cast_f32)
    part = lambda h, j: h * n_t + j
    kv_map = lambda i, h, j: (i, 0, h)
    pool_map = lambda i, h, j: (i, part(h, j), 0)
    slab = lambda a: pl.BlockSpec((a.shape[0] // n_steps, a.shape[1]),
                                  lambda i, h, j: (i * n_parts + part(h, j), 0))
    out = pl.pallas_call(
        functools.partial(_attn_kernel, n_parts=n_parts),
        grid=(b, N_KV_HEADS, n_t),
        in_specs=[
            pl.BlockSpec((1, tq, qw), lambda i, h, j: (i, j, h)),
            pl.BlockSpec((1, t, HEAD_DIM), kv_map),
            pl.BlockSpec((1, s, HEAD_DIM), kv_map),
            pl.BlockSpec((1, t, HEAD_DIM), kv_map),
            pl.BlockSpec((1, s, HEAD_DIM), kv_map),
            pl.BlockSpec((1, pool_rows, POOL_W), pool_map),
            pl.BlockSpec((1, POOL_PAD, POOL_W),
                         lambda i, h, j: (i, jnp.maximum(part(h, j) * halo_blocks - 1, 0), 0)),
            pl.BlockSpec((1, POOL_PAD, POOL_W),
                         lambda i, h, j: (i, jnp.minimum((part(h, j) + 1) * halo_blocks,
                                                         n_parts * halo_blocks - 1), 0)),
            pl.BlockSpec((N_POOL_GROUPS, pool_rows, HEAD_DIM), lambda i, h, j: (0, part(h, j), 0)),
            pl.BlockSpec((N_POOL_GROUPS, POOL_GROUP_W, POOL_GROUP_W), lambda i, h, j: (0, 0, 0)),
            pl.BlockSpec((1, POOL_W), lambda i, h, j: (0, 0)),
            *[slab(a) for a in cast_f32],
        ],
        out_specs=[pl.BlockSpec((1, tq, qw), lambda i, h, j: (i, j, h)),
                   pl.BlockSpec((1, pool_rows, POOL_W), pool_map),
                   *[slab(a) for a in cast_f32]],
        out_shape=[jax.ShapeDtypeStruct((b, t, ATTN_W), BF16),
                   jax.ShapeDtypeStruct((b, t, POOL_W), BF16),
                   *[jax.ShapeDtypeStruct(a.shape, BF16) for a in cast_f32]],
        compiler_params=_params("arbitrary", "arbitrary", "arbitrary"),
        name="attention",
    )(q, k_x, k_c, v_x, v_c, p, p, p, _pool_inv_counts(t), pool_w, pool_scale, *cast_f32)
    return out[0], out[1], out[2:]


def _out_proj_kernel(x_ref, *refs):
    lhs_refs, (mod_ref, nw_ref), w_refs, (o_ref, h_ref) = refs[:4], refs[4:6], refs[6:-2], refs[-2:]
    batch = pl.program_id(0)
    gate = _mod_row(mod_ref, 2, batch)
    nw = nw_ref[...]
    for r in range(x_ref.shape[1] // ROW_SUB):
        rows = slice(r * ROW_SUB, (r + 1) * ROW_SUB)
        lhs = jnp.concatenate([ref[0, rows, :] for ref in lhs_refs], axis=1)
        mix = jnp.concatenate([jnp.dot(lhs, w_ref[...], preferred_element_type=F32)
                               for w_ref in w_refs], axis=1)
        x1 = x_ref[0, rows, :] + gate * mix
        o_ref[0, rows, :] = x1
        h_ref[0, rows, :] = _norm_modulate(x1, nw, mod_ref, 3, batch).astype(BF16)


def _out_proj(x, pool_mix, attn, mod3, norm_w, w_out_bf, tm=1024):
    b, t, _ = x.shape
    row = lambda i, j: (i, j, 0)
    lhs_window = lambda c: pl.BlockSpec((1, tm, MXU_WINDOW_W), lambda i, j: (i, j, c))
    w_window = lambda c: pl.BlockSpec((MIX_W, MXU_WINDOW_W), lambda i, j: (0, c),
                                      pipeline_mode=pl.Buffered(1))
    n_w = D_MODEL // MXU_WINDOW_W
    return pl.pallas_call(
        _out_proj_kernel,
        grid=(b, t // tm),
        in_specs=[
            pl.BlockSpec((1, tm, D_MODEL), row),
            lhs_window(0), lhs_window(1), lhs_window(0), lhs_window(1),
            pl.BlockSpec((N_MOD, MOD_TILE, D_MODEL), lambda i, j: (0, 0, 0)),
            pl.BlockSpec((1, D_MODEL), lambda i, j: (0, 0)),
            *[w_window(c) for c in range(n_w)],
        ],
        out_specs=[pl.BlockSpec((1, tm, D_MODEL), row), pl.BlockSpec((1, tm, D_MODEL), row)],
        out_shape=[jax.ShapeDtypeStruct(x.shape, F32), jax.ShapeDtypeStruct(x.shape, BF16)],
        compiler_params=_params("arbitrary", "arbitrary"),
        name="out_proj",
    )(x, pool_mix, pool_mix, attn, attn, mod3, norm_w, *([w_out_bf] * n_w))


def _ffn_kernel(h_ref, x_ref, gate_ref, wg_ref, wu_ref, wd_ref, fn_ref, o_ref, act_ref, *, n_f, tf, tn):
    s = pl.program_id(2)

    @pl.when(s < n_f)
    def _():
        h = h_ref[0]
        for c in range(tf // FFN_CHUNK_W):
            cols = slice(c * FFN_CHUNK_W, (c + 1) * FFN_CHUNK_W)
            gate = jnp.dot(h, wg_ref[:, cols], preferred_element_type=F32)
            up = jnp.dot(h, wu_ref[:, cols], preferred_element_type=F32)
            half_gate = 0.5 * gate
            act_ref[s, :, cols] = (half_gate * (1.0 + jnp.tanh(half_gate)) * up).astype(BF16)

    @pl.when(s >= n_f)
    def _():
        y = jnp.dot(act_ref[0], wd_ref[0:tf, :], preferred_element_type=F32)
        for f in range(1, n_f):
            y += jnp.dot(act_ref[f], wd_ref[f * tf:(f + 1) * tf, :], preferred_element_type=F32)
        cols = pl.ds(pl.multiple_of((s - n_f) * tn, tn), tn)
        o_ref[0, :, cols] = x_ref[0] + _mod_row(gate_ref, 0, pl.program_id(0)) * y

    @pl.when(s == pl.num_programs(2) - 1)
    def _():
        o_ref[0] = _rms(o_ref[0]) * fn_ref[...]


def _ffn(h, x1, mod3, wg_bf, wu_bf, wd_bf, final_norm, tm=1024, tf=512, tn=256):
    b, t, _ = h.shape
    d_ff = wg_bf.shape[1]
    n_f = d_ff // tf
    n_n = D_MODEL // tn
    row = lambda i, j, s: (i, j, 0)
    up_blk = lambda i, j, s: (0, jnp.minimum(s, n_f - 1))
    down_blk = lambda i, j, s: (0, jnp.maximum(s - n_f, 0))
    return pl.pallas_call(
        functools.partial(_ffn_kernel, n_f=n_f, tf=tf, tn=tn),
        grid=(b, t // tm, n_f + n_n),
        in_specs=[
            pl.BlockSpec((1, tm, D_MODEL), row),
            pl.BlockSpec((1, tm, tn), lambda i, j, s: (i, j, jnp.maximum(s - n_f, 0))),
            pl.BlockSpec((1, MOD_TILE, tn), lambda i, j, s: (N_MOD - 1, 0, jnp.maximum(s - n_f, 0))),
            pl.BlockSpec((D_MODEL, tf), up_blk),
            pl.BlockSpec((D_MODEL, tf), up_blk),
            pl.BlockSpec((d_ff, tn), down_blk),
            pl.BlockSpec((1, D_MODEL), lambda i, j, s: (0, 0)),
        ],
        out_specs=pl.BlockSpec((1, tm, D_MODEL), row),
        out_shape=jax.ShapeDtypeStruct(x1.shape, F32),
        scratch_shapes=[pltpu.VMEM((n_f, tm, tf), BF16)],
        compiler_params=_params("arbitrary", "arbitrary", "arbitrary"),
        name="ffn",
    )(h, x1, mod3, wg_bf, wu_bf, wd_bf, final_norm)


def _rope_tables(t):
    n_rows = t // GRID_W
    rows = np.repeat(np.arange(n_rows, dtype=np.float32), GRID_W)
    cols = np.tile(np.arange(GRID_W, dtype=np.float32), n_rows)
    freqs = np.float32(ROPE_THETA) ** (-np.arange(0, AXIS_ROT, 2, dtype=np.float32) / np.float32(AXIS_ROT))
    ang = np.concatenate([rows[:, None] * freqs, cols[:, None] * freqs], axis=-1)
    cos_full = np.repeat(np.cos(ang), 2, axis=-1)
    sin = np.sin(ang)
    sin_signed = np.stack([-sin, sin], axis=-1).reshape(t, HEAD_DIM)
    return jnp.asarray(cos_full, F32), jnp.asarray(sin_signed, F32)


def kernel(x, c, ctx, c_ctx, w_ada, b_ada, norm_mix, norm_ffn, w_in, pool_w, pool_scale,
           q_norm, k_norm, w_out, w_gate, w_up, w_down, final_norm):
    depth = w_ada.shape[0]
    assert depth == 1, "context tokens are only updated between layers; one layer is implemented"
    b, t, _ = x.shape
    cos_full, sin_signed = _rope_tables(t)

    mod3 = _adaln_mod(c, c_ctx, w_ada[0], b_ada)

    w_in_bf = w_in[0].astype(BF16)
    k_c, v_c = _ctx_kv(ctx, mod3, norm_mix, w_in_bf, k_norm)
    p, q, k_x, v_x = _in_proj(x, mod3, norm_mix, w_in_bf, q_norm, k_norm, cos_full, sin_signed)
    attn, pooled, (wo_bf, wg_bf, wu_bf, wd_bf) = _attention(
        q, k_x, v_x, k_c, v_c, p, pool_w[0], pool_scale, (w_out[0], w_gate[0], w_up[0], w_down[0]))
    x1, hf = _out_proj(x, pooled, attn, mod3, norm_ffn, wo_bf)
    return _ffn(hf, x1, mod3, wg_bf, wu_bf, wd_bf, final_norm.reshape(1, D_MODEL))
```

```python
import functools
import math

import jax
import jax.numpy as jnp
import numpy as np
from jax import lax
from jax.experimental import pallas as pl
from jax.experimental.pallas import tpu as pltpu

D_MODEL = 2048
GRID_W = 64
HEAD_DIM = 128
N_HEADS = 8
N_KV_HEADS = 2
Q_PER_KV = N_HEADS // N_KV_HEADS
ATTN_W = N_HEADS * HEAD_DIM
KV_W = N_KV_HEADS * HEAD_DIM
POOL_WINDOWS = (2, 4, 8, 16)
N_POOL_GROUPS = len(POOL_WINDOWS)
POOL_W = D_MODEL // 2
POOL_GROUP_W = POOL_W // N_POOL_GROUPS
MIX_W = POOL_W + ATTN_W
PROJ_W = POOL_W + ATTN_W + 2 * KV_W
ROPE_THETA = 10000.0
AXIS_ROT = HEAD_DIM // 2
EPS = 1e-6
N_MOD = 6
MOD_ROWS = 16
MOD_TILE = 8
POOL_PAD = 8
ROW_SUB = 256
MXU_WINDOW_W = 512
ATTN_SUB = 256
FFN_CHUNK_W = 256
Q_SCALE = math.log2(math.e) / math.sqrt(HEAD_DIM)

F32 = jnp.float32
BF16 = jnp.bfloat16

VMEM_LIMIT_BYTES = 63 * 1024 * 1024


def _params(*sem):
    return pltpu.CompilerParams(dimension_semantics=sem, vmem_limit_bytes=VMEM_LIMIT_BYTES)


def _rms(x):
    return x * lax.rsqrt(jnp.mean(x * x, axis=-1, keepdims=True) + EPS)


def _adaln_kernel(c_ref, cctx_ref, w_ref, b_ref, o_ref):
    c = jnp.concatenate([c_ref[...], jnp.broadcast_to(cctx_ref[...], (MOD_ROWS - MOD_TILE, D_MODEL))], axis=0)
    a = c * jax.nn.sigmoid(c)
    o_ref[0] = jnp.dot(a, w_ref[...], preferred_element_type=F32) + b_ref[...]


def _adaln_mod(c, c_ctx, w_ada, b_ada):
    assert c.shape[0] == MOD_TILE
    return pl.pallas_call(
        _adaln_kernel,
        grid=(N_MOD,),
        in_specs=[
            pl.BlockSpec((MOD_TILE, D_MODEL), lambda j: (0, 0)),
            pl.BlockSpec((1, D_MODEL), lambda j: (0, 0)),
            pl.BlockSpec((D_MODEL, D_MODEL), lambda j: (0, j)),
            pl.BlockSpec((1, D_MODEL), lambda j: (0, j)),
        ],
        out_specs=pl.BlockSpec((1, MOD_ROWS, D_MODEL), lambda j: (j, 0, 0)),
        out_shape=jax.ShapeDtypeStruct((N_MOD, MOD_ROWS, D_MODEL), F32),
        compiler_params=_params("arbitrary"),
        name="adaln_mod",
    )(c, c_ctx.reshape(1, D_MODEL), w_ada, b_ada)


def _mod_row(mod_ref, chunk, row):
    return mod_ref[chunk, pl.ds(row, 1), :]


def _norm_modulate(x, nw, mod_ref, shift_idx, row):
    sh = _mod_row(mod_ref, shift_idx, row)
    sc = _mod_row(mod_ref, shift_idx + 1, row)
    return (_rms(x) * nw) * (1.0 + sc) + sh


def _rope(x, cos, sin_signed, even_lane):
    partner = jnp.where(even_lane, pltpu.roll(x, HEAD_DIM - 1, axis=1), pltpu.roll(x, 1, axis=1))
    return x * cos + partner * sin_signed


def _ctx_kv_kernel(x_ref, mod_ref, nw_ref, w_ref, kn_ref, k_ref, v_ref):
    nw = nw_ref[...]
    kn = kn_ref[...]
    for r in range(x_ref.shape[0] // ROW_SUB):
        rows = slice(r * ROW_SUB, (r + 1) * ROW_SUB)
        h = _norm_modulate(x_ref[rows, :], nw, mod_ref, 0, 0)
        kv = jnp.dot(h.astype(BF16), w_ref[...], preferred_element_type=F32)
        for j in range(N_KV_HEADS):
            cols = slice(j * HEAD_DIM, (j + 1) * HEAD_DIM)
            k_ref[rows, cols] = (_rms(kv[:, cols]) * kn).astype(BF16)
        v_ref[rows, :] = kv[:, KV_W:].astype(BF16)


def _ctx_kv(ctx, mod3, norm_w, w_in_bf, k_norm, tm=1024):
    b, s, _ = ctx.shape
    kv_block = (POOL_W + ATTN_W) // (2 * KV_W)
    k, v = pl.pallas_call(
        _ctx_kv_kernel,
        grid=(b * s // tm,),
        in_specs=[
            pl.BlockSpec((tm, D_MODEL), lambda i: (i, 0)),
            pl.BlockSpec((N_MOD, MOD_TILE, D_MODEL), lambda i: (0, 1, 0)),
            pl.BlockSpec((1, D_MODEL), lambda i: (0, 0)),
            pl.BlockSpec((D_MODEL, 2 * KV_W), lambda i: (0, kv_block)),
            pl.BlockSpec((1, HEAD_DIM), lambda i: (0, 0)),
        ],
        out_specs=[pl.BlockSpec((tm, KV_W), lambda i: (i, 0))] * 2,
        out_shape=[jax.ShapeDtypeStruct((b * s, KV_W), BF16)] * 2,
        compiler_params=_params("arbitrary"),
        name="ctx_kv",
    )(ctx.reshape(b * s, D_MODEL), mod3, norm_w, w_in_bf, k_norm)
    return k.reshape(b, s, KV_W), v.reshape(b, s, KV_W)


def _in_proj_kernel(x_ref, mod_ref, nw_ref, w_ref, qn_ref, kn_ref, cos_ref, sin_ref,
                    p_ref, q_ref, k_ref, v_ref):
    nw = nw_ref[...]
    qn = qn_ref[...] * Q_SCALE
    kn = kn_ref[...]
    even_lane = (lax.broadcasted_iota(jnp.int32, (ROW_SUB, HEAD_DIM), 1) % 2) == 0
    for r in range(x_ref.shape[1] // ROW_SUB):
        rows = slice(r * ROW_SUB, (r + 1) * ROW_SUB)
        h = _norm_modulate(x_ref[0, rows, :], nw, mod_ref, 0, pl.program_id(0))
        proj = jnp.dot(h.astype(BF16), w_ref[...], preferred_element_type=F32)
        p_ref[0, rows, :] = proj[:, :POOL_W]
        cos = cos_ref[rows, :]
        sin = sin_ref[rows, :]
        for j in range(N_HEADS):
            lo = POOL_W + j * HEAD_DIM
            qh = _rms(proj[:, lo:lo + HEAD_DIM]) * qn
            q_ref[0, rows, j * HEAD_DIM:(j + 1) * HEAD_DIM] = _rope(qh, cos, sin, even_lane).astype(BF16)
        for j in range(N_KV_HEADS):
            lo = POOL_W + ATTN_W + j * HEAD_DIM
            kh = _rms(proj[:, lo:lo + HEAD_DIM]) * kn
            k_ref[0, rows, j * HEAD_DIM:(j + 1) * HEAD_DIM] = _rope(kh, cos, sin, even_lane).astype(BF16)
        v_ref[0, rows, :] = proj[:, POOL_W + ATTN_W + KV_W:].astype(BF16)


def _in_proj(x, mod3, norm_w, w_in_bf, q_norm, k_norm, cos_full, sin_signed, tm=1024):
    b, t, _ = x.shape
    row = lambda i, j: (i, j, 0)
    const2 = lambda i, j: (0, 0)
    return pl.pallas_call(
        _in_proj_kernel,
        grid=(b, t // tm),
        in_specs=[
            pl.BlockSpec((1, tm, D_MODEL), row),
            pl.BlockSpec((N_MOD, MOD_TILE, D_MODEL), lambda i, j: (0, 0, 0)),
            pl.BlockSpec((1, D_MODEL), const2),
            pl.BlockSpec((D_MODEL, PROJ_W), const2, pipeline_mode=pl.Buffered(1)),
            pl.BlockSpec((1, HEAD_DIM), const2),
            pl.BlockSpec((1, HEAD_DIM), const2),
            pl.BlockSpec((tm, HEAD_DIM), lambda i, j: (j, 0)),
            pl.BlockSpec((tm, HEAD_DIM), lambda i, j: (j, 0)),
        ],
        out_specs=[
            pl.BlockSpec((1, tm, POOL_W), row),
            pl.BlockSpec((1, tm, ATTN_W), row),
            pl.BlockSpec((1, tm, KV_W), row),
            pl.BlockSpec((1, tm, KV_W), row),
        ],
        out_shape=[
            jax.ShapeDtypeStruct((b, t, POOL_W), F32),
            jax.ShapeDtypeStruct((b, t, ATTN_W), BF16),
            jax.ShapeDtypeStruct((b, t, KV_W), BF16),
            jax.ShapeDtypeStruct((b, t, KV_W), BF16),
        ],
        compiler_params=_params("arbitrary", "arbitrary"),
        name="in_proj",
    )(x, mod3, norm_w, w_in_bf, q_norm, k_norm, cos_full, sin_signed)


def _pool_inv_counts(t):
    pos = np.arange(t)[None, :]
    half = np.array(POOL_WINDOWS)[:, None] // 2
    count = np.minimum(pos + half, t) - np.maximum(pos - half, 0)
    inv = (np.float32(1.0) / count.astype(np.float32))[:, :, None]
    return jnp.asarray(np.broadcast_to(inv, (N_POOL_GROUPS, t, HEAD_DIM)))


def _pool_mix_rows(p_ref, prev_ref, next_ref, inv_ref, w_ref, s_ref, o_ref, has_prev, has_next):
    t = p_ref.shape[1]
    n = t + 2 * POOL_PAD
    for g, win in enumerate(POOL_WINDOWS):
        half = win // 2
        cols = slice(g * POOL_GROUP_W, (g + 1) * POOL_GROUP_W)
        u = p_ref[0, :, cols]
        top = jnp.where(has_prev, prev_ref[0, :, cols], 0.0)
        bottom = jnp.where(has_next, next_ref[0, :, cols], 0.0)
        f = jnp.concatenate([top, u, bottom], axis=0)
        step = 1
        while step < win:
            f = f + pltpu.roll(f, n - step, axis=0)
            step *= 2
        if half != POOL_PAD:
            f = pltpu.roll(f, n - (POOL_PAD - half), axis=0)
        inv = inv_ref[g]
        pooled = f[:t] * jnp.concatenate([inv] * (POOL_GROUP_W // inv.shape[1]), axis=1) - u
        mixed = jnp.dot(pooled.astype(BF16), w_ref[g].astype(BF16), preferred_element_type=F32)
        o_ref[0, :, cols] = (mixed * s_ref[:, cols]).astype(BF16)


def _attn_kernel(q_ref, kx_ref, kc_ref, vx_ref, vc_ref,
                 p_ref, prev_ref, next_ref, inv_ref, pw_ref, ps_ref, *refs, n_parts):
    n_cast = (len(refs) - 2) // 2
    o_ref, pool_ref = refs[n_cast], refs[n_cast + 1]
    kx = kx_ref[0]
    kc = kc_ref[0]
    vx = jnp.concatenate([vx_ref[0], jnp.ones(vx_ref.shape[1:], BF16)], axis=1)
    vc = jnp.concatenate([vc_ref[0], jnp.ones(vc_ref.shape[1:], BF16)], axis=1)
    nt = (((1,), (1,)), ((), ()))
    for r in range(q_ref.shape[1] // ATTN_SUB):
        rows = slice(r * ATTN_SUB, (r + 1) * ATTN_SUB)
        for g in range(Q_PER_KV):
            cols = slice(g * HEAD_DIM, (g + 1) * HEAD_DIM)
            q = q_ref[0, rows, cols]
            sx = lax.dot_general(q, kx, nt, preferred_element_type=F32)
            sc = lax.dot_general(q, kc, nt, preferred_element_type=F32)
            m = jnp.maximum(jnp.max(sx, axis=-1, keepdims=True), jnp.max(sc, axis=-1, keepdims=True))
            px = jnp.exp2(sx - m)
            pc = jnp.exp2(sc - m)
            o = (jnp.dot(px.astype(BF16), vx, preferred_element_type=F32)
                 + jnp.dot(pc.astype(BF16), vc, preferred_element_type=F32))
            o_ref[0, rows, cols] = (o[:, :HEAD_DIM] * (1.0 / o[:, HEAD_DIM:])).astype(BF16)
    part = pl.program_id(1) * pl.num_programs(2) + pl.program_id(2)
    _pool_mix_rows(p_ref, prev_ref, next_ref, inv_ref, pw_ref, ps_ref, pool_ref,
                   part > 0, part < n_parts - 1)
    for src_ref, dst_ref in zip(refs[:n_cast], refs[n_cast + 2:]):
        dst_ref[...] = src_ref[...].astype(BF16)


def _attention(q, k_x, v_x, k_c, v_c, p, pool_w, pool_scale, cast_f32, tq=1024):
    b, t, _ = q.shape
    s = k_c.shape[1]
    qw = Q_PER_KV * HEAD_DIM
    n_t = t // tq
    n_parts = N_KV_HEADS * n_t
    n_steps = b * n_parts
    pool_rows = t // n_parts
    halo_blocks = pool_rows // POOL_PAD
    assert pool_rows % POOL_PAD == 0 and all(a.shape[0] % (16 * n_steps) == 0 for a in cast_f32)
    part = lambda h, j: h * n_t + j
    kv_map = lambda i, h, j: (i, 0, h)
    pool_map = lambda i, h, j: (i, part(h, j), 0)
    slab = lambda a: pl.BlockSpec((a.shape[0] // n_steps, a.shape[1]),
                                  lambda i, h, j: (i * n_parts + part(h, j), 0))
    out = pl.pallas_call(
        functools.partial(_attn_kernel, n_parts=n_parts),
        grid=(b, N_KV_HEADS, n_t),
        in_specs=[
            pl.BlockSpec((1, tq, qw), lambda i, h, j: (i, j, h)),
            pl.BlockSpec((1, t, HEAD_DIM), kv_map),
            pl.BlockSpec((1, s, HEAD_DIM), kv_map),
            pl.BlockSpec((1, t, HEAD_DIM), kv_map),
            pl.BlockSpec((1, s, HEAD_DIM), kv_map),
            pl.BlockSpec((1, pool_rows, POOL_W), pool_map),
            pl.BlockSpec((1, POOL_PAD, POOL_W),
                         lambda i, h, j: (i, jnp.maximum(part(h, j) * halo_blocks - 1, 0), 0)),
            pl.BlockSpec((1, POOL_PAD, POOL_W),
                         lambda i, h, j: (i, jnp.minimum((part(h, j) + 1) * halo_blocks,
                                                         n_parts * halo_blocks - 1), 0)),
            pl.BlockSpec((N_POOL_GROUPS, pool_rows, HEAD_DIM), lambda i, h, j: (0, part(h, j), 0)),
            pl.BlockSpec((N_POOL_GROUPS, POOL_GROUP_W, POOL_GROUP_W), lambda i, h, j: (0, 0, 0)),
            pl.BlockSpec((1, POOL_W), lambda i, h, j: (0, 0)),
            *[slab(a) for a in cast_f32],
        ],
        out_specs=[pl.BlockSpec((1, tq, qw), lambda i, h, j: (i, j, h)),
                   pl.BlockSpec((1, pool_rows, POOL_W), pool_map),
                   *[slab(a) for a in cast_f32]],
        out_shape=[jax.ShapeDtypeStruct((b, t, ATTN_W), BF16),
                   jax.ShapeDtypeStruct((b, t, POOL_W), BF16),
                   *[jax.ShapeDtypeStruct(a.shape, BF16) for a in cast_f32]],
        compiler_params=_params("arbitrary", "arbitrary", "arbitrary"),
        name="attention",
    )(q, k_x, k_c, v_x, v_c, p, p, p, _pool_inv_counts(t), pool_w, pool_scale, *cast_f32)
    return out[0], out[1], out[2:]


def _out_proj_kernel(x_ref, *refs):
    lhs_refs, (mod_ref, nw_ref), w_refs, (o_ref, h_ref) = refs[:4], refs[4:6], refs[6:-2], refs[-2:]
    batch = pl.program_id(0)
    gate = _mod_row(mod_ref, 2, batch)
    nw = nw_ref[...]
    for r in range(x_ref.shape[1] // ROW_SUB):
        rows = slice(r * ROW_SUB, (r + 1) * ROW_SUB)
        lhs = jnp.concatenate([ref[0, rows, :] for ref in lhs_refs], axis=1)
        mix = jnp.concatenate([jnp.dot(lhs, w_ref[...], preferred_element_type=F32)
                               for w_ref in w_refs], axis=1)
        x1 = x_ref[0, rows, :] + gate * mix
        o_ref[0, rows, :] = x1
        h_ref[0, rows, :] = _norm_modulate(x1, nw, mod_ref, 3, batch).astype(BF16)


def _out_proj(x, pool_mix, attn, mod3, norm_w, w_out_bf, tm=1024):
    b, t, _ = x.shape
    row = lambda i, j: (i, j, 0)
    lhs_window = lambda c: pl.BlockSpec((1, tm, MXU_WINDOW_W), lambda i, j: (i, j, c))
    w_window = lambda c: pl.BlockSpec((MIX_W, MXU_WINDOW_W), lambda i, j: (0, c),
                                      pipeline_mode=pl.Buffered(1))
    n_w = D_MODEL // MXU_WINDOW_W
    return pl.pallas_call(
        _out_proj_kernel,
        grid=(b, t // tm),
        in_specs=[
            pl.BlockSpec((1, tm, D_MODEL), row),
            lhs_window(0), lhs_window(1), lhs_window(0), lhs_window(1),
            pl.BlockSpec((N_MOD, MOD_TILE, D_MODEL), lambda i, j: (0, 0, 0)),
            pl.BlockSpec((1, D_MODEL), lambda i, j: (0, 0)),
            *[w_window(c) for c in range(n_w)],
        ],
        out_specs=[pl.BlockSpec((1, tm, D_MODEL), row), pl.BlockSpec((1, tm, D_MODEL), row)],
        out_shape=[jax.ShapeDtypeStruct(x.shape, F32), jax.ShapeDtypeStruct(x.shape, BF16)],
        compiler_params=_params("arbitrary", "arbitrary"),
        name="out_proj",
    )(x, pool_mix, pool_mix, attn, attn, mod3, norm_w, *([w_out_bf] * n_w))


def _ffn_kernel(h_ref, x_ref, gate_ref, wg_ref, wu_ref, wd_ref, fn_ref, o_ref, act_ref, *, n_f, tf, tn):
    s = pl.program_id(2)

    @pl.when(s < n_f)
    def _():
        h = h_ref[0]
        for c in range(tf // FFN_CHUNK_W):
            cols = slice(c * FFN_CHUNK_W, (c + 1) * FFN_CHUNK_W)
            gate = jnp.dot(h, wg_ref[:, cols], preferred_element_type=F32)
            up = jnp.dot(h, wu_ref[:, cols], preferred_element_type=F32)
            half_gate = 0.5 * gate
            act_ref[s, :, cols] = (half_gate * (1.0 + jnp.tanh(half_gate)) * up).astype(BF16)

    @pl.when(s >= n_f)
    def _():
        y = jnp.dot(act_ref[0], wd_ref[0:tf, :], preferred_element_type=F32)
        for f in range(1, n_f):
            y += jnp.dot(act_ref[f], wd_ref[f * tf:(f + 1) * tf, :], preferred_element_type=F32)
        cols = pl.ds(pl.multiple_of((s - n_f) * tn, tn), tn)
        o_ref[0, :, cols] = x_ref[0] + _mod_row(gate_ref, 0, pl.program_id(0)) * y

    @pl.when(s == pl.num_programs(2) - 1)
    def _():
        o_ref[0] = _rms(o_ref[0]) * fn_ref[...]


def _ffn(h, x1, mod3, wg_bf, wu_bf, wd_bf, final_norm, tm=1024, tf=512, tn=512):
    b, t, _ = h.shape
    d_ff = wg_bf.shape[1]
    n_f = d_ff // tf
    n_n = D_MODEL // tn
    row = lambda i, j, s: (i, j, 0)
    up_blk = lambda i, j, s: (0, jnp.minimum(s, n_f - 1))
    down_blk = lambda i, j, s: (0, jnp.maximum(s - n_f, 0))
    return pl.pallas_call(
        functools.partial(_ffn_kernel, n_f=n_f, tf=tf, tn=tn),
        grid=(b, t // tm, n_f + n_n),
        in_specs=[
            pl.BlockSpec((1, tm, D_MODEL), row),
            pl.BlockSpec((1, tm, tn), lambda i, j, s: (i, j, jnp.maximum(s - n_f, 0))),
            pl.BlockSpec((1, MOD_TILE, tn), lambda i, j, s: (N_MOD - 1, 0, jnp.maximum(s - n_f, 0))),
            pl.BlockSpec((D_MODEL, tf), up_blk),
            pl.BlockSpec((D_MODEL, tf), up_blk),
            pl.BlockSpec((d_ff, tn), down_blk),
            pl.BlockSpec((1, D_MODEL), lambda i, j, s: (0, 0)),
        ],
        out_specs=pl.BlockSpec((1, tm, D_MODEL), row),
        out_shape=jax.ShapeDtypeStruct(x1.shape, F32),
        scratch_shapes=[pltpu.VMEM((n_f, tm, tf), BF16)],
        compiler_params=_params("arbitrary", "arbitrary", "arbitrary"),
        name="ffn",
    )(h, x1, mod3, wg_bf, wu_bf, wd_bf, final_norm)


def _rope_tables(t):
    n_rows = t // GRID_W
    rows = np.repeat(np.arange(n_rows, dtype=np.float32), GRID_W)
    cols = np.tile(np.arange(GRID_W, dtype=np.float32), n_rows)
    freqs = np.float32(ROPE_THETA) ** (-np.arange(0, AXIS_ROT, 2, dtype=np.float32) / np.float32(AXIS_ROT))
    ang = np.concatenate([rows[:, None] * freqs, cols[:, None] * freqs], axis=-1)
    cos_full = np.repeat(np.cos(ang), 2, axis=-1)
    sin = np.sin(ang)
    sin_signed = np.stack([-sin, sin], axis=-1).reshape(t, HEAD_DIM)
    return jnp.asarray(cos_full, F32), jnp.asarray(sin_signed, F32)


def kernel(x, c, ctx, c_ctx, w_ada, b_ada, norm_mix, norm_ffn, w_in, pool_w, pool_scale,
           q_norm, k_norm, w_out, w_gate, w_up, w_down, final_norm):
    depth = w_ada.shape[0]
    assert depth == 1, "context tokens are only updated between layers; one layer is implemented"
    b, t, _ = x.shape
    cos_full, sin_signed = _rope_tables(t)

    mod3 = _adaln_mod(c, c_ctx, w_ada[0], b_ada)

    w_in_bf = w_in[0].astype(BF16)
    k_c, v_c = _ctx_kv(ctx, mod3, norm_mix, w_in_bf, k_norm)
    p, q, k_x, v_x = _in_proj(x, mod3, norm_mix, w_in_bf, q_norm, k_norm, cos_full, sin_signed)
    attn, pooled, (wo_bf, wg_bf, wu_bf, wd_bf) = _attention(
        q, k_x, v_x, k_c, v_c, p, pool_w[0], pool_scale, (w_out[0], w_gate[0], w_up[0], w_down[0]))
    x1, hf = _out_proj(x, pooled, attn, mod3, norm_ffn, wo_bf)
    return _ffn(hf, x1, mod3, wg_bf, wu_bf, wd_bf, final_norm.reshape(1, D_MODEL))
```

```python
import functools
import math

import jax
import jax.numpy as jnp
import numpy as np
from jax import lax
from jax.experimental import pallas as pl
from jax.experimental.pallas import tpu as pltpu

D_MODEL = 2048
GRID_W = 64
HEAD_DIM = 128
N_HEADS = 8
N_KV_HEADS = 2
Q_PER_KV = N_HEADS // N_KV_HEADS
ATTN_W = N_HEADS * HEAD_DIM
KV_W = N_KV_HEADS * HEAD_DIM
POOL_WINDOWS = (2, 4, 8, 16)
N_POOL_GROUPS = len(POOL_WINDOWS)
POOL_W = D_MODEL // 2
POOL_GROUP_W = POOL_W // N_POOL_GROUPS
MIX_W = POOL_W + ATTN_W
PROJ_W = POOL_W + ATTN_W + 2 * KV_W
ROPE_THETA = 10000.0
AXIS_ROT = HEAD_DIM // 2
EPS = 1e-6
N_MOD = 6
MOD_ROWS = 16
MOD_TILE = 8
POOL_PAD = 8
ROW_SUB = 256
MXU_WINDOW_W = 512
ATTN_SUB = 256
FFN_CHUNK_W = 256
Q_SCALE = math.log2(math.e) / math.sqrt(HEAD_DIM)

F32 = jnp.float32
BF16 = jnp.bfloat16

VMEM_LIMIT_BYTES = 63 * 1024 * 1024


def _params(*sem):
    return pltpu.CompilerParams(dimension_semantics=sem, vmem_limit_bytes=VMEM_LIMIT_BYTES)


def _rms(x):
    return x * lax.rsqrt(jnp.mean(x * x, axis=-1, keepdims=True) + EPS)


def _adaln_kernel(c_ref, cctx_ref, w_ref, b_ref, o_ref):
    c = jnp.concatenate([c_ref[...], jnp.broadcast_to(cctx_ref[...], (MOD_ROWS - MOD_TILE, D_MODEL))], axis=0)
    a = c * jax.nn.sigmoid(c)
    o_ref[0] = jnp.dot(a, w_ref[...], preferred_element_type=F32) + b_ref[...]


def _adaln_mod(c, c_ctx, w_ada, b_ada):
    assert c.shape[0] == MOD_TILE
    return pl.pallas_call(
        _adaln_kernel,
        grid=(N_MOD,),
        in_specs=[
            pl.BlockSpec((MOD_TILE, D_MODEL), lambda j: (0, 0)),
            pl.BlockSpec((1, D_MODEL), lambda j: (0, 0)),
            pl.BlockSpec((D_MODEL, D_MODEL), lambda j: (0, j)),
            pl.BlockSpec((1, D_MODEL), lambda j: (0, j)),
        ],
        out_specs=pl.BlockSpec((1, MOD_ROWS, D_MODEL), lambda j: (j, 0, 0)),
        out_shape=jax.ShapeDtypeStruct((N_MOD, MOD_ROWS, D_MODEL), F32),
        compiler_params=_params("arbitrary"),
        name="adaln_mod",
    )(c, c_ctx.reshape(1, D_MODEL), w_ada, b_ada)


def _mod_row(mod_ref, chunk, row):
    return mod_ref[chunk, pl.ds(row, 1), :]


def _norm_modulate(x, nw, mod_ref, shift_idx, row):
    sh = _mod_row(mod_ref, shift_idx, row)
    sc = _mod_row(mod_ref, shift_idx + 1, row)
    return (_rms(x) * nw) * (1.0 + sc) + sh


def _rope(x, cos, sin_signed, even_lane):
    partner = jnp.where(even_lane, pltpu.roll(x, HEAD_DIM - 1, axis=1), pltpu.roll(x, 1, axis=1))
    return x * cos + partner * sin_signed


def _ctx_kv_kernel(x_ref, mod_ref, nw_ref, w_ref, kn_ref, k_ref, v_ref):
    nw = nw_ref[...]
    kn = kn_ref[...]
    for r in range(x_ref.shape[0] // ROW_SUB):
        rows = slice(r * ROW_SUB, (r + 1) * ROW_SUB)
        h = _norm_modulate(x_ref[rows, :], nw, mod_ref, 0, 0)
        kv = jnp.dot(h.astype(BF16), w_ref[...], preferred_element_type=F32)
        for j in range(N_KV_HEADS):
            cols = slice(j * HEAD_DIM, (j + 1) * HEAD_DIM)
            k_ref[rows, cols] = (_rms(kv[:, cols]) * kn).astype(BF16)
        v_ref[rows, :] = kv[:, KV_W:].astype(BF16)


def _ctx_kv(ctx, mod3, norm_w, w_in_bf, k_norm, tm=1024):
    b, s, _ = ctx.shape
    kv_block = (POOL_W + ATTN_W) // (2 * KV_W)
    k, v = pl.pallas_call(
        _ctx_kv_kernel,
        grid=(b * s // tm,),
        in_specs=[
            pl.BlockSpec((tm, D_MODEL), lambda i: (i, 0)),
            pl.BlockSpec((N_MOD, MOD_TILE, D_MODEL), lambda i: (0, 1, 0)),
            pl.BlockSpec((1, D_MODEL), lambda i: (0, 0)),
            pl.BlockSpec((D_MODEL, 2 * KV_W), lambda i: (0, kv_block)),
            pl.BlockSpec((1, HEAD_DIM), lambda i: (0, 0)),
        ],
        out_specs=[pl.BlockSpec((tm, KV_W), lambda i: (i, 0))] * 2,
        out_shape=[jax.ShapeDtypeStruct((b * s, KV_W), BF16)] * 2,
        compiler_params=_params("arbitrary"),
        name="ctx_kv",
    )(ctx.reshape(b * s, D_MODEL), mod3, norm_w, w_in_bf, k_norm)
    return k.reshape(b, s, KV_W), v.reshape(b, s, KV_W)


def _in_proj_kernel(x_ref, mod_ref, nw_ref, w_ref, qn_ref, kn_ref, cos_ref, sin_ref,
                    p_ref, q_ref, k_ref, v_ref):
    nw = nw_ref[...]
    qn = qn_ref[...] * Q_SCALE
    kn = kn_ref[...]
    even_lane = (lax.broadcasted_iota(jnp.int32, (ROW_SUB, HEAD_DIM), 1) % 2) == 0
    for r in range(x_ref.shape[1] // ROW_SUB):
        rows = slice(r * ROW_SUB, (r + 1) * ROW_SUB)
        h = _norm_modulate(x_ref[0, rows, :], nw, mod_ref, 0, pl.program_id(0))
        proj = jnp.dot(h.astype(BF16), w_ref[...], preferred_element_type=F32)
        p_ref[0, rows, :] = proj[:, :POOL_W]
        cos = cos_ref[rows, :]
        sin = sin_ref[rows, :]
        for j in range(N_HEADS):
            lo = POOL_W + j * HEAD_DIM
            qh = _rms(proj[:, lo:lo + HEAD_DIM]) * qn
            q_ref[0, rows, j * HEAD_DIM:(j + 1) * HEAD_DIM] = _rope(qh, cos, sin, even_lane).astype(BF16)
        for j in range(N_KV_HEADS):
            lo = POOL_W + ATTN_W + j * HEAD_DIM
            kh = _rms(proj[:, lo:lo + HEAD_DIM]) * kn
            k_ref[0, rows, j * HEAD_DIM:(j + 1) * HEAD_DIM] = _rope(kh, cos, sin, even_lane).astype(BF16)
        v_ref[0, rows, :] = proj[:, POOL_W + ATTN_W + KV_W:].astype(BF16)


def _in_proj(x, mod3, norm_w, w_in_bf, q_norm, k_norm, cos_full, sin_signed, tm=1024):
    b, t, _ = x.shape
    row = lambda i, j: (i, j, 0)
    const2 = lambda i, j: (0, 0)
    return pl.pallas_call(
        _in_proj_kernel,
        grid=(b, t // tm),
        in_specs=[
            pl.BlockSpec((1, tm, D_MODEL), row),
            pl.BlockSpec((N_MOD, MOD_TILE, D_MODEL), lambda i, j: (0, 0, 0)),
            pl.BlockSpec((1, D_MODEL), const2),
            pl.BlockSpec((D_MODEL, PROJ_W), const2, pipeline_mode=pl.Buffered(1)),
            pl.BlockSpec((1, HEAD_DIM), const2),
            pl.BlockSpec((1, HEAD_DIM), const2),
            pl.BlockSpec((tm, HEAD_DIM), lambda i, j: (j, 0)),
            pl.BlockSpec((tm, HEAD_DIM), lambda i, j: (j, 0)),
        ],
        out_specs=[
            pl.BlockSpec((1, tm, POOL_W), row),
            pl.BlockSpec((1, tm, ATTN_W), row),
            pl.BlockSpec((1, tm, KV_W), row),
            pl.BlockSpec((1, tm, KV_W), row),
        ],
        out_shape=[
            jax.ShapeDtypeStruct((b, t, POOL_W), F32),
            jax.ShapeDtypeStruct((b, t, ATTN_W), BF16),
            jax.ShapeDtypeStruct((b, t, KV_W), BF16),
            jax.ShapeDtypeStruct((b, t, KV_W), BF16),
        ],
        compiler_params=_params("arbitrary", "arbitrary"),
        name="in_proj",
    )(x, mod3, norm_w, w_in_bf, q_norm, k_norm, cos_full, sin_signed)


def _pool_inv_counts(t):
    pos = np.arange(t)[None, :]
    half = np.array(POOL_WINDOWS)[:, None] // 2
    count = np.minimum(pos + half, t) - np.maximum(pos - half, 0)
    inv = (np.float32(1.0) / count.astype(np.float32))[:, :, None]
    return jnp.asarray(np.broadcast_to(inv, (N_POOL_GROUPS, t, HEAD_DIM)))


def _pool_mix_rows(p_ref, prev_ref, next_ref, inv_ref, w_ref, s_ref, o_ref, has_prev, has_next):
    t = p_ref.shape[1]
    n = t + 2 * POOL_PAD
    for g, win in enumerate(POOL_WINDOWS):
        half = win // 2
        cols = slice(g * POOL_GROUP_W, (g + 1) * POOL_GROUP_W)
        u = p_ref[0, :, cols]
        top = jnp.where(has_prev, prev_ref[0, :, cols], 0.0)
        bottom = jnp.where(has_next, next_ref[0, :, cols], 0.0)
        f = jnp.concatenate([top, u, bottom], axis=0)
        step = 1
        while step < win:
            f = f + pltpu.roll(f, n - step, axis=0)
            step *= 2
        if half != POOL_PAD:
            f = pltpu.roll(f, n - (POOL_PAD - half), axis=0)
        inv = inv_ref[g]
        pooled = f[:t] * jnp.concatenate([inv] * (POOL_GROUP_W // inv.shape[1]), axis=1) - u
        mixed = jnp.dot(pooled.astype(BF16), w_ref[g].astype(BF16), preferred_element_type=F32)
        o_ref[0, :, cols] = (mixed * s_ref[:, cols]).astype(BF16)


def _attn_kernel(q_ref, kx_ref, kc_ref, vx_ref, vc_ref,
                 p_ref, prev_ref, next_ref, inv_ref, pw_ref, ps_ref, *refs, n_parts):
    n_cast = (len(refs) - 2) // 2
    o_ref, pool_ref = refs[n_cast], refs[n_cast + 1]
    kx = kx_ref[0]
    kc = kc_ref[0]
    vx = jnp.concatenate([vx_ref[0], jnp.ones(vx_ref.shape[1:], BF16)], axis=1)
    vc = jnp.concatenate([vc_ref[0], jnp.ones(vc_ref.shape[1:], BF16)], axis=1)
    nt = (((1,), (1,)), ((), ()))
    for r in range(q_ref.shape[1] // ATTN_SUB):
        rows = slice(r * ATTN_SUB, (r + 1) * ATTN_SUB)
        for g in range(Q_PER_KV):
            cols = slice(g * HEAD_DIM, (g + 1) * HEAD_DIM)
            q = q_ref[0, rows, cols]
            sx = lax.dot_general(q, kx, nt, preferred_element_type=F32)
            sc = lax.dot_general(q, kc, nt, preferred_element_type=F32)
            m = jnp.maximum(jnp.max(sx, axis=-1, keepdims=True), jnp.max(sc, axis=-1, keepdims=True))
            px = jnp.exp2(sx - m)
            pc = jnp.exp2(sc - m)
            o = (jnp.dot(px.astype(BF16), vx, preferred_element_type=F32)
                 + jnp.dot(pc.astype(BF16), vc, preferred_element_type=F32))
            o_ref[0, rows, cols] = (o[:, :HEAD_DIM] * (1.0 / o[:, HEAD_DIM:])).astype(BF16)
    part = pl.program_id(1) * pl.num_programs(2) + pl.program_id(2)
    _pool_mix_rows(p_ref, prev_ref, next_ref, inv_ref, pw_ref, ps_ref, pool_ref,
                   part > 0, part < n_parts - 1)
    for src_ref, dst_ref in zip(refs[:n_cast], refs[n_cast + 2:]):
        dst_ref[...] = src_ref[...].astype(BF16)


def _attention(q, k_x, v_x, k_c, v_c, p, pool_w, pool_scale, cast_f32, tq=1024):
    b, t, _ = q.shape
    s = k_c.shape[1]
    qw = Q_PER_KV * HEAD_DIM
    n_t = t // tq
    n_parts = N_KV_HEADS * n_t
    n_steps = b * n_parts
    pool_rows = t // n_parts
    halo_blocks = pool_rows // POOL_PAD
    assert pool_rows % POOL_PAD == 0 and all(a.shape[0] % (16 * n_steps) == 0 for a in cast_f32)
    part = lambda h, j: h * n_t + j
    kv_map = lambda i, h, j: (i, 0, h)
    pool_map = lambda i, h, j: (i, part(h, j), 0)
    slab = lambda a: pl.BlockSpec((a.shape[0] // n_steps, a.shape[1]),
                                  lambda i, h, j: (i * n_parts + part(h, j), 0))
    out = pl.pallas_call(
        functools.partial(_attn_kernel, n_parts=n_parts),
        grid=(b, N_KV_HEADS, n_t),
        in_specs=[
            pl.BlockSpec((1, tq, qw), lambda i, h, j: (i, j, h)),
            pl.BlockSpec((1, t, HEAD_DIM), kv_map),
            pl.BlockSpec((1, s, HEAD_DIM), kv_map),
            pl.BlockSpec((1, t, HEAD_DIM), kv_map),
            pl.BlockSpec((1, s, HEAD_DIM), kv_map),
            pl.BlockSpec((1, pool_rows, POOL_W), pool_map),
            pl.BlockSpec((1, POOL_PAD, POOL_W),
                         lambda i, h, j: (i, jnp.maximum(part(h, j) * halo_blocks - 1, 0), 0)),
            pl.BlockSpec((1, POOL_PAD, POOL_W),
                         lambda i, h, j: (i, jnp.minimum((part(h, j) + 1) * halo_blocks,
                                                         n_parts * halo_blocks - 1), 0)),
            pl.BlockSpec((N_POOL_GROUPS, pool_rows, HEAD_DIM), lambda i, h, j: (0, part(h, j), 0)),
            pl.BlockSpec((N_POOL_GROUPS, POOL_GROUP_W, POOL_GROUP_W), lambda i, h, j: (0, 0, 0)),
            pl.BlockSpec((1, POOL_W), lambda i, h, j: (0, 0)),
            *[slab(a) for a in cast_f32],
        ],
        out_specs=[pl.BlockSpec((1, tq, qw), lambda i, h, j: (i, j, h)),
                   pl.BlockSpec((1, pool_rows, POOL_W), pool_map),
                   *[slab(a) for a in cast_f32]],
        out_shape=[jax.ShapeDtypeStruct((b, t, ATTN_W), BF16),
                   jax.ShapeDtypeStruct((b, t, POOL_W), BF16),
                   *[jax.ShapeDtypeStruct(a.shape, BF16) for a in cast_f32]],
        compiler_params=_params("arbitrary", "arbitrary", "arbitrary"),
        name="attention",
    )(q, k_x, k_c, v_x, v_c, p, p, p, _pool_inv_counts(t), pool_w, pool_scale, *cast_f32)
    return out[0], out[1], out[2:]


def _out_proj_kernel(x_ref, *refs):
    lhs_refs, (mod_ref, nw_ref), w_refs, (o_ref, h_ref) = refs[:4], refs[4:6], refs[6:-2], refs[-2:]
    batch = pl.program_id(0)
    gate = _mod_row(mod_ref, 2, batch)
    nw = nw_ref[...]
    for r in range(x_ref.shape[1] // ROW_SUB):
        rows = slice(r * ROW_SUB, (r + 1) * ROW_SUB)
        lhs = jnp.concatenate([ref[0, rows, :] for ref in lhs_refs], axis=1)
        mix = jnp.concatenate([jnp.dot(lhs, w_ref[...], preferred_element_type=F32)
                               for w_ref in w_refs], axis=1)
        x1 = x_ref[0, rows, :] + gate * mix
        o_ref[0, rows, :] = x1
        h_ref[0, rows, :] = _norm_modulate(x1, nw, mod_ref, 3, batch).astype(BF16)


def _out_proj(x, pool_mix, attn, mod3, norm_w, w_out_bf, tm=1024):
    b, t, _ = x.shape
    row = lambda i, j: (i, j, 0)
    lhs_window = lambda c: pl.BlockSpec((1, tm, MXU_WINDOW_W), lambda i, j: (i, j, c))
    w_window = lambda c: pl.BlockSpec((MIX_W, MXU_WINDOW_W), lambda i, j: (0, c),
                                      pipeline_mode=pl.Buffered(1))
    n_w = D_MODEL // MXU_WINDOW_W
    return pl.pallas_call(
        _out_proj_kernel,
        grid=(b, t // tm),
        in_specs=[
            pl.BlockSpec((1, tm, D_MODEL), row),
            lhs_window(0), lhs_window(1), lhs_window(0), lhs_window(1),
            pl.BlockSpec((N_MOD, MOD_TILE, D_MODEL), lambda i, j: (0, 0, 0)),
            pl.BlockSpec((1, D_MODEL), lambda i, j: (0, 0)),
            *[w_window(c) for c in range(n_w)],
        ],
        out_specs=[pl.BlockSpec((1, tm, D_MODEL), row), pl.BlockSpec((1, tm, D_MODEL), row)],
        out_shape=[jax.ShapeDtypeStruct(x.shape, F32), jax.ShapeDtypeStruct(x.shape, BF16)],
        compiler_params=_params("arbitrary", "arbitrary"),
        name="out_proj",
    )(x, pool_mix, pool_mix, attn, attn, mod3, norm_w, *([w_out_bf] * n_w))


def _ffn_kernel(h_ref, x_ref, gate_ref, wg_ref, wu_ref, wd_ref, fn_ref, o_hbm, y_ref, act_ref, out_sem,
                *, n_f, tf, tn):
    i, j, s = pl.program_id(0), pl.program_id(1), pl.program_id(2)
    n_i, n_j, n_s = pl.num_programs(0), pl.num_programs(1), pl.num_programs(2)
    tm = y_ref.shape[0]
    chunks = wg_ref.shape[1] // tf
    n_up = -(-n_f // chunks)

    def out_copy(bi, tj):
        return pltpu.make_async_copy(y_ref, o_hbm.at[bi, pl.ds(tj * tm, tm), :], out_sem)

    def up_step(first_chunk, n_chunks):
        h = h_ref[0]
        for c in range(n_chunks * tf // FFN_CHUNK_W):
            cols = slice(c * FFN_CHUNK_W, (c + 1) * FFN_CHUNK_W)
            gate = jnp.dot(h, wg_ref[:, cols], preferred_element_type=F32)
            up = jnp.dot(h, wu_ref[:, cols], preferred_element_type=F32)
            half_gate = 0.5 * gate
            act = (half_gate * (1.0 + jnp.tanh(half_gate)) * up).astype(BF16)
            k, lo = divmod(c * FFN_CHUNK_W, tf)
            act_ref[first_chunk + k, :, lo:lo + FFN_CHUNK_W] = act

    @pl.when(s < n_up - 1)
    def _():
        up_step(s * chunks, chunks)

    @pl.when(s == n_up - 1)
    def _():
        up_step((n_up - 1) * chunks, n_f - (n_up - 1) * chunks)

    @pl.when((s == n_up) & ((i > 0) | (j > 0)))
    def _():
        out_copy(jnp.where(j == 0, i - 1, i), jnp.where(j == 0, n_j - 1, j - 1)).wait()

    @pl.when(s >= n_up)
    def _():
        y = jnp.dot(act_ref[0], wd_ref[0:tf, :], preferred_element_type=F32)
        for f in range(1, n_f):
            y += jnp.dot(act_ref[f], wd_ref[f * tf:(f + 1) * tf, :], preferred_element_type=F32)
        cols = pl.ds(pl.multiple_of((s - n_up) * tn, tn), tn)
        y_ref[:, cols] = x_ref[0] + _mod_row(gate_ref, 0, i) * y

    @pl.when(s == n_s - 1)
    def _():
        y_ref[...] = _rms(y_ref[...]) * fn_ref[...]
        out_copy(i, j).start()

        @pl.when((i == n_i - 1) & (j == n_j - 1))
        def _():
            out_copy(i, j).wait()


def _ffn(h, x1, mod3, wg_bf, wu_bf, wd_bf, final_norm, tm=1024, tf=512, up_chunks=2, tn=512):
    b, t, _ = h.shape
    d_ff = wg_bf.shape[1]
    n_f = d_ff // tf
    n_up = -(-n_f // up_chunks)
    n_n = D_MODEL // tn
    up_blk = lambda i, j, s: (0, jnp.minimum(s, n_up - 1))
    down_blk = lambda i, j, s: (0, jnp.maximum(s - n_up, 0))
    return pl.pallas_call(
        functools.partial(_ffn_kernel, n_f=n_f, tf=tf, tn=tn),
        grid=(b, t // tm, n_up + n_n),
        in_specs=[
            pl.BlockSpec((1, tm, D_MODEL), lambda i, j, s: (i, j, 0)),
            pl.BlockSpec((1, tm, tn), lambda i, j, s: (i, j, jnp.maximum(s - n_up, 0))),
            pl.BlockSpec((1, MOD_TILE, tn), lambda i, j, s: (N_MOD - 1, 0, jnp.maximum(s - n_up, 0))),
            pl.BlockSpec((D_MODEL, up_chunks * tf), up_blk),
            pl.BlockSpec((D_MODEL, up_chunks * tf), up_blk),
            pl.BlockSpec((d_ff, tn), down_blk),
            pl.BlockSpec((1, D_MODEL), lambda i, j, s: (0, 0)),
        ],
        out_specs=pl.BlockSpec(memory_space=pl.ANY),
        out_shape=jax.ShapeDtypeStruct(x1.shape, F32),
        scratch_shapes=[pltpu.VMEM((tm, D_MODEL), F32), pltpu.VMEM((n_f, tm, tf), BF16),
                        pltpu.SemaphoreType.DMA(())],
        compiler_params=_params("arbitrary", "arbitrary", "arbitrary"),
        name="ffn",
    )(h, x1, mod3, wg_bf, wu_bf, wd_bf, final_norm)


def _rope_tables(t):
    n_rows = t // GRID_W
    rows = np.repeat(np.arange(n_rows, dtype=np.float32), GRID_W)
    cols = np.tile(np.arange(GRID_W, dtype=np.float32), n_rows)
    freqs = np.float32(ROPE_THETA) ** (-np.arange(0, AXIS_ROT, 2, dtype=np.float32) / np.float32(AXIS_ROT))
    ang = np.concatenate([rows[:, None] * freqs, cols[:, None] * freqs], axis=-1)
    cos_full = np.repeat(np.cos(ang), 2, axis=-1)
    sin = np.sin(ang)
    sin_signed = np.stack([-sin, sin], axis=-1).reshape(t, HEAD_DIM)
    return jnp.asarray(cos_full, F32), jnp.asarray(sin_signed, F32)


def kernel(x, c, ctx, c_ctx, w_ada, b_ada, norm_mix, norm_ffn, w_in, pool_w, pool_scale,
           q_norm, k_norm, w_out, w_gate, w_up, w_down, final_norm):
    depth = w_ada.shape[0]
    assert depth == 1, "context tokens are only updated between layers; one layer is implemented"
    b, t, _ = x.shape
    cos_full, sin_signed = _rope_tables(t)

    mod3 = _adaln_mod(c, c_ctx, w_ada[0], b_ada)

    w_in_bf = w_in[0].astype(BF16)
    k_c, v_c = _ctx_kv(ctx, mod3, norm_mix, w_in_bf, k_norm)
    p, q, k_x, v_x = _in_proj(x, mod3, norm_mix, w_in_bf, q_norm, k_norm, cos_full, sin_signed)
    attn, pooled, (wo_bf, wg_bf, wu_bf, wd_bf) = _attention(
        q, k_x, v_x, k_c, v_c, p, pool_w[0], pool_scale, (w_out[0], w_gate[0], w_up[0], w_down[0]))
    x1, hf = _out_proj(x, pooled, attn, mod3, norm_ffn, wo_bf)
    return _ffn(hf, x1, mod3, wg_bf, wu_bf, wd_bf, final_norm.reshape(1, D_MODEL))
```

```python
import functools
import math

import jax
import jax.numpy as jnp
import numpy as np
from jax import lax
from jax.experimental import pallas as pl
from jax.experimental.pallas import tpu as pltpu

D_MODEL = 2048
GRID_W = 64
HEAD_DIM = 128
N_HEADS = 8
N_KV_HEADS = 2
Q_PER_KV = N_HEADS // N_KV_HEADS
ATTN_W = N_HEADS * HEAD_DIM
KV_W = N_KV_HEADS * HEAD_DIM
POOL_WINDOWS = (2, 4, 8, 16)
N_POOL_GROUPS = len(POOL_WINDOWS)
POOL_W = D_MODEL // 2
POOL_GROUP_W = POOL_W // N_POOL_GROUPS
MIX_W = POOL_W + ATTN_W
PROJ_W = POOL_W + ATTN_W + 2 * KV_W
ROPE_THETA = 10000.0
AXIS_ROT = HEAD_DIM // 2
EPS = 1e-6
N_MOD = 6
MOD_ROWS = 16
MOD_TILE = 8
POOL_PAD = 8
ROW_SUB = 256
MXU_WINDOW_W = 512
ATTN_SUB = 256
FFN_CHUNK_W = 256
Q_SCALE = math.log2(math.e) / math.sqrt(HEAD_DIM)

F32 = jnp.float32
BF16 = jnp.bfloat16

VMEM_LIMIT_BYTES = 63 * 1024 * 1024


def _params(*sem):
    return pltpu.CompilerParams(dimension_semantics=sem, vmem_limit_bytes=VMEM_LIMIT_BYTES)


def _rms(x):
    return x * lax.rsqrt(jnp.mean(x * x, axis=-1, keepdims=True) + EPS)


def _adaln_kernel(c_ref, cctx_ref, w_ref, b_ref, o_ref):
    c = jnp.concatenate([c_ref[...], jnp.broadcast_to(cctx_ref[...], (MOD_ROWS - MOD_TILE, D_MODEL))], axis=0)
    a = c * jax.nn.sigmoid(c)
    o_ref[0] = jnp.dot(a, w_ref[...], preferred_element_type=F32) + b_ref[...]


def _adaln_mod(c, c_ctx, w_ada, b_ada):
    assert c.shape[0] == MOD_TILE
    return pl.pallas_call(
        _adaln_kernel,
        grid=(N_MOD,),
        in_specs=[
            pl.BlockSpec((MOD_TILE, D_MODEL), lambda j: (0, 0)),
            pl.BlockSpec((1, D_MODEL), lambda j: (0, 0)),
            pl.BlockSpec((D_MODEL, D_MODEL), lambda j: (0, j)),
            pl.BlockSpec((1, D_MODEL), lambda j: (0, j)),
        ],
        out_specs=pl.BlockSpec((1, MOD_ROWS, D_MODEL), lambda j: (j, 0, 0)),
        out_shape=jax.ShapeDtypeStruct((N_MOD, MOD_ROWS, D_MODEL), F32),
        compiler_params=_params("arbitrary"),
        name="adaln_mod",
    )(c, c_ctx.reshape(1, D_MODEL), w_ada, b_ada)


def _mod_row(mod_ref, chunk, row):
    return mod_ref[chunk, pl.ds(row, 1), :]


def _norm_modulate(x, nw, mod_ref, shift_idx, row):
    sh = _mod_row(mod_ref, shift_idx, row)
    sc = _mod_row(mod_ref, shift_idx + 1, row)
    return (_rms(x) * nw) * (1.0 + sc) + sh


def _rope(x, cos, sin_signed, even_lane):
    partner = jnp.where(even_lane, pltpu.roll(x, HEAD_DIM - 1, axis=1), pltpu.roll(x, 1, axis=1))
    return x * cos + partner * sin_signed


def _ctx_kv_kernel(x_ref, mod_ref, nw_ref, w_ref, kn_ref, k_ref, v_ref):
    nw = nw_ref[...]
    kn = kn_ref[...]
    for r in range(x_ref.shape[0] // ROW_SUB):
        rows = slice(r * ROW_SUB, (r + 1) * ROW_SUB)
        h = _norm_modulate(x_ref[rows, :], nw, mod_ref, 0, 0)
        kv = jnp.dot(h.astype(BF16), w_ref[...].astype(BF16), preferred_element_type=F32)
        for j in range(N_KV_HEADS):
            cols = slice(j * HEAD_DIM, (j + 1) * HEAD_DIM)
            k_ref[rows, cols] = (_rms(kv[:, cols]) * kn).astype(BF16)
        v_ref[rows, :] = kv[:, KV_W:].astype(BF16)


def _ctx_kv(ctx, mod3, norm_w, w_in_bf, k_norm, tm=1024):
    b, s, _ = ctx.shape
    kv_block = (POOL_W + ATTN_W) // (2 * KV_W)
    k, v = pl.pallas_call(
        _ctx_kv_kernel,
        grid=(b * s // tm,),
        in_specs=[
            pl.BlockSpec((tm, D_MODEL), lambda i: (i, 0)),
            pl.BlockSpec((N_MOD, MOD_TILE, D_MODEL), lambda i: (0, 1, 0)),
            pl.BlockSpec((1, D_MODEL), lambda i: (0, 0)),
            pl.BlockSpec((D_MODEL, 2 * KV_W), lambda i: (0, kv_block)),
            pl.BlockSpec((1, HEAD_DIM), lambda i: (0, 0)),
        ],
        out_specs=[pl.BlockSpec((tm, KV_W), lambda i: (i, 0))] * 2,
        out_shape=[jax.ShapeDtypeStruct((b * s, KV_W), BF16)] * 2,
        compiler_params=_params("arbitrary"),
        name="ctx_kv",
    )(ctx.reshape(b * s, D_MODEL), mod3, norm_w, w_in_bf, k_norm)
    return k.reshape(b, s, KV_W), v.reshape(b, s, KV_W)


def _in_proj_kernel(x_ref, mod_ref, nw_ref, w_ref, qn_ref, kn_ref, cos_ref, sin_ref,
                    p_ref, q_ref, k_ref, v_ref):
    nw = nw_ref[...]
    qn = qn_ref[...] * Q_SCALE
    kn = kn_ref[...]
    even_lane = (lax.broadcasted_iota(jnp.int32, (ROW_SUB, HEAD_DIM), 1) % 2) == 0
    for r in range(x_ref.shape[1] // ROW_SUB):
        rows = slice(r * ROW_SUB, (r + 1) * ROW_SUB)
        h = _norm_modulate(x_ref[0, rows, :], nw, mod_ref, 0, pl.program_id(0))
        proj = jnp.dot(h.astype(BF16), w_ref[...].astype(BF16), preferred_element_type=F32)
        p_ref[0, rows, :] = proj[:, :POOL_W]
        cos = cos_ref[rows, :]
        sin = sin_ref[rows, :]
        for j in range(N_HEADS):
            lo = POOL_W + j * HEAD_DIM
            qh = _rms(proj[:, lo:lo + HEAD_DIM]) * qn
            q_ref[0, rows, j * HEAD_DIM:(j + 1) * HEAD_DIM] = _rope(qh, cos, sin, even_lane).astype(BF16)
        for j in range(N_KV_HEADS):
            lo = POOL_W + ATTN_W + j * HEAD_DIM
            kh = _rms(proj[:, lo:lo + HEAD_DIM]) * kn
            k_ref[0, rows, j * HEAD_DIM:(j + 1) * HEAD_DIM] = _rope(kh, cos, sin, even_lane).astype(BF16)
        v_ref[0, rows, :] = proj[:, POOL_W + ATTN_W + KV_W:].astype(BF16)


def _in_proj(x, mod3, norm_w, w_in_bf, q_norm, k_norm, cos_full, sin_signed, tm=1024):
    b, t, _ = x.shape
    row = lambda i, j: (i, j, 0)
    const2 = lambda i, j: (0, 0)
    return pl.pallas_call(
        _in_proj_kernel,
        grid=(b, t // tm),
        in_specs=[
            pl.BlockSpec((1, tm, D_MODEL), row),
            pl.BlockSpec((N_MOD, MOD_TILE, D_MODEL), lambda i, j: (0, 0, 0)),
            pl.BlockSpec((1, D_MODEL), const2),
            pl.BlockSpec((D_MODEL, PROJ_W), const2, pipeline_mode=pl.Buffered(1)),
            pl.BlockSpec((1, HEAD_DIM), const2),
            pl.BlockSpec((1, HEAD_DIM), const2),
            pl.BlockSpec((tm, HEAD_DIM), lambda i, j: (j, 0)),
            pl.BlockSpec((tm, HEAD_DIM), lambda i, j: (j, 0)),
        ],
        out_specs=[
            pl.BlockSpec((1, tm, POOL_W), row),
            pl.BlockSpec((1, tm, ATTN_W), row),
            pl.BlockSpec((1, tm, KV_W), row),
            pl.BlockSpec((1, tm, KV_W), row),
        ],
        out_shape=[
            jax.ShapeDtypeStruct((b, t, POOL_W), F32),
            jax.ShapeDtypeStruct((b, t, ATTN_W), BF16),
            jax.ShapeDtypeStruct((b, t, KV_W), BF16),
            jax.ShapeDtypeStruct((b, t, KV_W), BF16),
        ],
        compiler_params=_params("arbitrary", "arbitrary"),
        name="in_proj",
    )(x, mod3, norm_w, w_in_bf, q_norm, k_norm, cos_full, sin_signed)


def _pool_inv_counts(t):
    pos = np.arange(t)[None, :]
    half = np.array(POOL_WINDOWS)[:, None] // 2
    count = np.minimum(pos + half, t) - np.maximum(pos - half, 0)
    inv = (np.float32(1.0) / count.astype(np.float32))[:, :, None]
    return jnp.asarray(np.broadcast_to(inv, (N_POOL_GROUPS, t, HEAD_DIM)))


def _pool_mix_rows(p_ref, prev_ref, next_ref, inv_ref, w_ref, s_ref, o_ref, has_prev, has_next):
    t = p_ref.shape[1]
    n = t + 2 * POOL_PAD
    for g, win in enumerate(POOL_WINDOWS):
        half = win // 2
        cols = slice(g * POOL_GROUP_W, (g + 1) * POOL_GROUP_W)
        u = p_ref[0, :, cols]
        top = jnp.where(has_prev, prev_ref[0, :, cols], 0.0)
        bottom = jnp.where(has_next, next_ref[0, :, cols], 0.0)
        f = jnp.concatenate([top, u, bottom], axis=0)
        step = 1
        while step < win:
            f = f + pltpu.roll(f, n - step, axis=0)
            step *= 2
        if half != POOL_PAD:
            f = pltpu.roll(f, n - (POOL_PAD - half), axis=0)
        inv = inv_ref[g]
        pooled = f[:t] * jnp.concatenate([inv] * (POOL_GROUP_W // inv.shape[1]), axis=1) - u
        mixed = jnp.dot(pooled.astype(BF16), w_ref[g].astype(BF16), preferred_element_type=F32)
        o_ref[0, :, cols] = (mixed * s_ref[:, cols]).astype(BF16)


def _attn_kernel(q_ref, kx_ref, kc_ref, vx_ref, vc_ref,
                 p_ref, prev_ref, next_ref, inv_ref, pw_ref, ps_ref, *refs, n_parts):
    n_cast = (len(refs) - 2) // 2
    o_ref, pool_ref = refs[n_cast], refs[n_cast + 1]
    kx = kx_ref[0]
    kc = kc_ref[0]
    vx = jnp.concatenate([vx_ref[0], jnp.ones(vx_ref.shape[1:], BF16)], axis=1)
    vc = jnp.concatenate([vc_ref[0], jnp.ones(vc_ref.shape[1:], BF16)], axis=1)
    nt = (((1,), (1,)), ((), ()))
    for r in range(q_ref.shape[1] // ATTN_SUB):
        rows = slice(r * ATTN_SUB, (r + 1) * ATTN_SUB)
        for g in range(Q_PER_KV):
            cols = slice(g * HEAD_DIM, (g + 1) * HEAD_DIM)
            q = q_ref[0, rows, cols]
            sx = lax.dot_general(q, kx, nt, preferred_element_type=F32)
            sc = lax.dot_general(q, kc, nt, preferred_element_type=F32)
            m = jnp.maximum(jnp.max(sx, axis=-1, keepdims=True), jnp.max(sc, axis=-1, keepdims=True))
            px = jnp.exp2(sx - m)
            pc = jnp.exp2(sc - m)
            o = (jnp.dot(px.astype(BF16), vx, preferred_element_type=F32)
                 + jnp.dot(pc.astype(BF16), vc, preferred_element_type=F32))
            o_ref[0, rows, cols] = (o[:, :HEAD_DIM] * (1.0 / o[:, HEAD_DIM:])).astype(BF16)
    part = pl.program_id(1) * pl.num_programs(2) + pl.program_id(2)
    _pool_mix_rows(p_ref, prev_ref, next_ref, inv_ref, pw_ref, ps_ref, pool_ref,
                   part > 0, part < n_parts - 1)
    for src_ref, dst_ref in zip(refs[:n_cast], refs[n_cast + 2:]):
        dst_ref[...] = src_ref[...].astype(BF16)


def _attention(q, k_x, v_x, k_c, v_c, p, pool_w, pool_scale, cast_f32, tq=1024):
    b, t, _ = q.shape
    s = k_c.shape[1]
    qw = Q_PER_KV * HEAD_DIM
    n_t = t // tq
    n_parts = N_KV_HEADS * n_t
    n_steps = b * n_parts
    pool_rows = t // n_parts
    halo_blocks = pool_rows // POOL_PAD
    assert pool_rows % POOL_PAD == 0 and all(a.shape[0] % (16 * n_steps) == 0 for a in cast_f32)
    part = lambda h, j: h * n_t + j
    kv_map = lambda i, h, j: (i, 0, h)
    pool_map = lambda i, h, j: (i, part(h, j), 0)
    slab = lambda a: pl.BlockSpec((a.shape[0] // n_steps, a.shape[1]),
                                  lambda i, h, j: (i * n_parts + part(h, j), 0))
    out = pl.pallas_call(
        functools.partial(_attn_kernel, n_parts=n_parts),
        grid=(b, N_KV_HEADS, n_t),
        in_specs=[
            pl.BlockSpec((1, tq, qw), lambda i, h, j: (i, j, h)),
            pl.BlockSpec((1, t, HEAD_DIM), kv_map),
            pl.BlockSpec((1, s, HEAD_DIM), kv_map),
            pl.BlockSpec((1, t, HEAD_DIM), kv_map),
            pl.BlockSpec((1, s, HEAD_DIM), kv_map),
            pl.BlockSpec((1, pool_rows, POOL_W), pool_map),
            pl.BlockSpec((1, POOL_PAD, POOL_W),
                         lambda i, h, j: (i, jnp.maximum(part(h, j) * halo_blocks - 1, 0), 0)),
            pl.BlockSpec((1, POOL_PAD, POOL_W),
                         lambda i, h, j: (i, jnp.minimum((part(h, j) + 1) * halo_blocks,
                                                         n_parts * halo_blocks - 1), 0)),
            pl.BlockSpec((N_POOL_GROUPS, pool_rows, HEAD_DIM), lambda i, h, j: (0, part(h, j), 0)),
            pl.BlockSpec((N_POOL_GROUPS, POOL_GROUP_W, POOL_GROUP_W), lambda i, h, j: (0, 0, 0)),
            pl.BlockSpec((1, POOL_W), lambda i, h, j: (0, 0)),
            *[slab(a) for a in cast_f32],
        ],
        out_specs=[pl.BlockSpec((1, tq, qw), lambda i, h, j: (i, j, h)),
                   pl.BlockSpec((1, pool_rows, POOL_W), pool_map),
                   *[slab(a) for a in cast_f32]],
        out_shape=[jax.ShapeDtypeStruct((b, t, ATTN_W), BF16),
                   jax.ShapeDtypeStruct((b, t, POOL_W), BF16),
                   *[jax.ShapeDtypeStruct(a.shape, BF16) for a in cast_f32]],
        compiler_params=_params("arbitrary", "arbitrary", "arbitrary"),
        name="attention",
    )(q, k_x, k_c, v_x, v_c, p, p, p, _pool_inv_counts(t), pool_w, pool_scale, *cast_f32)
    return out[0], out[1], out[2:]


def _out_proj_kernel(x_ref, *refs):
    lhs_refs, (mod_ref, nw_ref), w_refs, (o_ref, h_ref) = refs[:4], refs[4:6], refs[6:-2], refs[-2:]
    batch = pl.program_id(0)
    gate = _mod_row(mod_ref, 2, batch)
    nw = nw_ref[...]
    for r in range(x_ref.shape[1] // ROW_SUB):
        rows = slice(r * ROW_SUB, (r + 1) * ROW_SUB)
        lhs = jnp.concatenate([ref[0, rows, :] for ref in lhs_refs], axis=1)
        mix = jnp.concatenate([jnp.dot(lhs, w_ref[...], preferred_element_type=F32)
                               for w_ref in w_refs], axis=1)
        x1 = x_ref[0, rows, :] + gate * mix
        o_ref[0, rows, :] = x1
        h_ref[0, rows, :] = _norm_modulate(x1, nw, mod_ref, 3, batch).astype(BF16)


def _out_proj(x, pool_mix, attn, mod3, norm_w, w_out_bf, tm=1024):
    b, t, _ = x.shape
    row = lambda i, j: (i, j, 0)
    lhs_window = lambda c: pl.BlockSpec((1, tm, MXU_WINDOW_W), lambda i, j: (i, j, c))
    w_window = lambda c: pl.BlockSpec((MIX_W, MXU_WINDOW_W), lambda i, j: (0, c),
                                      pipeline_mode=pl.Buffered(1))
    n_w = D_MODEL // MXU_WINDOW_W
    return pl.pallas_call(
        _out_proj_kernel,
        grid=(b, t // tm),
        in_specs=[
            pl.BlockSpec((1, tm, D_MODEL), row),
            lhs_window(0), lhs_window(1), lhs_window(0), lhs_window(1),
            pl.BlockSpec((N_MOD, MOD_TILE, D_MODEL), lambda i, j: (0, 0, 0)),
            pl.BlockSpec((1, D_MODEL), lambda i, j: (0, 0)),
            *[w_window(c) for c in range(n_w)],
        ],
        out_specs=[pl.BlockSpec((1, tm, D_MODEL), row), pl.BlockSpec((1, tm, D_MODEL), row)],
        out_shape=[jax.ShapeDtypeStruct(x.shape, F32), jax.ShapeDtypeStruct(x.shape, BF16)],
        compiler_params=_params("arbitrary", "arbitrary"),
        name="out_proj",
    )(x, pool_mix, pool_mix, attn, attn, mod3, norm_w, *([w_out_bf] * n_w))


def _ffn_kernel(h_ref, x_ref, gate_ref, wg_ref, wu_ref, wd_ref, fn_ref, o_hbm, y_ref, act_ref, out_sem,
                *, n_f, tf, tn):
    i, j, s = pl.program_id(0), pl.program_id(1), pl.program_id(2)
    n_i, n_j, n_s = pl.num_programs(0), pl.num_programs(1), pl.num_programs(2)
    tm = y_ref.shape[0]
    chunks = wg_ref.shape[1] // tf
    n_up = -(-n_f // chunks)

    def out_copy(bi, tj):
        return pltpu.make_async_copy(y_ref, o_hbm.at[bi, pl.ds(tj * tm, tm), :], out_sem)

    def up_step(first_chunk, n_chunks):
        h = h_ref[0]
        for c in range(n_chunks * tf // FFN_CHUNK_W):
            cols = slice(c * FFN_CHUNK_W, (c + 1) * FFN_CHUNK_W)
            gate = jnp.dot(h, wg_ref[:, cols], preferred_element_type=F32)
            up = jnp.dot(h, wu_ref[:, cols], preferred_element_type=F32)
            half_gate = 0.5 * gate
            act = (half_gate * (1.0 + jnp.tanh(half_gate)) * up).astype(BF16)
            k, lo = divmod(c * FFN_CHUNK_W, tf)
            act_ref[first_chunk + k, :, lo:lo + FFN_CHUNK_W] = act

    @pl.when(s < n_up - 1)
    def _():
        up_step(s * chunks, chunks)

    @pl.when(s == n_up - 1)
    def _():
        up_step((n_up - 1) * chunks, n_f - (n_up - 1) * chunks)

    @pl.when((s == n_up) & ((i > 0) | (j > 0)))
    def _():
        out_copy(jnp.where(j == 0, i - 1, i), jnp.where(j == 0, n_j - 1, j - 1)).wait()

    @pl.when(s >= n_up)
    def _():
        y = jnp.dot(act_ref[0], wd_ref[0:tf, :], preferred_element_type=F32)
        for f in range(1, n_f):
            y += jnp.dot(act_ref[f], wd_ref[f * tf:(f + 1) * tf, :], preferred_element_type=F32)
        cols = pl.ds(pl.multiple_of((s - n_up) * tn, tn), tn)
        y_ref[:, cols] = x_ref[0] + _mod_row(gate_ref, 0, i) * y

    @pl.when(s == n_s - 1)
    def _():
        y_ref[...] = _rms(y_ref[...]) * fn_ref[...]
        out_copy(i, j).start()

        @pl.when((i == n_i - 1) & (j == n_j - 1))
        def _():
            out_copy(i, j).wait()


def _ffn(h, x1, mod3, wg_bf, wu_bf, wd_bf, final_norm, tm=1024, tf=512, up_chunks=2, tn=512):
    b, t, _ = h.shape
    d_ff = wg_bf.shape[1]
    n_f = d_ff // tf
    n_up = -(-n_f // up_chunks)
    n_n = D_MODEL // tn
    up_blk = lambda i, j, s: (0, jnp.minimum(s, n_up - 1))
    down_blk = lambda i, j, s: (0, jnp.maximum(s - n_up, 0))
    return pl.pallas_call(
        functools.partial(_ffn_kernel, n_f=n_f, tf=tf, tn=tn),
        grid=(b, t // tm, n_up + n_n),
        in_specs=[
            pl.BlockSpec((1, tm, D_MODEL), lambda i, j, s: (i, j, 0)),
            pl.BlockSpec((1, tm, tn), lambda i, j, s: (i, j, jnp.maximum(s - n_up, 0))),
            pl.BlockSpec((1, MOD_TILE, tn), lambda i, j, s: (N_MOD - 1, 0, jnp.maximum(s - n_up, 0))),
            pl.BlockSpec((D_MODEL, up_chunks * tf), up_blk),
            pl.BlockSpec((D_MODEL, up_chunks * tf), up_blk),
            pl.BlockSpec((d_ff, tn), down_blk),
            pl.BlockSpec((1, D_MODEL), lambda i, j, s: (0, 0)),
        ],
        out_specs=pl.BlockSpec(memory_space=pl.ANY),
        out_shape=jax.ShapeDtypeStruct(x1.shape, F32),
        scratch_shapes=[pltpu.VMEM((tm, D_MODEL), F32), pltpu.VMEM((n_f, tm, tf), BF16),
                        pltpu.SemaphoreType.DMA(())],
        compiler_params=_params("arbitrary", "arbitrary", "arbitrary"),
        name="ffn",
    )(h, x1, mod3, wg_bf, wu_bf, wd_bf, final_norm)


def _rope_tables(t):
    n_rows = t // GRID_W
    rows = np.repeat(np.arange(n_rows, dtype=np.float32), GRID_W)
    cols = np.tile(np.arange(GRID_W, dtype=np.float32), n_rows)
    freqs = np.float32(ROPE_THETA) ** (-np.arange(0, AXIS_ROT, 2, dtype=np.float32) / np.float32(AXIS_ROT))
    ang = np.concatenate([rows[:, None] * freqs, cols[:, None] * freqs], axis=-1)
    cos_full = np.repeat(np.cos(ang), 2, axis=-1)
    sin = np.sin(ang)
    sin_signed = np.stack([-sin, sin], axis=-1).reshape(t, HEAD_DIM)
    return jnp.asarray(cos_full, F32), jnp.asarray(sin_signed, F32)


def kernel(x, c, ctx, c_ctx, w_ada, b_ada, norm_mix, norm_ffn, w_in, pool_w, pool_scale,
           q_norm, k_norm, w_out, w_gate, w_up, w_down, final_norm):
    depth = w_ada.shape[0]
    assert depth == 1, "context tokens are only updated between layers; one layer is implemented"
    b, t, _ = x.shape
    cos_full, sin_signed = _rope_tables(t)

    mod3 = _adaln_mod(c, c_ctx, w_ada[0], b_ada)

    w_in_bf = w_in[0]
    k_c, v_c = _ctx_kv(ctx, mod3, norm_mix, w_in_bf, k_norm)
    p, q, k_x, v_x = _in_proj(x, mod3, norm_mix, w_in_bf, q_norm, k_norm, cos_full, sin_signed)
    attn, pooled, (wo_bf, wg_bf, wu_bf, wd_bf) = _attention(
        q, k_x, v_x, k_c, v_c, p, pool_w[0], pool_scale, (w_out[0], w_gate[0], w_up[0], w_down[0]))
    x1, hf = _out_proj(x, pooled, attn, mod3, norm_ffn, wo_bf)
    return _ffn(hf, x1, mod3, wg_bf, wu_bf, wd_bf, final_norm.reshape(1, D_MODEL))
```

```python
import functools
import math

import jax
import jax.numpy as jnp
import numpy as np
from jax import lax
from jax.experimental import pallas as pl
from jax.experimental.pallas import tpu as pltpu

D_MODEL = 2048
GRID_W = 64
HEAD_DIM = 128
N_HEADS = 8
N_KV_HEADS = 2
Q_PER_KV = N_HEADS // N_KV_HEADS
ATTN_W = N_HEADS * HEAD_DIM
KV_W = N_KV_HEADS * HEAD_DIM
POOL_WINDOWS = (2, 4, 8, 16)
N_POOL_GROUPS = len(POOL_WINDOWS)
POOL_W = D_MODEL // 2
POOL_GROUP_W = POOL_W // N_POOL_GROUPS
MIX_W = POOL_W + ATTN_W
PROJ_W = POOL_W + ATTN_W + 2 * KV_W
ROPE_THETA = 10000.0
AXIS_ROT = HEAD_DIM // 2
EPS = 1e-6
N_MOD = 6
MOD_ROWS = 16
MOD_TILE = 8
POOL_PAD = 8
ROW_SUB = 256
OUT_SUB = 512
MXU_WINDOW_W = 512
ATTN_SUB = 256
FFN_CHUNK_W = 256
Q_SCALE = math.log2(math.e) / math.sqrt(HEAD_DIM)

F32 = jnp.float32
BF16 = jnp.bfloat16

VMEM_LIMIT_BYTES = 63 * 1024 * 1024


def _params(*sem):
    return pltpu.CompilerParams(dimension_semantics=sem, vmem_limit_bytes=VMEM_LIMIT_BYTES)


def _rms(x):
    return x * lax.rsqrt(jnp.mean(x * x, axis=-1, keepdims=True) + EPS)


def _adaln_kernel(c_ref, cctx_ref, w_ref, b_ref, o_ref):
    c = jnp.concatenate([c_ref[...], jnp.broadcast_to(cctx_ref[...], (MOD_ROWS - MOD_TILE, D_MODEL))], axis=0)
    a = c * jax.nn.sigmoid(c)
    o_ref[0] = jnp.dot(a, w_ref[...], preferred_element_type=F32) + b_ref[...]


def _adaln_mod(c, c_ctx, w_ada, b_ada, tn=1024):
    assert c.shape[0] == MOD_TILE
    per_chunk = D_MODEL // tn
    return pl.pallas_call(
        _adaln_kernel,
        grid=(N_MOD * per_chunk,),
        in_specs=[
            pl.BlockSpec((MOD_TILE, D_MODEL), lambda j: (0, 0)),
            pl.BlockSpec((1, D_MODEL), lambda j: (0, 0)),
            pl.BlockSpec((D_MODEL, tn), lambda j: (0, j)),
            pl.BlockSpec((1, tn), lambda j: (0, j)),
        ],
        out_specs=pl.BlockSpec((1, MOD_ROWS, tn), lambda j: (j // per_chunk, 0, j % per_chunk)),
        out_shape=jax.ShapeDtypeStruct((N_MOD, MOD_ROWS, D_MODEL), F32),
        compiler_params=_params("arbitrary"),
        name="adaln_mod",
    )(c, c_ctx.reshape(1, D_MODEL), w_ada, b_ada)


def _mod_row(mod_ref, chunk, row):
    return mod_ref[chunk, pl.ds(row, 1), :]


def _norm_modulate(x, nw, mod_ref, shift_idx, row):
    sh = _mod_row(mod_ref, shift_idx, row)
    sc = _mod_row(mod_ref, shift_idx + 1, row)
    return (_rms(x) * nw) * (1.0 + sc) + sh


def _rope(x, cos, sin_signed, even_lane):
    partner = jnp.where(even_lane, pltpu.roll(x, HEAD_DIM - 1, axis=1), pltpu.roll(x, 1, axis=1))
    return x * cos + partner * sin_signed


def _ctx_kv_kernel(x_ref, mod_ref, nw_ref, w_ref, kn_ref, k_ref, v_ref):
    nw = nw_ref[...]
    kn = kn_ref[...]
    for r in range(x_ref.shape[0] // ROW_SUB):
        rows = slice(r * ROW_SUB, (r + 1) * ROW_SUB)
        h = _norm_modulate(x_ref[rows, :], nw, mod_ref, 0, 0)
        kv = jnp.dot(h.astype(BF16), w_ref[...].astype(BF16), preferred_element_type=F32)
        for j in range(N_KV_HEADS):
            cols = slice(j * HEAD_DIM, (j + 1) * HEAD_DIM)
            k_ref[rows, cols] = (_rms(kv[:, cols]) * kn).astype(BF16)
        v_ref[rows, :] = kv[:, KV_W:].astype(BF16)


def _ctx_kv(ctx, mod3, norm_w, w_in_bf, k_norm, tm=1024):
    b, s, _ = ctx.shape
    kv_block = (POOL_W + ATTN_W) // (2 * KV_W)
    k, v = pl.pallas_call(
        _ctx_kv_kernel,
        grid=(b * s // tm,),
        in_specs=[
            pl.BlockSpec((tm, D_MODEL), lambda i: (i, 0)),
            pl.BlockSpec((N_MOD, MOD_TILE, D_MODEL), lambda i: (0, 1, 0)),
            pl.BlockSpec((1, D_MODEL), lambda i: (0, 0)),
            pl.BlockSpec((D_MODEL, 2 * KV_W), lambda i: (0, kv_block)),
            pl.BlockSpec((1, HEAD_DIM), lambda i: (0, 0)),
        ],
        out_specs=[pl.BlockSpec((tm, KV_W), lambda i: (i, 0))] * 2,
        out_shape=[jax.ShapeDtypeStruct((b * s, KV_W), BF16)] * 2,
        compiler_params=_params("arbitrary"),
        name="ctx_kv",
    )(ctx.reshape(b * s, D_MODEL), mod3, norm_w, w_in_bf, k_norm)
    return k.reshape(b, s, KV_W), v.reshape(b, s, KV_W)


def _in_proj_kernel(x_ref, mod_ref, nw_ref, w_ref, qn_ref, kn_ref, cos_ref, sin_ref,
                    p_ref, q_ref, k_ref, v_ref):
    nw = nw_ref[...]
    qn = qn_ref[...] * Q_SCALE
    kn = kn_ref[...]
    even_lane = (lax.broadcasted_iota(jnp.int32, (ROW_SUB, HEAD_DIM), 1) % 2) == 0
    for r in range(x_ref.shape[1] // ROW_SUB):
        rows = slice(r * ROW_SUB, (r + 1) * ROW_SUB)
        h = _norm_modulate(x_ref[0, rows, :], nw, mod_ref, 0, pl.program_id(0))
        proj = jnp.dot(h.astype(BF16), w_ref[...].astype(BF16), preferred_element_type=F32)
        p_ref[0, rows, :] = proj[:, :POOL_W]
        cos = cos_ref[rows, :]
        sin = sin_ref[rows, :]
        for j in range(N_HEADS):
            lo = POOL_W + j * HEAD_DIM
            qh = _rms(proj[:, lo:lo + HEAD_DIM]) * qn
            q_ref[0, rows, j * HEAD_DIM:(j + 1) * HEAD_DIM] = _rope(qh, cos, sin, even_lane).astype(BF16)
        for j in range(N_KV_HEADS):
            lo = POOL_W + ATTN_W + j * HEAD_DIM
            kh = _rms(proj[:, lo:lo + HEAD_DIM]) * kn
            k_ref[0, rows, j * HEAD_DIM:(j + 1) * HEAD_DIM] = _rope(kh, cos, sin, even_lane).astype(BF16)
        v_ref[0, rows, :] = proj[:, POOL_W + ATTN_W + KV_W:].astype(BF16)


def _in_proj(x, mod3, norm_w, w_in_bf, q_norm, k_norm, cos_full, sin_signed, tm=1024):
    b, t, _ = x.shape
    row = lambda i, j: (i, j, 0)
    const2 = lambda i, j: (0, 0)
    return pl.pallas_call(
        _in_proj_kernel,
        grid=(b, t // tm),
        in_specs=[
            pl.BlockSpec((1, tm, D_MODEL), row),
            pl.BlockSpec((N_MOD, MOD_TILE, D_MODEL), lambda i, j: (0, 0, 0)),
            pl.BlockSpec((1, D_MODEL), const2),
            pl.BlockSpec((D_MODEL, PROJ_W), const2, pipeline_mode=pl.Buffered(1)),
            pl.BlockSpec((1, HEAD_DIM), const2),
            pl.BlockSpec((1, HEAD_DIM), const2),
            pl.BlockSpec((tm, HEAD_DIM), lambda i, j: (j, 0)),
            pl.BlockSpec((tm, HEAD_DIM), lambda i, j: (j, 0)),
        ],
        out_specs=[
            pl.BlockSpec((1, tm, POOL_W), row),
            pl.BlockSpec((1, tm, ATTN_W), row),
            pl.BlockSpec((1, tm, KV_W), row),
            pl.BlockSpec((1, tm, KV_W), row),
        ],
        out_shape=[
            jax.ShapeDtypeStruct((b, t, POOL_W), F32),
            jax.ShapeDtypeStruct((b, t, ATTN_W), BF16),
            jax.ShapeDtypeStruct((b, t, KV_W), BF16),
            jax.ShapeDtypeStruct((b, t, KV_W), BF16),
        ],
        compiler_params=_params("arbitrary", "arbitrary"),
        name="in_proj",
    )(x, mod3, norm_w, w_in_bf, q_norm, k_norm, cos_full, sin_signed)


def _pool_inv_counts(t):
    pos = np.arange(t)[None, :]
    half = np.array(POOL_WINDOWS)[:, None] // 2
    count = np.minimum(pos + half, t) - np.maximum(pos - half, 0)
    inv = (np.float32(1.0) / count.astype(np.float32))[:, :, None]
    return jnp.asarray(np.broadcast_to(inv, (N_POOL_GROUPS, t, HEAD_DIM)))


def _pool_mix_rows(p_ref, prev_ref, next_ref, inv_ref, w_ref, s_ref, o_ref, has_prev, has_next):
    t = p_ref.shape[1]
    n = t + 2 * POOL_PAD
    for g, win in enumerate(POOL_WINDOWS):
        half = win // 2
        cols = slice(g * POOL_GROUP_W, (g + 1) * POOL_GROUP_W)
        u = p_ref[0, :, cols]
        top = jnp.where(has_prev, prev_ref[0, :, cols], 0.0)
        bottom = jnp.where(has_next, next_ref[0, :, cols], 0.0)
        f = jnp.concatenate([top, u, bottom], axis=0)
        step = 1
        while step < win:
            f = f + pltpu.roll(f, n - step, axis=0)
            step *= 2
        if half != POOL_PAD:
            f = pltpu.roll(f, n - (POOL_PAD - half), axis=0)
        inv = inv_ref[g]
        pooled = f[:t] * jnp.concatenate([inv] * (POOL_GROUP_W // inv.shape[1]), axis=1) - u
        mixed = jnp.dot(pooled.astype(BF16), w_ref[g].astype(BF16), preferred_element_type=F32)
        o_ref[0, :, cols] = (mixed * s_ref[:, cols]).astype(BF16)


def _attn_kernel(q_ref, kx_ref, kc_ref, vx_ref, vc_ref,
                 p_ref, prev_ref, next_ref, inv_ref, pw_ref, ps_ref, *refs, n_parts):
    n_cast = (len(refs) - 2) // 2
    o_ref, pool_ref = refs[n_cast], refs[n_cast + 1]
    kx = kx_ref[0]
    kc = kc_ref[0]
    vx = jnp.concatenate([vx_ref[0], jnp.ones(vx_ref.shape[1:], BF16)], axis=1)
    vc = jnp.concatenate([vc_ref[0], jnp.ones(vc_ref.shape[1:], BF16)], axis=1)
    nt = (((1,), (1,)), ((), ()))
    for r in range(q_ref.shape[1] // ATTN_SUB):
        rows = slice(r * ATTN_SUB, (r + 1) * ATTN_SUB)
        for g in range(Q_PER_KV):
            cols = slice(g * HEAD_DIM, (g + 1) * HEAD_DIM)
            q = q_ref[0, rows, cols]
            sx = lax.dot_general(q, kx, nt, preferred_element_type=F32)
            sc = lax.dot_general(q, kc, nt, preferred_element_type=F32)
            m = jnp.maximum(jnp.max(sx, axis=-1, keepdims=True), jnp.max(sc, axis=-1, keepdims=True))
            px = jnp.exp2(sx - m)
            pc = jnp.exp2(sc - m)
            o = (jnp.dot(px.astype(BF16), vx, preferred_element_type=F32)
                 + jnp.dot(pc.astype(BF16), vc, preferred_element_type=F32))
            o_ref[0, rows, cols] = (o[:, :HEAD_DIM] * (1.0 / o[:, HEAD_DIM:])).astype(BF16)
    part = pl.program_id(1) * pl.num_programs(2) + pl.program_id(2)
    _pool_mix_rows(p_ref, prev_ref, next_ref, inv_ref, pw_ref, ps_ref, pool_ref,
                   part > 0, part < n_parts - 1)
    for src_ref, dst_ref in zip(refs[:n_cast], refs[n_cast + 2:]):
        dst_ref[...] = src_ref[...].astype(BF16)


def _attention(q, k_x, v_x, k_c, v_c, p, pool_w, pool_scale, cast_f32, tq=1024):
    b, t, _ = q.shape
    s = k_c.shape[1]
    qw = Q_PER_KV * HEAD_DIM
    n_t = t // tq
    n_parts = N_KV_HEADS * n_t
    n_steps = b * n_parts
    pool_rows = t // n_parts
    halo_blocks = pool_rows // POOL_PAD
    assert pool_rows % POOL_PAD == 0 and all(a.shape[0] % (16 * n_steps) == 0 for a in cast_f32)
    part = lambda h, j: h * n_t + j
    kv_map = lambda i, h, j: (i, 0, h)
    pool_map = lambda i, h, j: (i, part(h, j), 0)
    slab = lambda a: pl.BlockSpec((a.shape[0] // n_steps, a.shape[1]),
                                  lambda i, h, j: (i * n_parts + part(h, j), 0))
    out = pl.pallas_call(
        functools.partial(_attn_kernel, n_parts=n_parts),
        grid=(b, N_KV_HEADS, n_t),
        in_specs=[
            pl.BlockSpec((1, tq, qw), lambda i, h, j: (i, j, h)),
            pl.BlockSpec((1, t, HEAD_DIM), kv_map),
            pl.BlockSpec((1, s, HEAD_DIM), kv_map),
            pl.BlockSpec((1, t, HEAD_DIM), kv_map),
            pl.BlockSpec((1, s, HEAD_DIM), kv_map),
            pl.BlockSpec((1, pool_rows, POOL_W), pool_map),
            pl.BlockSpec((1, POOL_PAD, POOL_W),
                         lambda i, h, j: (i, jnp.maximum(part(h, j) * halo_blocks - 1, 0), 0)),
            pl.BlockSpec((1, POOL_PAD, POOL_W),
                         lambda i, h, j: (i, jnp.minimum((part(h, j) + 1) * halo_blocks,
                                                         n_parts * halo_blocks - 1), 0)),
            pl.BlockSpec((N_POOL_GROUPS, pool_rows, HEAD_DIM), lambda i, h, j: (0, part(h, j), 0)),
            pl.BlockSpec((N_POOL_GROUPS, POOL_GROUP_W, POOL_GROUP_W), lambda i, h, j: (0, 0, 0)),
            pl.BlockSpec((1, POOL_W), lambda i, h, j: (0, 0)),
            *[slab(a) for a in cast_f32],
        ],
        out_specs=[pl.BlockSpec((1, tq, qw), lambda i, h, j: (i, j, h)),
                   pl.BlockSpec((1, pool_rows, POOL_W), pool_map),
                   *[slab(a) for a in cast_f32]],
        out_shape=[jax.ShapeDtypeStruct((b, t, ATTN_W), BF16),
                   jax.ShapeDtypeStruct((b, t, POOL_W), BF16),
                   *[jax.ShapeDtypeStruct(a.shape, BF16) for a in cast_f32]],
        compiler_params=_params("arbitrary", "arbitrary", "arbitrary"),
        name="attention",
    )(q, k_x, k_c, v_x, v_c, p, p, p, _pool_inv_counts(t), pool_w, pool_scale, *cast_f32)
    return out[0], out[1], out[2:]


def _out_proj_kernel(x_ref, *refs):
    lhs_refs, (mod_ref, nw_ref), w_refs, (o_ref, h_ref) = refs[:4], refs[4:6], refs[6:-2], refs[-2:]
    batch = pl.program_id(0)
    gate = _mod_row(mod_ref, 2, batch)
    nw = nw_ref[...]
    for r in range(x_ref.shape[1] // OUT_SUB):
        rows = slice(r * OUT_SUB, (r + 1) * OUT_SUB)
        lhs = jnp.concatenate([ref[0, rows, :] for ref in lhs_refs], axis=1)
        mix = jnp.concatenate([jnp.dot(lhs, w_ref[...], preferred_element_type=F32)
                               for w_ref in w_refs], axis=1)
        x1 = x_ref[0, rows, :] + gate * mix
        o_ref[0, rows, :] = x1
        h_ref[0, rows, :] = _norm_modulate(x1, nw, mod_ref, 3, batch).astype(BF16)


def _out_proj(x, pool_mix, attn, mod3, norm_w, w_out_bf, tm=1024):
    b, t, _ = x.shape
    row = lambda i, j: (i, j, 0)
    lhs_window = lambda c: pl.BlockSpec((1, tm, MXU_WINDOW_W), lambda i, j: (i, j, c))
    w_window = lambda c: pl.BlockSpec((MIX_W, MXU_WINDOW_W), lambda i, j: (0, c),
                                      pipeline_mode=pl.Buffered(1))
    n_w = D_MODEL // MXU_WINDOW_W
    return pl.pallas_call(
        _out_proj_kernel,
        grid=(b, t // tm),
        in_specs=[
            pl.BlockSpec((1, tm, D_MODEL), row),
            lhs_window(0), lhs_window(1), lhs_window(0), lhs_window(1),
            pl.BlockSpec((N_MOD, MOD_TILE, D_MODEL), lambda i, j: (0, 0, 0)),
            pl.BlockSpec((1, D_MODEL), lambda i, j: (0, 0)),
            *[w_window(c) for c in range(n_w)],
        ],
        out_specs=[pl.BlockSpec((1, tm, D_MODEL), row), pl.BlockSpec((1, tm, D_MODEL), row)],
        out_shape=[jax.ShapeDtypeStruct(x.shape, F32), jax.ShapeDtypeStruct(x.shape, BF16)],
        compiler_params=_params("arbitrary", "arbitrary"),
        name="out_proj",
    )(x, pool_mix, pool_mix, attn, attn, mod3, norm_w, *([w_out_bf] * n_w))


def _ffn_kernel(h_ref, x_ref, gate_ref, wg_ref, wu_ref, wd_ref, fn_ref, o_hbm, y_ref, act_ref, out_sem,
                *, n_f, tf, tn):
    i, j, s = pl.program_id(0), pl.program_id(1), pl.program_id(2)
    n_i, n_j, n_s = pl.num_programs(0), pl.num_programs(1), pl.num_programs(2)
    tm = y_ref.shape[0]
    chunks = wg_ref.shape[1] // tf
    n_up = -(-n_f // chunks)

    def out_copy(bi, tj):
        return pltpu.make_async_copy(y_ref, o_hbm.at[bi, pl.ds(tj * tm, tm), :], out_sem)

    def up_step(first_chunk, n_chunks):
        h = h_ref[0]
        for c in range(n_chunks * tf // FFN_CHUNK_W):
            cols = slice(c * FFN_CHUNK_W, (c + 1) * FFN_CHUNK_W)
            gate = jnp.dot(h, wg_ref[:, cols], preferred_element_type=F32)
            up = jnp.dot(h, wu_ref[:, cols], preferred_element_type=F32)
            half_gate = 0.5 * gate
            act = (half_gate * (1.0 + jnp.tanh(half_gate)) * up).astype(BF16)
            k, lo = divmod(c * FFN_CHUNK_W, tf)
            act_ref[first_chunk + k, :, lo:lo + FFN_CHUNK_W] = act

    @pl.when(s < n_up - 1)
    def _():
        up_step(s * chunks, chunks)

    @pl.when(s == n_up - 1)
    def _():
        up_step((n_up - 1) * chunks, n_f - (n_up - 1) * chunks)

    @pl.when((s == n_up) & ((i > 0) | (j > 0)))
    def _():
        out_copy(jnp.where(j == 0, i - 1, i), jnp.where(j == 0, n_j - 1, j - 1)).wait()

    @pl.when(s >= n_up)
    def _():
        y = jnp.dot(act_ref[0], wd_ref[0:tf, :], preferred_element_type=F32)
        for f in range(1, n_f):
            y += jnp.dot(act_ref[f], wd_ref[f * tf:(f + 1) * tf, :], preferred_element_type=F32)
        cols = pl.ds(pl.multiple_of((s - n_up) * tn, tn), tn)
        y_ref[:, cols] = x_ref[0] + _mod_row(gate_ref, 0, i) * y

    @pl.when(s == n_s - 1)
    def _():
        y_ref[...] = _rms(y_ref[...]) * fn_ref[...]
        out_copy(i, j).start()

        @pl.when((i == n_i - 1) & (j == n_j - 1))
        def _():
            out_copy(i, j).wait()


def _ffn(h, x1, mod3, wg_bf, wu_bf, wd_bf, final_norm, tm=1024, tf=512, up_chunks=2, tn=512):
    b, t, _ = h.shape
    d_ff = wg_bf.shape[1]
    n_f = d_ff // tf
    n_up = -(-n_f // up_chunks)
    n_n = D_MODEL // tn
    up_blk = lambda i, j, s: (0, jnp.minimum(s, n_up - 1))
    down_blk = lambda i, j, s: (0, jnp.maximum(s - n_up, 0))
    return pl.pallas_call(
        functools.partial(_ffn_kernel, n_f=n_f, tf=tf, tn=tn),
        grid=(b, t // tm, n_up + n_n),
        in_specs=[
            pl.BlockSpec((1, tm, D_MODEL), lambda i, j, s: (i, j, 0)),
            pl.BlockSpec((1, tm, tn), lambda i, j, s: (i, j, jnp.maximum(s - n_up, 0))),
            pl.BlockSpec((1, MOD_TILE, tn), lambda i, j, s: (N_MOD - 1, 0, jnp.maximum(s - n_up, 0))),
            pl.BlockSpec((D_MODEL, up_chunks * tf), up_blk),
            pl.BlockSpec((D_MODEL, up_chunks * tf), up_blk),
            pl.BlockSpec((d_ff, tn), down_blk),
            pl.BlockSpec((1, D_MODEL), lambda i, j, s: (0, 0)),
        ],
        out_specs=pl.BlockSpec(memory_space=pl.ANY),
        out_shape=jax.ShapeDtypeStruct(x1.shape, F32),
        scratch_shapes=[pltpu.VMEM((tm, D_MODEL), F32), pltpu.VMEM((n_f, tm, tf), BF16),
                        pltpu.SemaphoreType.DMA(())],
        compiler_params=_params("arbitrary", "arbitrary", "arbitrary"),
        name="ffn",
    )(h, x1, mod3, wg_bf, wu_bf, wd_bf, final_norm)


def _rope_tables(t):
    n_rows = t // GRID_W
    rows = np.repeat(np.arange(n_rows, dtype=np.float32), GRID_W)
    cols = np.tile(np.arange(GRID_W, dtype=np.float32), n_rows)
    freqs = np.float32(ROPE_THETA) ** (-np.arange(0, AXIS_ROT, 2, dtype=np.float32) / np.float32(AXIS_ROT))
    ang = np.concatenate([rows[:, None] * freqs, cols[:, None] * freqs], axis=-1)
    cos_full = np.repeat(np.cos(ang), 2, axis=-1)
    sin = np.sin(ang)
    sin_signed = np.stack([-sin, sin], axis=-1).reshape(t, HEAD_DIM)
    return jnp.asarray(cos_full, F32), jnp.asarray(sin_signed, F32)


def kernel(x, c, ctx, c_ctx, w_ada, b_ada, norm_mix, norm_ffn, w_in, pool_w, pool_scale,
           q_norm, k_norm, w_out, w_gate, w_up, w_down, final_norm):
    depth = w_ada.shape[0]
    assert depth == 1, "context tokens are only updated between layers; one layer is implemented"
    b, t, _ = x.shape
    cos_full, sin_signed = _rope_tables(t)

    mod3 = _adaln_mod(c, c_ctx, w_ada[0], b_ada)

    w_in_bf = w_in[0]
    k_c, v_c = _ctx_kv(ctx, mod3, norm_mix, w_in_bf, k_norm)
    p, q, k_x, v_x = _in_proj(x, mod3, norm_mix, w_in_bf, q_norm, k_norm, cos_full, sin_signed)
    attn, pooled, (wo_bf, wg_bf, wu_bf, wd_bf) = _attention(
        q, k_x, v_x, k_c, v_c, p, pool_w[0], pool_scale, (w_out[0], w_gate[0], w_up[0], w_down[0]))
    x1, hf = _out_proj(x, pooled, attn, mod3, norm_ffn, wo_bf)
    return _ffn(hf, x1, mod3, wg_bf, wu_bf, wd_bf, final_norm.reshape(1, D_MODEL))
```

```python
import functools
import math

import jax
import jax.numpy as jnp
import numpy as np
from jax import lax
from jax.experimental import pallas as pl
from jax.experimental.pallas import tpu as pltpu

D_MODEL = 2048
GRID_W = 64
HEAD_DIM = 128
N_HEADS = 8
N_KV_HEADS = 2
Q_PER_KV = N_HEADS // N_KV_HEADS
ATTN_W = N_HEADS * HEAD_DIM
KV_W = N_KV_HEADS * HEAD_DIM
POOL_WINDOWS = (2, 4, 8, 16)
N_POOL_GROUPS = len(POOL_WINDOWS)
POOL_W = D_MODEL // 2
POOL_GROUP_W = POOL_W // N_POOL_GROUPS
MIX_W = POOL_W + ATTN_W
PROJ_W = POOL_W + ATTN_W + 2 * KV_W
ROPE_THETA = 10000.0
AXIS_ROT = HEAD_DIM // 2
EPS = 1e-6
N_MOD = 6
MOD_ROWS = 16
MOD_TILE = 8
POOL_PAD = 8
ROW_SUB = 256
IN_SUB = 512
OUT_SUB = 512
MXU_WINDOW_W = 512
ATTN_SUB = 256
FFN_CHUNK_W = 256
Q_SCALE = math.log2(math.e) / math.sqrt(HEAD_DIM)

F32 = jnp.float32
BF16 = jnp.bfloat16

VMEM_LIMIT_BYTES = 63 * 1024 * 1024


def _params(*sem):
    return pltpu.CompilerParams(dimension_semantics=sem, vmem_limit_bytes=VMEM_LIMIT_BYTES)


def _rms(x):
    return x * lax.rsqrt(jnp.mean(x * x, axis=-1, keepdims=True) + EPS)


def _adaln_kernel(c_ref, cctx_ref, w_ref, b_ref, o_ref):
    c = jnp.concatenate([c_ref[...], jnp.broadcast_to(cctx_ref[...], (MOD_ROWS - MOD_TILE, D_MODEL))], axis=0)
    a = c * jax.nn.sigmoid(c)
    o_ref[0] = jnp.dot(a, w_ref[...], preferred_element_type=F32) + b_ref[...]


def _adaln_mod(c, c_ctx, w_ada, b_ada, tn=1024):
    assert c.shape[0] == MOD_TILE
    per_chunk = D_MODEL // tn
    return pl.pallas_call(
        _adaln_kernel,
        grid=(N_MOD * per_chunk,),
        in_specs=[
            pl.BlockSpec((MOD_TILE, D_MODEL), lambda j: (0, 0)),
            pl.BlockSpec((1, D_MODEL), lambda j: (0, 0)),
            pl.BlockSpec((D_MODEL, tn), lambda j: (0, j)),
            pl.BlockSpec((1, tn), lambda j: (0, j)),
        ],
        out_specs=pl.BlockSpec((1, MOD_ROWS, tn), lambda j: (j // per_chunk, 0, j % per_chunk)),
        out_shape=jax.ShapeDtypeStruct((N_MOD, MOD_ROWS, D_MODEL), F32),
        compiler_params=_params("arbitrary"),
        name="adaln_mod",
    )(c, c_ctx.reshape(1, D_MODEL), w_ada, b_ada)


def _mod_row(mod_ref, chunk, row):
    return mod_ref[chunk, pl.ds(row, 1), :]


def _norm_modulate(x, nw, mod_ref, shift_idx, row):
    sh = _mod_row(mod_ref, shift_idx, row)
    sc = _mod_row(mod_ref, shift_idx + 1, row)
    return (_rms(x) * nw) * (1.0 + sc) + sh


def _rope(x, cos, sin_signed, even_lane):
    partner = jnp.where(even_lane, pltpu.roll(x, HEAD_DIM - 1, axis=1), pltpu.roll(x, 1, axis=1))
    return x * cos + partner * sin_signed


def _ctx_kv_kernel(x_ref, mod_ref, nw_ref, w_ref, kn_ref, k_ref, v_ref):
    nw = nw_ref[...]
    kn = kn_ref[...]
    for r in range(x_ref.shape[0] // ROW_SUB):
        rows = slice(r * ROW_SUB, (r + 1) * ROW_SUB)
        h = _norm_modulate(x_ref[rows, :], nw, mod_ref, 0, 0)
        kv = jnp.dot(h.astype(BF16), w_ref[...].astype(BF16), preferred_element_type=F32)
        for j in range(N_KV_HEADS):
            cols = slice(j * HEAD_DIM, (j + 1) * HEAD_DIM)
            k_ref[rows, cols] = (_rms(kv[:, cols]) * kn).astype(BF16)
        v_ref[rows, :] = kv[:, KV_W:].astype(BF16)


def _ctx_kv(ctx, mod3, norm_w, w_in_bf, k_norm, tm=1024):
    b, s, _ = ctx.shape
    kv_block = (POOL_W + ATTN_W) // (2 * KV_W)
    k, v = pl.pallas_call(
        _ctx_kv_kernel,
        grid=(b * s // tm,),
        in_specs=[
            pl.BlockSpec((tm, D_MODEL), lambda i: (i, 0)),
            pl.BlockSpec((N_MOD, MOD_TILE, D_MODEL), lambda i: (0, 1, 0)),
            pl.BlockSpec((1, D_MODEL), lambda i: (0, 0)),
            pl.BlockSpec((D_MODEL, 2 * KV_W), lambda i: (0, kv_block)),
            pl.BlockSpec((1, HEAD_DIM), lambda i: (0, 0)),
        ],
        out_specs=[pl.BlockSpec((tm, KV_W), lambda i: (i, 0))] * 2,
        out_shape=[jax.ShapeDtypeStruct((b * s, KV_W), BF16)] * 2,
        compiler_params=_params("arbitrary"),
        name="ctx_kv",
    )(ctx.reshape(b * s, D_MODEL), mod3, norm_w, w_in_bf, k_norm)
    return k.reshape(b, s, KV_W), v.reshape(b, s, KV_W)


def _in_proj_kernel(x_ref, mod_ref, nw_ref, w_ref, qn_ref, kn_ref, cos_ref, sin_ref,
                    p_ref, q_ref, k_ref, v_ref):
    nw = nw_ref[...]
    qn = qn_ref[...] * Q_SCALE
    kn = kn_ref[...]
    even_lane = (lax.broadcasted_iota(jnp.int32, (IN_SUB, HEAD_DIM), 1) % 2) == 0
    for r in range(x_ref.shape[1] // IN_SUB):
        rows = slice(r * IN_SUB, (r + 1) * IN_SUB)
        h = _norm_modulate(x_ref[0, rows, :], nw, mod_ref, 0, pl.program_id(0)).astype(BF16)
        qkv = jnp.dot(h, w_ref[:, POOL_W:].astype(BF16), preferred_element_type=F32)
        cos = cos_ref[rows, :]
        sin = sin_ref[rows, :]
        for j in range(N_HEADS):
            cols = slice(j * HEAD_DIM, (j + 1) * HEAD_DIM)
            q_ref[0, rows, cols] = _rope(_rms(qkv[:, cols]) * qn, cos, sin, even_lane).astype(BF16)
        for j in range(N_KV_HEADS):
            lo = ATTN_W + j * HEAD_DIM
            kh = _rms(qkv[:, lo:lo + HEAD_DIM]) * kn
            k_ref[0, rows, j * HEAD_DIM:(j + 1) * HEAD_DIM] = _rope(kh, cos, sin, even_lane).astype(BF16)
        v_ref[0, rows, :] = qkv[:, ATTN_W + KV_W:].astype(BF16)
        p_ref[0, rows, :] = jnp.dot(h, w_ref[:, :POOL_W].astype(BF16), preferred_element_type=F32)


def _in_proj(x, mod3, norm_w, w_in_bf, q_norm, k_norm, cos_full, sin_signed, tm=1024):
    b, t, _ = x.shape
    row = lambda i, j: (i, j, 0)
    const2 = lambda i, j: (0, 0)
    return pl.pallas_call(
        _in_proj_kernel,
        grid=(b, t // tm),
        in_specs=[
            pl.BlockSpec((1, tm, D_MODEL), row),
            pl.BlockSpec((N_MOD, MOD_TILE, D_MODEL), lambda i, j: (0, 0, 0)),
            pl.BlockSpec((1, D_MODEL), const2),
            pl.BlockSpec((D_MODEL, PROJ_W), const2, pipeline_mode=pl.Buffered(1)),
            pl.BlockSpec((1, HEAD_DIM), const2),
            pl.BlockSpec((1, HEAD_DIM), const2),
            pl.BlockSpec((tm, HEAD_DIM), lambda i, j: (j, 0)),
            pl.BlockSpec((tm, HEAD_DIM), lambda i, j: (j, 0)),
        ],
        out_specs=[
            pl.BlockSpec((1, tm, POOL_W), row),
            pl.BlockSpec((1, tm, ATTN_W), row),
            pl.BlockSpec((1, tm, KV_W), row),
            pl.BlockSpec((1, tm, KV_W), row),
        ],
        out_shape=[
            jax.ShapeDtypeStruct((b, t, POOL_W), F32),
            jax.ShapeDtypeStruct((b, t, ATTN_W), BF16),
            jax.ShapeDtypeStruct((b, t, KV_W), BF16),
            jax.ShapeDtypeStruct((b, t, KV_W), BF16),
        ],
        compiler_params=_params("arbitrary", "arbitrary"),
        name="in_proj",
    )(x, mod3, norm_w, w_in_bf, q_norm, k_norm, cos_full, sin_signed)


def _pool_inv_counts(t):
    pos = np.arange(t)[None, :]
    half = np.array(POOL_WINDOWS)[:, None] // 2
    count = np.minimum(pos + half, t) - np.maximum(pos - half, 0)
    inv = (np.float32(1.0) / count.astype(np.float32))[:, :, None]
    return jnp.asarray(np.broadcast_to(inv, (N_POOL_GROUPS, t, HEAD_DIM)))


def _pool_mix_rows(p_ref, prev_ref, next_ref, inv_ref, w_ref, s_ref, o_ref, has_prev, has_next):
    t = p_ref.shape[1]
    n = t + 2 * POOL_PAD
    for g, win in enumerate(POOL_WINDOWS):
        half = win // 2
        cols = slice(g * POOL_GROUP_W, (g + 1) * POOL_GROUP_W)
        u = p_ref[0, :, cols]
        top = jnp.where(has_prev, prev_ref[0, :, cols], 0.0)
        bottom = jnp.where(has_next, next_ref[0, :, cols], 0.0)
        f = jnp.concatenate([top, u, bottom], axis=0)
        step = 1
        while step < win:
            f = f + pltpu.roll(f, n - step, axis=0)
            step *= 2
        if half != POOL_PAD:
            f = pltpu.roll(f, n - (POOL_PAD - half), axis=0)
        inv = inv_ref[g]
        pooled = f[:t] * jnp.concatenate([inv] * (POOL_GROUP_W // inv.shape[1]), axis=1) - u
        mixed = jnp.dot(pooled.astype(BF16), w_ref[g].astype(BF16), preferred_element_type=F32)
        o_ref[0, :, cols] = (mixed * s_ref[:, cols]).astype(BF16)


def _attn_kernel(q_ref, kx_ref, kc_ref, vx_ref, vc_ref,
                 p_ref, prev_ref, next_ref, inv_ref, pw_ref, ps_ref, *refs, n_parts):
    n_cast = (len(refs) - 2) // 2
    o_ref, pool_ref = refs[n_cast], refs[n_cast + 1]
    kx = kx_ref[0]
    kc = kc_ref[0]
    vx = jnp.concatenate([vx_ref[0], jnp.ones(vx_ref.shape[1:], BF16)], axis=1)
    vc = jnp.concatenate([vc_ref[0], jnp.ones(vc_ref.shape[1:], BF16)], axis=1)
    nt = (((1,), (1,)), ((), ()))
    for r in range(q_ref.shape[1] // ATTN_SUB):
        rows = slice(r * ATTN_SUB, (r + 1) * ATTN_SUB)
        for g in range(Q_PER_KV):
            cols = slice(g * HEAD_DIM, (g + 1) * HEAD_DIM)
            q = q_ref[0, rows, cols]
            sx = lax.dot_general(q, kx, nt, preferred_element_type=F32)
            sc = lax.dot_general(q, kc, nt, preferred_element_type=F32)
            m = jnp.maximum(jnp.max(sx, axis=-1, keepdims=True), jnp.max(sc, axis=-1, keepdims=True))
            px = jnp.exp2(sx - m)
            pc = jnp.exp2(sc - m)
            o = (jnp.dot(px.astype(BF16), vx, preferred_element_type=F32)
                 + jnp.dot(pc.astype(BF16), vc, preferred_element_type=F32))
            o_ref[0, rows, cols] = (o[:, :HEAD_DIM] * (1.0 / o[:, HEAD_DIM:])).astype(BF16)
    part = pl.program_id(1) * pl.num_programs(2) + pl.program_id(2)
    _pool_mix_rows(p_ref, prev_ref, next_ref, inv_ref, pw_ref, ps_ref, pool_ref,
                   part > 0, part < n_parts - 1)
    for src_ref, dst_ref in zip(refs[:n_cast], refs[n_cast + 2:]):
        dst_ref[...] = src_ref[...].astype(BF16)


def _attention(q, k_x, v_x, k_c, v_c, p, pool_w, pool_scale, cast_f32, tq=1024):
    b, t, _ = q.shape
    s = k_c.shape[1]
    qw = Q_PER_KV * HEAD_DIM
    n_t = t // tq
    n_parts = N_KV_HEADS * n_t
    n_steps = b * n_parts
    pool_rows = t // n_parts
    halo_blocks = pool_rows // POOL_PAD
    assert pool_rows % POOL_PAD == 0 and all(a.shape[0] % (16 * n_steps) == 0 for a in cast_f32)
    part = lambda h, j: h * n_t + j
    kv_map = lambda i, h, j: (i, 0, h)
    pool_map = lambda i, h, j: (i, part(h, j), 0)
    slab = lambda a: pl.BlockSpec((a.shape[0] // n_steps, a.shape[1]),
                                  lambda i, h, j: (i * n_parts + part(h, j), 0))
    out = pl.pallas_call(
        functools.partial(_attn_kernel, n_parts=n_parts),
        grid=(b, N_KV_HEADS, n_t),
        in_specs=[
            pl.BlockSpec((1, tq, qw), lambda i, h, j: (i, j, h)),
            pl.BlockSpec((1, t, HEAD_DIM), kv_map),
            pl.BlockSpec((1, s, HEAD_DIM), kv_map),
            pl.BlockSpec((1, t, HEAD_DIM), kv_map),
            pl.BlockSpec((1, s, HEAD_DIM), kv_map),
            pl.BlockSpec((1, pool_rows, POOL_W), pool_map),
            pl.BlockSpec((1, POOL_PAD, POOL_W),
                         lambda i, h, j: (i, jnp.maximum(part(h, j) * halo_blocks - 1, 0), 0)),
            pl.BlockSpec((1, POOL_PAD, POOL_W),
                         lambda i, h, j: (i, jnp.minimum((part(h, j) + 1) * halo_blocks,
                                                         n_parts * halo_blocks - 1), 0)),
            pl.BlockSpec((N_POOL_GROUPS, pool_rows, HEAD_DIM), lambda i, h, j: (0, part(h, j), 0)),
            pl.BlockSpec((N_POOL_GROUPS, POOL_GROUP_W, POOL_GROUP_W), lambda i, h, j: (0, 0, 0)),
            pl.BlockSpec((1, POOL_W), lambda i, h, j: (0, 0)),
            *[slab(a) for a in cast_f32],
        ],
        out_specs=[pl.BlockSpec((1, tq, qw), lambda i, h, j: (i, j, h)),
                   pl.BlockSpec((1, pool_rows, POOL_W), pool_map),
                   *[slab(a) for a in cast_f32]],
        out_shape=[jax.ShapeDtypeStruct((b, t, ATTN_W), BF16),
                   jax.ShapeDtypeStruct((b, t, POOL_W), BF16),
                   *[jax.ShapeDtypeStruct(a.shape, BF16) for a in cast_f32]],
        compiler_params=_params("arbitrary", "arbitrary", "arbitrary"),
        name="attention",
    )(q, k_x, k_c, v_x, v_c, p, p, p, _pool_inv_counts(t), pool_w, pool_scale, *cast_f32)
    return out[0], out[1], out[2:]


def _out_proj_kernel(x_ref, *refs):
    lhs_refs, (mod_ref, nw_ref), w_refs, (o_ref, h_ref) = refs[:4], refs[4:6], refs[6:-2], refs[-2:]
    batch = pl.program_id(0)
    gate = _mod_row(mod_ref, 2, batch)
    nw = nw_ref[...]
    for r in range(x_ref.shape[1] // OUT_SUB):
        rows = slice(r * OUT_SUB, (r + 1) * OUT_SUB)
        lhs = jnp.concatenate([ref[0, rows, :] for ref in lhs_refs], axis=1)
        mix = jnp.concatenate([jnp.dot(lhs, w_ref[...], preferred_element_type=F32)
                               for w_ref in w_refs], axis=1)
        x1 = x_ref[0, rows, :] + gate * mix
        o_ref[0, rows, :] = x1
        h_ref[0, rows, :] = _norm_modulate(x1, nw, mod_ref, 3, batch).astype(BF16)


def _out_proj(x, pool_mix, attn, mod3, norm_w, w_out_bf, tm=1024):
    b, t, _ = x.shape
    row = lambda i, j: (i, j, 0)
    lhs_window = lambda c: pl.BlockSpec((1, tm, MXU_WINDOW_W), lambda i, j: (i, j, c))
    w_window = lambda c: pl.BlockSpec((MIX_W, MXU_WINDOW_W), lambda i, j: (0, c),
                                      pipeline_mode=pl.Buffered(1))
    n_w = D_MODEL // MXU_WINDOW_W
    return pl.pallas_call(
        _out_proj_kernel,
        grid=(b, t // tm),
        in_specs=[
            pl.BlockSpec((1, tm, D_MODEL), row),
            lhs_window(0), lhs_window(1), lhs_window(0), lhs_window(1),
            pl.BlockSpec((N_MOD, MOD_TILE, D_MODEL), lambda i, j: (0, 0, 0)),
            pl.BlockSpec((1, D_MODEL), lambda i, j: (0, 0)),
            *[w_window(c) for c in range(n_w)],
        ],
        out_specs=[pl.BlockSpec((1, tm, D_MODEL), row), pl.BlockSpec((1, tm, D_MODEL), row)],
        out_shape=[jax.ShapeDtypeStruct(x.shape, F32), jax.ShapeDtypeStruct(x.shape, BF16)],
        compiler_params=_params("arbitrary", "arbitrary"),
        name="out_proj",
    )(x, pool_mix, pool_mix, attn, attn, mod3, norm_w, *([w_out_bf] * n_w))


def _ffn_kernel(h_ref, x_ref, gate_ref, wg_ref, wu_ref, wd_ref, fn_ref, o_hbm, y_ref, act_ref, out_sem,
                *, n_f, tf, tn):
    i, j, s = pl.program_id(0), pl.program_id(1), pl.program_id(2)
    n_i, n_j, n_s = pl.num_programs(0), pl.num_programs(1), pl.num_programs(2)
    tm = y_ref.shape[0]
    chunks = wg_ref.shape[1] // tf
    n_up = -(-n_f // chunks)

    def out_copy(bi, tj):
        return pltpu.make_async_copy(y_ref, o_hbm.at[bi, pl.ds(tj * tm, tm), :], out_sem)

    def up_step(first_chunk, n_chunks):
        h = h_ref[0]
        for c in range(n_chunks * tf // FFN_CHUNK_W):
            cols = slice(c * FFN_CHUNK_W, (c + 1) * FFN_CHUNK_W)
            gate = jnp.dot(h, wg_ref[:, cols], preferred_element_type=F32)
            up = jnp.dot(h, wu_ref[:, cols], preferred_element_type=F32)
            half_gate = 0.5 * gate
            act = (half_gate * (1.0 + jnp.tanh(half_gate)) * up).astype(BF16)
            k, lo = divmod(c * FFN_CHUNK_W, tf)
            act_ref[first_chunk + k, :, lo:lo + FFN_CHUNK_W] = act

    @pl.when(s < n_up - 1)
    def _():
        up_step(s * chunks, chunks)

    @pl.when(s == n_up - 1)
    def _():
        up_step((n_up - 1) * chunks, n_f - (n_up - 1) * chunks)

    @pl.when((s == n_up) & ((i > 0) | (j > 0)))
    def _():
        out_copy(jnp.where(j == 0, i - 1, i), jnp.where(j == 0, n_j - 1, j - 1)).wait()

    @pl.when(s >= n_up)
    def _():
        y = jnp.dot(act_ref[0], wd_ref[0:tf, :], preferred_element_type=F32)
        for f in range(1, n_f):
            y += jnp.dot(act_ref[f], wd_ref[f * tf:(f + 1) * tf, :], preferred_element_type=F32)
        cols = pl.ds(pl.multiple_of((s - n_up) * tn, tn), tn)
        y_ref[:, cols] = x_ref[0] + _mod_row(gate_ref, 0, i) * y

    @pl.when(s == n_s - 1)
    def _():
        y_ref[...] = _rms(y_ref[...]) * fn_ref[...]
        out_copy(i, j).start()

        @pl.when((i == n_i - 1) & (j == n_j - 1))
        def _():
            out_copy(i, j).wait()


def _ffn(h, x1, mod3, wg_bf, wu_bf, wd_bf, final_norm, tm=1024, tf=512, up_chunks=2, tn=512):
    b, t, _ = h.shape
    d_ff = wg_bf.shape[1]
    n_f = d_ff // tf
    n_up = -(-n_f // up_chunks)
    n_n = D_MODEL // tn
    up_blk = lambda i, j, s: (0, jnp.minimum(s, n_up - 1))
    down_blk = lambda i, j, s: (0, jnp.maximum(s - n_up, 0))
    return pl.pallas_call(
        functools.partial(_ffn_kernel, n_f=n_f, tf=tf, tn=tn),
        grid=(b, t // tm, n_up + n_n),
        in_specs=[
            pl.BlockSpec((1, tm, D_MODEL), lambda i, j, s: (i, j, 0)),
            pl.BlockSpec((1, tm, tn), lambda i, j, s: (i, j, jnp.maximum(s - n_up, 0))),
            pl.BlockSpec((1, MOD_TILE, tn), lambda i, j, s: (N_MOD - 1, 0, jnp.maximum(s - n_up, 0))),
            pl.BlockSpec((D_MODEL, up_chunks * tf), up_blk),
            pl.BlockSpec((D_MODEL, up_chunks * tf), up_blk),
            pl.BlockSpec((d_ff, tn), down_blk),
            pl.BlockSpec((1, D_MODEL), lambda i, j, s: (0, 0)),
        ],
        out_specs=pl.BlockSpec(memory_space=pl.ANY),
        out_shape=jax.ShapeDtypeStruct(x1.shape, F32),
        scratch_shapes=[pltpu.VMEM((tm, D_MODEL), F32), pltpu.VMEM((n_f, tm, tf), BF16),
                        pltpu.SemaphoreType.DMA(())],
        compiler_params=_params("arbitrary", "arbitrary", "arbitrary"),
        name="ffn",
    )(h, x1, mod3, wg_bf, wu_bf, wd_bf, final_norm)


def _rope_tables(t):
    n_rows = t // GRID_W
    rows = np.repeat(np.arange(n_rows, dtype=np.float32), GRID_W)
    cols = np.tile(np.arange(GRID_W, dtype=np.float32), n_rows)
    freqs = np.float32(ROPE_THETA) ** (-np.arange(0, AXIS_ROT, 2, dtype=np.float32) / np.float32(AXIS_ROT))
    ang = np.concatenate([rows[:, None] * freqs, cols[:, None] * freqs], axis=-1)
    cos_full = np.repeat(np.cos(ang), 2, axis=-1)
    sin = np.sin(ang)
    sin_signed = np.stack([-sin, sin], axis=-1).reshape(t, HEAD_DIM)
    return jnp.asarray(cos_full, F32), jnp.asarray(sin_signed, F32)


def kernel(x, c, ctx, c_ctx, w_ada, b_ada, norm_mix, norm_ffn, w_in, pool_w, pool_scale,
           q_norm, k_norm, w_out, w_gate, w_up, w_down, final_norm):
    depth = w_ada.shape[0]
    assert depth == 1, "context tokens are only updated between layers; one layer is implemented"
    b, t, _ = x.shape
    cos_full, sin_signed = _rope_tables(t)

    mod3 = _adaln_mod(c, c_ctx, w_ada[0], b_ada)

    w_in_bf = w_in[0]
    k_c, v_c = _ctx_kv(ctx, mod3, norm_mix, w_in_bf, k_norm)
    p, q, k_x, v_x = _in_proj(x, mod3, norm_mix, w_in_bf, q_norm, k_norm, cos_full, sin_signed)
    attn, pooled, (wo_bf, wg_bf, wu_bf, wd_bf) = _attention(
        q, k_x, v_x, k_c, v_c, p, pool_w[0], pool_scale, (w_out[0], w_gate[0], w_up[0], w_down[0]))
    x1, hf = _out_proj(x, pooled, attn, mod3, norm_ffn, wo_bf)
    return _ffn(hf, x1, mod3, wg_bf, wu_bf, wd_bf, final_norm.reshape(1, D_MODEL))
```

```python
import functools
import math

import jax
import jax.numpy as jnp
import numpy as np
from jax import lax
from jax.experimental import pallas as pl
from jax.experimental.pallas import tpu as pltpu

D_MODEL = 2048
GRID_W = 64
HEAD_DIM = 128
N_HEADS = 8
N_KV_HEADS = 2
Q_PER_KV = N_HEADS // N_KV_HEADS
ATTN_W = N_HEADS * HEAD_DIM
KV_W = N_KV_HEADS * HEAD_DIM
POOL_WINDOWS = (2, 4, 8, 16)
N_POOL_GROUPS = len(POOL_WINDOWS)
POOL_W = D_MODEL // 2
POOL_GROUP_W = POOL_W // N_POOL_GROUPS
MIX_W = POOL_W + ATTN_W
PROJ_W = POOL_W + ATTN_W + 2 * KV_W
ROPE_THETA = 10000.0
AXIS_ROT = HEAD_DIM // 2
EPS = 1e-6
N_MOD = 6
MOD_ROWS = 16
MOD_TILE = 8
POOL_PAD = 8
IN_SUB = 512
OUT_SUB = 512
MXU_WINDOW_W = 512
ATTN_SUB = 256
FFN_CHUNK_W = 256
Q_SCALE = math.log2(math.e) / math.sqrt(HEAD_DIM)

F32 = jnp.float32
BF16 = jnp.bfloat16

VMEM_LIMIT_BYTES = 63 * 1024 * 1024


def _params(*sem):
    return pltpu.CompilerParams(dimension_semantics=sem, vmem_limit_bytes=VMEM_LIMIT_BYTES)


def _rms(x):
    return x * lax.rsqrt(jnp.mean(x * x, axis=-1, keepdims=True) + EPS)


def _adaln_kernel(c_ref, cctx_ref, w_ref, b_ref, o_ref):
    c = jnp.concatenate([c_ref[...], jnp.broadcast_to(cctx_ref[...], (MOD_ROWS - MOD_TILE, D_MODEL))], axis=0)
    a = c * jax.nn.sigmoid(c)
    o_ref[0] = jnp.dot(a, w_ref[...], preferred_element_type=F32) + b_ref[...]


def _adaln_mod(c, c_ctx, w_ada, b_ada, tn=1024):
    assert c.shape[0] == MOD_TILE
    per_chunk = D_MODEL // tn
    return pl.pallas_call(
        _adaln_kernel,
        grid=(N_MOD * per_chunk,),
        in_specs=[
            pl.BlockSpec((MOD_TILE, D_MODEL), lambda j: (0, 0)),
            pl.BlockSpec((1, D_MODEL), lambda j: (0, 0)),
            pl.BlockSpec((D_MODEL, tn), lambda j: (0, j)),
            pl.BlockSpec((1, tn), lambda j: (0, j)),
        ],
        out_specs=pl.BlockSpec((1, MOD_ROWS, tn), lambda j: (j // per_chunk, 0, j % per_chunk)),
        out_shape=jax.ShapeDtypeStruct((N_MOD, MOD_ROWS, D_MODEL), F32),
        compiler_params=_params("arbitrary"),
        name="adaln_mod",
    )(c, c_ctx.reshape(1, D_MODEL), w_ada, b_ada)


def _mod_row(mod_ref, chunk, row):
    return mod_ref[chunk, pl.ds(row, 1), :]


def _norm_modulate(x, nw, mod_ref, shift_idx, row):
    sh = _mod_row(mod_ref, shift_idx, row)
    sc = _mod_row(mod_ref, shift_idx + 1, row)
    return (_rms(x) * nw) * (1.0 + sc) + sh


def _rope(x, cos, sin_signed, even_lane):
    partner = jnp.where(even_lane, pltpu.roll(x, HEAD_DIM - 1, axis=1), pltpu.roll(x, 1, axis=1))
    return x * cos + partner * sin_signed


def _ctx_kv_kernel(x_ref, mod_ref, nw_ref, w_ref, kn_ref, k_ref, v_ref):
    nw = nw_ref[...]
    kn = kn_ref[...]
    for r in range(x_ref.shape[0] // IN_SUB):
        rows = slice(r * IN_SUB, (r + 1) * IN_SUB)
        h = _norm_modulate(x_ref[rows, :], nw, mod_ref, 0, 0)
        kv = jnp.dot(h.astype(BF16), w_ref[...].astype(BF16), preferred_element_type=F32)
        for j in range(N_KV_HEADS):
            cols = slice(j * HEAD_DIM, (j + 1) * HEAD_DIM)
            k_ref[rows, cols] = (_rms(kv[:, cols]) * kn).astype(BF16)
        v_ref[rows, :] = kv[:, KV_W:].astype(BF16)


def _ctx_kv(ctx, mod3, norm_w, w_in, k_norm, tm=1024):
    b, s, _ = ctx.shape
    kv_block = (POOL_W + ATTN_W) // (2 * KV_W)
    k, v = pl.pallas_call(
        _ctx_kv_kernel,
        grid=(b * s // tm,),
        in_specs=[
            pl.BlockSpec((tm, D_MODEL), lambda i: (i, 0)),
            pl.BlockSpec((N_MOD, MOD_TILE, D_MODEL), lambda i: (0, 1, 0)),
            pl.BlockSpec((1, D_MODEL), lambda i: (0, 0)),
            pl.BlockSpec((D_MODEL, 2 * KV_W), lambda i: (0, kv_block)),
            pl.BlockSpec((1, HEAD_DIM), lambda i: (0, 0)),
        ],
        out_specs=[pl.BlockSpec((tm, KV_W), lambda i: (i, 0))] * 2,
        out_shape=[jax.ShapeDtypeStruct((b * s, KV_W), BF16)] * 2,
        compiler_params=_params("arbitrary"),
        name="ctx_kv",
    )(ctx.reshape(b * s, D_MODEL), mod3, norm_w, w_in, k_norm)
    return k.reshape(b, s, KV_W), v.reshape(b, s, KV_W)


def _in_proj_kernel(x_ref, mod_ref, nw_ref, w_ref, qn_ref, kn_ref, cos_ref, sin_ref,
                    p_ref, q_ref, k_ref, v_ref):
    nw = nw_ref[...]
    qn = qn_ref[...] * Q_SCALE
    kn = kn_ref[...]
    even_lane = (lax.broadcasted_iota(jnp.int32, (IN_SUB, HEAD_DIM), 1) % 2) == 0
    for r in range(x_ref.shape[1] // IN_SUB):
        rows = slice(r * IN_SUB, (r + 1) * IN_SUB)
        h = _norm_modulate(x_ref[0, rows, :], nw, mod_ref, 0, pl.program_id(0)).astype(BF16)
        qkv = jnp.dot(h, w_ref[:, POOL_W:].astype(BF16), preferred_element_type=F32)
        cos = cos_ref[rows, :]
        sin = sin_ref[rows, :]
        for j in range(N_HEADS):
            cols = slice(j * HEAD_DIM, (j + 1) * HEAD_DIM)
            q_ref[0, rows, cols] = _rope(_rms(qkv[:, cols]) * qn, cos, sin, even_lane).astype(BF16)
        for j in range(N_KV_HEADS):
            lo = ATTN_W + j * HEAD_DIM
            kh = _rms(qkv[:, lo:lo + HEAD_DIM]) * kn
            k_ref[0, rows, j * HEAD_DIM:(j + 1) * HEAD_DIM] = _rope(kh, cos, sin, even_lane).astype(BF16)
        v_ref[0, rows, :] = qkv[:, ATTN_W + KV_W:].astype(BF16)
        p_ref[0, rows, :] = jnp.dot(h, w_ref[:, :POOL_W].astype(BF16), preferred_element_type=F32)


def _in_proj(x, mod3, norm_w, w_in, q_norm, k_norm, cos_full, sin_signed, tm=1024):
    b, t, _ = x.shape
    row = lambda i, j: (i, j, 0)
    const2 = lambda i, j: (0, 0)
    return pl.pallas_call(
        _in_proj_kernel,
        grid=(b, t // tm),
        in_specs=[
            pl.BlockSpec((1, tm, D_MODEL), row),
            pl.BlockSpec((N_MOD, MOD_TILE, D_MODEL), lambda i, j: (0, 0, 0)),
            pl.BlockSpec((1, D_MODEL), const2),
            pl.BlockSpec((D_MODEL, PROJ_W), const2, pipeline_mode=pl.Buffered(1)),
            pl.BlockSpec((1, HEAD_DIM), const2),
            pl.BlockSpec((1, HEAD_DIM), const2),
            pl.BlockSpec((tm, HEAD_DIM), lambda i, j: (j, 0)),
            pl.BlockSpec((tm, HEAD_DIM), lambda i, j: (j, 0)),
        ],
        out_specs=[
            pl.BlockSpec((1, tm, POOL_W), row),
            pl.BlockSpec((1, tm, ATTN_W), row),
            pl.BlockSpec((1, tm, KV_W), row),
            pl.BlockSpec((1, tm, KV_W), row),
        ],
        out_shape=[
            jax.ShapeDtypeStruct((b, t, POOL_W), F32),
            jax.ShapeDtypeStruct((b, t, ATTN_W), BF16),
            jax.ShapeDtypeStruct((b, t, KV_W), BF16),
            jax.ShapeDtypeStruct((b, t, KV_W), BF16),
        ],
        compiler_params=_params("arbitrary", "arbitrary"),
        name="in_proj",
    )(x, mod3, norm_w, w_in, q_norm, k_norm, cos_full, sin_signed)


def _pool_inv_counts(t):
    pos = np.arange(t)[None, :]
    half = np.array(POOL_WINDOWS)[:, None] // 2
    count = np.minimum(pos + half, t) - np.maximum(pos - half, 0)
    inv = (np.float32(1.0) / count.astype(np.float32))[:, :, None]
    return jnp.asarray(np.broadcast_to(inv, (N_POOL_GROUPS, t, HEAD_DIM)))


def _pool_mix_rows(p_ref, prev_ref, next_ref, inv_ref, w_ref, s_ref, o_ref, has_prev, has_next):
    t = p_ref.shape[1]
    n = t + 2 * POOL_PAD
    for g, win in enumerate(POOL_WINDOWS):
        half = win // 2
        cols = slice(g * POOL_GROUP_W, (g + 1) * POOL_GROUP_W)
        u = p_ref[0, :, cols]
        top = jnp.where(has_prev, prev_ref[0, :, cols], 0.0)
        bottom = jnp.where(has_next, next_ref[0, :, cols], 0.0)
        f = jnp.concatenate([top, u, bottom], axis=0)
        step = 1
        while step < win:
            f = f + pltpu.roll(f, n - step, axis=0)
            step *= 2
        if half != POOL_PAD:
            f = pltpu.roll(f, n - (POOL_PAD - half), axis=0)
        inv = inv_ref[g]
        pooled = f[:t] * jnp.concatenate([inv] * (POOL_GROUP_W // inv.shape[1]), axis=1) - u
        mixed = jnp.dot(pooled.astype(BF16), w_ref[g].astype(BF16), preferred_element_type=F32)
        o_ref[0, :, cols] = (mixed * s_ref[:, cols]).astype(BF16)


def _attn_kernel(q_ref, kx_ref, kc_ref, vx_ref, vc_ref,
                 p_ref, prev_ref, next_ref, inv_ref, pw_ref, ps_ref, *refs, n_parts):
    n_cast = (len(refs) - 2) // 2
    o_ref, pool_ref = refs[n_cast], refs[n_cast + 1]
    kx = kx_ref[0]
    kc = kc_ref[0]
    vx = jnp.concatenate([vx_ref[0], jnp.ones(vx_ref.shape[1:], BF16)], axis=1)
    vc = jnp.concatenate([vc_ref[0], jnp.ones(vc_ref.shape[1:], BF16)], axis=1)
    nt = (((1,), (1,)), ((), ()))
    for r in range(q_ref.shape[1] // ATTN_SUB):
        rows = slice(r * ATTN_SUB, (r + 1) * ATTN_SUB)
        for g in range(Q_PER_KV):
            cols = slice(g * HEAD_DIM, (g + 1) * HEAD_DIM)
            q = q_ref[0, rows, cols]
            sx = lax.dot_general(q, kx, nt, preferred_element_type=F32)
            sc = lax.dot_general(q, kc, nt, preferred_element_type=F32)
            m = jnp.maximum(jnp.max(sx, axis=-1, keepdims=True), jnp.max(sc, axis=-1, keepdims=True))
            px = jnp.exp2(sx - m)
            pc = jnp.exp2(sc - m)
            o = (jnp.dot(px.astype(BF16), vx, preferred_element_type=F32)
                 + jnp.dot(pc.astype(BF16), vc, preferred_element_type=F32))
            o_ref[0, rows, cols] = (o[:, :HEAD_DIM] * (1.0 / o[:, HEAD_DIM:])).astype(BF16)
    part = pl.program_id(1) * pl.num_programs(2) + pl.program_id(2)
    _pool_mix_rows(p_ref, prev_ref, next_ref, inv_ref, pw_ref, ps_ref, pool_ref,
                   part > 0, part < n_parts - 1)
    for src_ref, dst_ref in zip(refs[:n_cast], refs[n_cast + 2:]):
        dst_ref[...] = src_ref[...].astype(BF16)


def _attention(q, k_x, v_x, k_c, v_c, p, pool_w, pool_scale, cast_f32, tq=1024):
    b, t, _ = q.shape
    s = k_c.shape[1]
    qw = Q_PER_KV * HEAD_DIM
    n_t = t // tq
    n_parts = N_KV_HEADS * n_t
    n_steps = b * n_parts
    pool_rows = t // n_parts
    halo_blocks = pool_rows // POOL_PAD
    assert pool_rows % POOL_PAD == 0 and all(a.shape[0] % (16 * n_steps) == 0 for a in cast_f32)
    part = lambda h, j: h * n_t + j
    kv_map = lambda i, h, j: (i, 0, h)
    pool_map = lambda i, h, j: (i, part(h, j), 0)
    slab = lambda a: pl.BlockSpec((a.shape[0] // n_steps, a.shape[1]),
                                  lambda i, h, j: (i * n_parts + part(h, j), 0))
    out = pl.pallas_call(
        functools.partial(_attn_kernel, n_parts=n_parts),
        grid=(b, N_KV_HEADS, n_t),
        in_specs=[
            pl.BlockSpec((1, tq, qw), lambda i, h, j: (i, j, h)),
            pl.BlockSpec((1, t, HEAD_DIM), kv_map),
            pl.BlockSpec((1, s, HEAD_DIM), kv_map),
            pl.BlockSpec((1, t, HEAD_DIM), kv_map),
            pl.BlockSpec((1, s, HEAD_DIM), kv_map),
            pl.BlockSpec((1, pool_rows, POOL_W), pool_map),
            pl.BlockSpec((1, POOL_PAD, POOL_W),
                         lambda i, h, j: (i, jnp.maximum(part(h, j) * halo_blocks - 1, 0), 0)),
            pl.BlockSpec((1, POOL_PAD, POOL_W),
                         lambda i, h, j: (i, jnp.minimum((part(h, j) + 1) * halo_blocks,
                                                         n_parts * halo_blocks - 1), 0)),
            pl.BlockSpec((N_POOL_GROUPS, pool_rows, HEAD_DIM), lambda i, h, j: (0, part(h, j), 0)),
            pl.BlockSpec((N_POOL_GROUPS, POOL_GROUP_W, POOL_GROUP_W), lambda i, h, j: (0, 0, 0)),
            pl.BlockSpec((1, POOL_W), lambda i, h, j: (0, 0)),
            *[slab(a) for a in cast_f32],
        ],
        out_specs=[pl.BlockSpec((1, tq, qw), lambda i, h, j: (i, j, h)),
                   pl.BlockSpec((1, pool_rows, POOL_W), pool_map),
                   *[slab(a) for a in cast_f32]],
        out_shape=[jax.ShapeDtypeStruct((b, t, ATTN_W), BF16),
                   jax.ShapeDtypeStruct((b, t, POOL_W), BF16),
                   *[jax.ShapeDtypeStruct(a.shape, BF16) for a in cast_f32]],
        compiler_params=_params("arbitrary", "arbitrary", "arbitrary"),
        name="attention",
    )(q, k_x, k_c, v_x, v_c, p, p, p, _pool_inv_counts(t), pool_w, pool_scale, *cast_f32)
    return out[0], out[1], out[2:]


def _out_proj_kernel(x_ref, *refs):
    lhs_refs, (mod_ref, nw_ref), w_refs, (o_ref, h_ref) = refs[:4], refs[4:6], refs[6:-2], refs[-2:]
    batch = pl.program_id(0)
    gate = _mod_row(mod_ref, 2, batch)
    nw = nw_ref[...]
    for r in range(x_ref.shape[1] // OUT_SUB):
        rows = slice(r * OUT_SUB, (r + 1) * OUT_SUB)
        lhs = jnp.concatenate([ref[0, rows, :] for ref in lhs_refs], axis=1)
        mix = jnp.concatenate([jnp.dot(lhs, w_ref[...], preferred_element_type=F32)
                               for w_ref in w_refs], axis=1)
        x1 = x_ref[0, rows, :] + gate * mix
        o_ref[0, rows, :] = x1
        h_ref[0, rows, :] = _norm_modulate(x1, nw, mod_ref, 3, batch).astype(BF16)


def _out_proj(x, pool_mix, attn, mod3, norm_w, w_out_bf, tm=1024):
    b, t, _ = x.shape
    row = lambda i, j: (i, j, 0)
    lhs_window = lambda c: pl.BlockSpec((1, tm, MXU_WINDOW_W), lambda i, j: (i, j, c))
    w_window = lambda c: pl.BlockSpec((MIX_W, MXU_WINDOW_W), lambda i, j: (0, c),
                                      pipeline_mode=pl.Buffered(1))
    n_w = D_MODEL // MXU_WINDOW_W
    return pl.pallas_call(
        _out_proj_kernel,
        grid=(b, t // tm),
        in_specs=[
            pl.BlockSpec((1, tm, D_MODEL), row),
            lhs_window(0), lhs_window(1), lhs_window(0), lhs_window(1),
            pl.BlockSpec((N_MOD, MOD_TILE, D_MODEL), lambda i, j: (0, 0, 0)),
            pl.BlockSpec((1, D_MODEL), lambda i, j: (0, 0)),
            *[w_window(c) for c in range(n_w)],
        ],
        out_specs=[pl.BlockSpec((1, tm, D_MODEL), row), pl.BlockSpec((1, tm, D_MODEL), row)],
        out_shape=[jax.ShapeDtypeStruct(x.shape, F32), jax.ShapeDtypeStruct(x.shape, BF16)],
        compiler_params=_params("arbitrary", "arbitrary"),
        name="out_proj",
    )(x, pool_mix, pool_mix, attn, attn, mod3, norm_w, *([w_out_bf] * n_w))


def _ffn_kernel(h_ref, x_ref, gate_ref, wg_ref, wu_ref, wd_ref, fn_ref, o_hbm, y_ref, act_ref, out_sem,
                *, n_f, tf, tn):
    i, j, s = pl.program_id(0), pl.program_id(1), pl.program_id(2)
    n_i, n_j, n_s = pl.num_programs(0), pl.num_programs(1), pl.num_programs(2)
    tm = y_ref.shape[0]
    chunks = wg_ref.shape[1] // tf
    n_up = -(-n_f // chunks)

    def out_copy(bi, tj):
        return pltpu.make_async_copy(y_ref, o_hbm.at[bi, pl.ds(tj * tm, tm), :], out_sem)

    def up_step(first_chunk, n_chunks):
        h = h_ref[0]
        for c in range(n_chunks * tf // FFN_CHUNK_W):
            cols = slice(c * FFN_CHUNK_W, (c + 1) * FFN_CHUNK_W)
            gate = jnp.dot(h, wg_ref[:, cols], preferred_element_type=F32)
            up = jnp.dot(h, wu_ref[:, cols], preferred_element_type=F32)
            half_gate = 0.5 * gate
            act = (half_gate * (1.0 + jnp.tanh(half_gate)) * up).astype(BF16)
            k, lo = divmod(c * FFN_CHUNK_W, tf)
            act_ref[first_chunk + k, :, lo:lo + FFN_CHUNK_W] = act

    @pl.when(s < n_up - 1)
    def _():
        up_step(s * chunks, chunks)

    @pl.when(s == n_up - 1)
    def _():
        up_step((n_up - 1) * chunks, n_f - (n_up - 1) * chunks)

    @pl.when((s == n_up) & ((i > 0) | (j > 0)))
    def _():
        out_copy(jnp.where(j == 0, i - 1, i), jnp.where(j == 0, n_j - 1, j - 1)).wait()

    @pl.when(s >= n_up)
    def _():
        y = jnp.dot(act_ref[0], wd_ref[0:tf, :], preferred_element_type=F32)
        for f in range(1, n_f):
            y += jnp.dot(act_ref[f], wd_ref[f * tf:(f + 1) * tf, :], preferred_element_type=F32)
        cols = pl.ds(pl.multiple_of((s - n_up) * tn, tn), tn)
        y_ref[:, cols] = x_ref[0] + _mod_row(gate_ref, 0, i) * y

    @pl.when(s == n_s - 1)
    def _():
        y_ref[...] = _rms(y_ref[...]) * fn_ref[...]
        out_copy(i, j).start()

        @pl.when((i == n_i - 1) & (j == n_j - 1))
        def _():
            out_copy(i, j).wait()


def _ffn(h, x1, mod3, wg_bf, wu_bf, wd_bf, final_norm, tm=1024, tf=512, up_chunks=2, tn=512):
    b, t, _ = h.shape
    d_ff = wg_bf.shape[1]
    n_f = d_ff // tf
    n_up = -(-n_f // up_chunks)
    n_n = D_MODEL // tn
    up_blk = lambda i, j, s: (0, jnp.minimum(s, n_up - 1))
    down_blk = lambda i, j, s: (0, jnp.maximum(s - n_up, 0))
    return pl.pallas_call(
        functools.partial(_ffn_kernel, n_f=n_f, tf=tf, tn=tn),
        grid=(b, t // tm, n_up + n_n),
        in_specs=[
            pl.BlockSpec((1, tm, D_MODEL), lambda i, j, s: (i, j, 0)),
            pl.BlockSpec((1, tm, tn), lambda i, j, s: (i, j, jnp.maximum(s - n_up, 0))),
            pl.BlockSpec((1, MOD_TILE, tn), lambda i, j, s: (N_MOD - 1, 0, jnp.maximum(s - n_up, 0))),
            pl.BlockSpec((D_MODEL, up_chunks * tf), up_blk),
            pl.BlockSpec((D_MODEL, up_chunks * tf), up_blk),
            pl.BlockSpec((d_ff, tn), down_blk),
            pl.BlockSpec((1, D_MODEL), lambda i, j, s: (0, 0)),
        ],
        out_specs=pl.BlockSpec(memory_space=pl.ANY),
        out_shape=jax.ShapeDtypeStruct(x1.shape, F32),
        scratch_shapes=[pltpu.VMEM((tm, D_MODEL), F32), pltpu.VMEM((n_f, tm, tf), BF16),
                        pltpu.SemaphoreType.DMA(())],
        compiler_params=_params("arbitrary", "arbitrary", "arbitrary"),
        name="ffn",
    )(h, x1, mod3, wg_bf, wu_bf, wd_bf, final_norm)


def _rope_tables(t):
    n_rows = t // GRID_W
    rows = np.repeat(np.arange(n_rows, dtype=np.float32), GRID_W)
    cols = np.tile(np.arange(GRID_W, dtype=np.float32), n_rows)
    freqs = np.float32(ROPE_THETA) ** (-np.arange(0, AXIS_ROT, 2, dtype=np.float32) / np.float32(AXIS_ROT))
    ang = np.concatenate([rows[:, None] * freqs, cols[:, None] * freqs], axis=-1)
    cos_full = np.repeat(np.cos(ang), 2, axis=-1)
    sin = np.sin(ang)
    sin_signed = np.stack([-sin, sin], axis=-1).reshape(t, HEAD_DIM)
    return jnp.asarray(cos_full, F32), jnp.asarray(sin_signed, F32)


def kernel(x, c, ctx, c_ctx, w_ada, b_ada, norm_mix, norm_ffn, w_in, pool_w, pool_scale,
           q_norm, k_norm, w_out, w_gate, w_up, w_down, final_norm):
    depth = w_ada.shape[0]
    assert depth == 1, "context tokens are only updated between layers; one layer is implemented"
    b, t, _ = x.shape
    cos_full, sin_signed = _rope_tables(t)

    mod3 = _adaln_mod(c, c_ctx, w_ada[0], b_ada)

    k_c, v_c = _ctx_kv(ctx, mod3, norm_mix, w_in[0], k_norm)
    p, q, k_x, v_x = _in_proj(x, mod3, norm_mix, w_in[0], q_norm, k_norm, cos_full, sin_signed)
    attn, pooled, (wo_bf, wg_bf, wu_bf, wd_bf) = _attention(
        q, k_x, v_x, k_c, v_c, p, pool_w[0], pool_scale, (w_out[0], w_gate[0], w_up[0], w_down[0]))
    x1, hf = _out_proj(x, pooled, attn, mod3, norm_ffn, wo_bf)
    return _ffn(hf, x1, mod3, wg_bf, wu_bf, wd_bf, final_norm.reshape(1, D_MODEL))
```

```python
import functools
import math

import jax
import jax.numpy as jnp
import numpy as np
from jax import lax
from jax.experimental import pallas as pl
from jax.experimental.pallas import tpu as pltpu

D_MODEL = 2048
GRID_W = 64
HEAD_DIM = 128
N_HEADS = 8
N_KV_HEADS = 2
Q_PER_KV = N_HEADS // N_KV_HEADS
ATTN_W = N_HEADS * HEAD_DIM
KV_W = N_KV_HEADS * HEAD_DIM
POOL_WINDOWS = (2, 4, 8, 16)
N_POOL_GROUPS = len(POOL_WINDOWS)
POOL_W = D_MODEL // 2
POOL_GROUP_W = POOL_W // N_POOL_GROUPS
MIX_W = POOL_W + ATTN_W
PROJ_W = POOL_W + ATTN_W + 2 * KV_W
ROPE_THETA = 10000.0
AXIS_ROT = HEAD_DIM // 2
EPS = 1e-6
N_MOD = 6
MOD_ROWS = 16
MOD_TILE = 8
POOL_PAD = 8
IN_SUB = 512
OUT_SUB = 512
MXU_WINDOW_W = 512
ATTN_SUB = 256
FFN_CHUNK_W = 256
Q_SCALE = math.log2(math.e) / math.sqrt(HEAD_DIM)

F32 = jnp.float32
BF16 = jnp.bfloat16

VMEM_LIMIT_BYTES = 63 * 1024 * 1024


def _params(*sem):
    return pltpu.CompilerParams(dimension_semantics=sem, vmem_limit_bytes=VMEM_LIMIT_BYTES)


def _rms(x):
    return x * lax.rsqrt(jnp.mean(x * x, axis=-1, keepdims=True) + EPS)


def _adaln_kernel(c_ref, cctx_ref, w_ref, b_ref, o_ref):
    c = jnp.concatenate([c_ref[...], jnp.broadcast_to(cctx_ref[...], (MOD_ROWS - MOD_TILE, D_MODEL))], axis=0)
    a = c * jax.nn.sigmoid(c)
    o_ref[0] = jnp.dot(a, w_ref[...], preferred_element_type=F32) + b_ref[...]


def _adaln_mod(c, c_ctx, w_ada, b_ada, tn=2048):
    assert c.shape[0] == MOD_TILE
    per_chunk = D_MODEL // tn
    return pl.pallas_call(
        _adaln_kernel,
        grid=(N_MOD * per_chunk,),
        in_specs=[
            pl.BlockSpec((MOD_TILE, D_MODEL), lambda j: (0, 0)),
            pl.BlockSpec((1, D_MODEL), lambda j: (0, 0)),
            pl.BlockSpec((D_MODEL, tn), lambda j: (0, j)),
            pl.BlockSpec((1, tn), lambda j: (0, j)),
        ],
        out_specs=pl.BlockSpec((1, MOD_ROWS, tn), lambda j: (j // per_chunk, 0, j % per_chunk)),
        out_shape=jax.ShapeDtypeStruct((N_MOD, MOD_ROWS, D_MODEL), F32),
        compiler_params=_params("arbitrary"),
        name="adaln_mod",
    )(c, c_ctx.reshape(1, D_MODEL), w_ada, b_ada)


def _mod_row(mod_ref, chunk, row):
    return mod_ref[chunk, pl.ds(row, 1), :]


def _norm_modulate(x, nw, mod_ref, shift_idx, row):
    sh = _mod_row(mod_ref, shift_idx, row)
    sc = _mod_row(mod_ref, shift_idx + 1, row)
    return (_rms(x) * nw) * (1.0 + sc) + sh


def _rope(x, cos, sin_signed, even_lane):
    partner = jnp.where(even_lane, pltpu.roll(x, HEAD_DIM - 1, axis=1), pltpu.roll(x, 1, axis=1))
    return x * cos + partner * sin_signed


def _ctx_kv_kernel(x_ref, mod_ref, nw_ref, w_ref, kn_ref, k_ref, v_ref):
    nw = nw_ref[...]
    kn = kn_ref[...]
    for r in range(x_ref.shape[0] // IN_SUB):
        rows = slice(r * IN_SUB, (r + 1) * IN_SUB)
        h = _norm_modulate(x_ref[rows, :], nw, mod_ref, 0, 0)
        kv = jnp.dot(h.astype(BF16), w_ref[...].astype(BF16), preferred_element_type=F32)
        for j in range(N_KV_HEADS):
            cols = slice(j * HEAD_DIM, (j + 1) * HEAD_DIM)
            k_ref[rows, cols] = (_rms(kv[:, cols]) * kn).astype(BF16)
        v_ref[rows, :] = kv[:, KV_W:].astype(BF16)


def _ctx_kv(ctx, mod3, norm_w, w_in, k_norm, tm=1024):
    b, s, _ = ctx.shape
    kv_block = (POOL_W + ATTN_W) // (2 * KV_W)
    k, v = pl.pallas_call(
        _ctx_kv_kernel,
        grid=(b * s // tm,),
        in_specs=[
            pl.BlockSpec((tm, D_MODEL), lambda i: (i, 0)),
            pl.BlockSpec((N_MOD, MOD_TILE, D_MODEL), lambda i: (0, 1, 0)),
            pl.BlockSpec((1, D_MODEL), lambda i: (0, 0)),
            pl.BlockSpec((D_MODEL, 2 * KV_W), lambda i: (0, kv_block)),
            pl.BlockSpec((1, HEAD_DIM), lambda i: (0, 0)),
        ],
        out_specs=[pl.BlockSpec((tm, KV_W), lambda i: (i, 0))] * 2,
        out_shape=[jax.ShapeDtypeStruct((b * s, KV_W), BF16)] * 2,
        compiler_params=_params("arbitrary"),
        name="ctx_kv",
    )(ctx.reshape(b * s, D_MODEL), mod3, norm_w, w_in, k_norm)
    return k.reshape(b, s, KV_W), v.reshape(b, s, KV_W)


def _in_proj_kernel(x_ref, mod_ref, nw_ref, w_ref, qn_ref, kn_ref, cos_ref, sin_ref,
                    p_ref, q_ref, k_ref, v_ref):
    nw = nw_ref[...]
    qn = qn_ref[...] * Q_SCALE
    kn = kn_ref[...]
    even_lane = (lax.broadcasted_iota(jnp.int32, (IN_SUB, HEAD_DIM), 1) % 2) == 0
    for r in range(x_ref.shape[1] // IN_SUB):
        rows = slice(r * IN_SUB, (r + 1) * IN_SUB)
        h = _norm_modulate(x_ref[0, rows, :], nw, mod_ref, 0, pl.program_id(0)).astype(BF16)
        qkv = jnp.dot(h, w_ref[:, POOL_W:].astype(BF16), preferred_element_type=F32)
        cos = cos_ref[rows, :]
        sin = sin_ref[rows, :]
        for j in range(N_HEADS):
            cols = slice(j * HEAD_DIM, (j + 1) * HEAD_DIM)
            q_ref[0, rows, cols] = _rope(_rms(qkv[:, cols]) * qn, cos, sin, even_lane).astype(BF16)
        for j in range(N_KV_HEADS):
            lo = ATTN_W + j * HEAD_DIM
            kh = _rms(qkv[:, lo:lo + HEAD_DIM]) * kn
            k_ref[0, rows, j * HEAD_DIM:(j + 1) * HEAD_DIM] = _rope(kh, cos, sin, even_lane).astype(BF16)
        v_ref[0, rows, :] = qkv[:, ATTN_W + KV_W:].astype(BF16)
        p_ref[0, rows, :] = jnp.dot(h, w_ref[:, :POOL_W].astype(BF16), preferred_element_type=F32)


def _in_proj(x, mod3, norm_w, w_in, q_norm, k_norm, cos_full, sin_signed, tm=1024):
    b, t, _ = x.shape
    row = lambda i, j: (i, j, 0)
    const2 = lambda i, j: (0, 0)
    return pl.pallas_call(
        _in_proj_kernel,
        grid=(b, t // tm),
        in_specs=[
            pl.BlockSpec((1, tm, D_MODEL), row),
            pl.BlockSpec((N_MOD, MOD_TILE, D_MODEL), lambda i, j: (0, 0, 0)),
            pl.BlockSpec((1, D_MODEL), const2),
            pl.BlockSpec((D_MODEL, PROJ_W), const2, pipeline_mode=pl.Buffered(1)),
            pl.BlockSpec((1, HEAD_DIM), const2),
            pl.BlockSpec((1, HEAD_DIM), const2),
            pl.BlockSpec((tm, HEAD_DIM), lambda i, j: (j, 0)),
            pl.BlockSpec((tm, HEAD_DIM), lambda i, j: (j, 0)),
        ],
        out_specs=[
            pl.BlockSpec((1, tm, POOL_W), row),
            pl.BlockSpec((1, tm, ATTN_W), row),
            pl.BlockSpec((1, tm, KV_W), row),
            pl.BlockSpec((1, tm, KV_W), row),
        ],
        out_shape=[
            jax.ShapeDtypeStruct((b, t, POOL_W), F32),
            jax.ShapeDtypeStruct((b, t, ATTN_W), BF16),
            jax.ShapeDtypeStruct((b, t, KV_W), BF16),
            jax.ShapeDtypeStruct((b, t, KV_W), BF16),
        ],
        compiler_params=_params("arbitrary", "arbitrary"),
        name="in_proj",
    )(x, mod3, norm_w, w_in, q_norm, k_norm, cos_full, sin_signed)


def _pool_inv_counts(t):
    pos = np.arange(t)[None, :]
    half = np.array(POOL_WINDOWS)[:, None] // 2
    count = np.minimum(pos + half, t) - np.maximum(pos - half, 0)
    inv = (np.float32(1.0) / count.astype(np.float32))[:, :, None]
    return jnp.asarray(np.broadcast_to(inv, (N_POOL_GROUPS, t, HEAD_DIM)))


def _pool_mix_rows(p_ref, prev_ref, next_ref, inv_ref, w_ref, s_ref, o_ref, has_prev, has_next):
    t = p_ref.shape[1]
    n = t + 2 * POOL_PAD
    for g, win in enumerate(POOL_WINDOWS):
        half = win // 2
        cols = slice(g * POOL_GROUP_W, (g + 1) * POOL_GROUP_W)
        u = p_ref[0, :, cols]
        top = jnp.where(has_prev, prev_ref[0, :, cols], 0.0)
        bottom = jnp.where(has_next, next_ref[0, :, cols], 0.0)
        f = jnp.concatenate([top, u, bottom], axis=0)
        step = 1
        while step < win:
            f = f + pltpu.roll(f, n - step, axis=0)
            step *= 2
        if half != POOL_PAD:
            f = pltpu.roll(f, n - (POOL_PAD - half), axis=0)
        inv = inv_ref[g]
        pooled = f[:t] * jnp.concatenate([inv] * (POOL_GROUP_W // inv.shape[1]), axis=1) - u
        mixed = jnp.dot(pooled.astype(BF16), w_ref[g].astype(BF16), preferred_element_type=F32)
        o_ref[0, :, cols] = (mixed * s_ref[:, cols]).astype(BF16)


def _attn_kernel(q_ref, kx_ref, kc_ref, vx_ref, vc_ref,
                 p_ref, prev_ref, next_ref, inv_ref, pw_ref, ps_ref, *refs, n_parts):
    n_cast = (len(refs) - 2) // 2
    o_ref, pool_ref = refs[n_cast], refs[n_cast + 1]
    kx = kx_ref[0]
    kc = kc_ref[0]
    vx = jnp.concatenate([vx_ref[0], jnp.ones(vx_ref.shape[1:], BF16)], axis=1)
    vc = jnp.concatenate([vc_ref[0], jnp.ones(vc_ref.shape[1:], BF16)], axis=1)
    nt = (((1,), (1,)), ((), ()))
    for r in range(q_ref.shape[1] // ATTN_SUB):
        rows = slice(r * ATTN_SUB, (r + 1) * ATTN_SUB)
        for g in range(Q_PER_KV):
            cols = slice(g * HEAD_DIM, (g + 1) * HEAD_DIM)
            q = q_ref[0, rows, cols]
            sx = lax.dot_general(q, kx, nt, preferred_element_type=F32)
            sc = lax.dot_general(q, kc, nt, preferred_element_type=F32)
            m = jnp.maximum(jnp.max(sx, axis=-1, keepdims=True), jnp.max(sc, axis=-1, keepdims=True))
            px = jnp.exp2(sx - m)
            pc = jnp.exp2(sc - m)
            o = (jnp.dot(px.astype(BF16), vx, preferred_element_type=F32)
                 + jnp.dot(pc.astype(BF16), vc, preferred_element_type=F32))
            o_ref[0, rows, cols] = (o[:, :HEAD_DIM] * (1.0 / o[:, HEAD_DIM:])).astype(BF16)
    part = pl.program_id(1) * pl.num_programs(2) + pl.program_id(2)
    _pool_mix_rows(p_ref, prev_ref, next_ref, inv_ref, pw_ref, ps_ref, pool_ref,
                   part > 0, part < n_parts - 1)
    for src_ref, dst_ref in zip(refs[:n_cast], refs[n_cast + 2:]):
        dst_ref[...] = src_ref[...].astype(BF16)


def _attention(q, k_x, v_x, k_c, v_c, p, pool_w, pool_scale, cast_f32, tq=1024):
    b, t, _ = q.shape
    s = k_c.shape[1]
    qw = Q_PER_KV * HEAD_DIM
    n_t = t // tq
    n_parts = N_KV_HEADS * n_t
    n_steps = b * n_parts
    pool_rows = t // n_parts
    halo_blocks = pool_rows // POOL_PAD
    assert pool_rows % POOL_PAD == 0 and all(a.shape[0] % (16 * n_steps) == 0 for a in cast_f32)
    part = lambda h, j: h * n_t + j
    kv_map = lambda i, h, j: (i, 0, h)
    pool_map = lambda i, h, j: (i, part(h, j), 0)
    slab = lambda a: pl.BlockSpec((a.shape[0] // n_steps, a.shape[1]),
                                  lambda i, h, j: (i * n_parts + part(h, j), 0))
    out = pl.pallas_call(
        functools.partial(_attn_kernel, n_parts=n_parts),
        grid=(b, N_KV_HEADS, n_t),
        in_specs=[
            pl.BlockSpec((1, tq, qw), lambda i, h, j: (i, j, h)),
            pl.BlockSpec((1, t, HEAD_DIM), kv_map),
            pl.BlockSpec((1, s, HEAD_DIM), kv_map),
            pl.BlockSpec((1, t, HEAD_DIM), kv_map),
            pl.BlockSpec((1, s, HEAD_DIM), kv_map),
            pl.BlockSpec((1, pool_rows, POOL_W), pool_map),
            pl.BlockSpec((1, POOL_PAD, POOL_W),
                         lambda i, h, j: (i, jnp.maximum(part(h, j) * halo_blocks - 1, 0), 0)),
            pl.BlockSpec((1, POOL_PAD, POOL_W),
                         lambda i, h, j: (i, jnp.minimum((part(h, j) + 1) * halo_blocks,
                                                         n_parts * halo_blocks - 1), 0)),
            pl.BlockSpec((N_POOL_GROUPS, pool_rows, HEAD_DIM), lambda i, h, j: (0, part(h, j), 0)),
            pl.BlockSpec((N_POOL_GROUPS, POOL_GROUP_W, POOL_GROUP_W), lambda i, h, j: (0, 0, 0)),
            pl.BlockSpec((1, POOL_W), lambda i, h, j: (0, 0)),
            *[slab(a) for a in cast_f32],
        ],
        out_specs=[pl.BlockSpec((1, tq, qw), lambda i, h, j: (i, j, h)),
                   pl.BlockSpec((1, pool_rows, POOL_W), pool_map),
                   *[slab(a) for a in cast_f32]],
        out_shape=[jax.ShapeDtypeStruct((b, t, ATTN_W), BF16),
                   jax.ShapeDtypeStruct((b, t, POOL_W), BF16),
                   *[jax.ShapeDtypeStruct(a.shape, BF16) for a in cast_f32]],
        compiler_params=_params("arbitrary", "arbitrary", "arbitrary"),
        name="attention",
    )(q, k_x, k_c, v_x, v_c, p, p, p, _pool_inv_counts(t), pool_w, pool_scale, *cast_f32)
    return out[0], out[1], out[2:]


def _out_proj_kernel(x_ref, *refs):
    lhs_refs, (mod_ref, nw_ref), w_refs, (o_ref, h_ref) = refs[:4], refs[4:6], refs[6:-2], refs[-2:]
    batch = pl.program_id(0)
    gate = _mod_row(mod_ref, 2, batch)
    nw = nw_ref[...]
    for r in range(x_ref.shape[1] // OUT_SUB):
        rows = slice(r * OUT_SUB, (r + 1) * OUT_SUB)
        lhs = jnp.concatenate([ref[0, rows, :] for ref in lhs_refs], axis=1)
        mix = jnp.concatenate([jnp.dot(lhs, w_ref[...], preferred_element_type=F32)
                               for w_ref in w_refs], axis=1)
        x1 = x_ref[0, rows, :] + gate * mix
        o_ref[0, rows, :] = x1
        h_ref[0, rows, :] = _norm_modulate(x1, nw, mod_ref, 3, batch).astype(BF16)


def _out_proj(x, pool_mix, attn, mod3, norm_w, w_out_bf, tm=1024):
    b, t, _ = x.shape
    row = lambda i, j: (i, j, 0)
    lhs_window = lambda c: pl.BlockSpec((1, tm, MXU_WINDOW_W), lambda i, j: (i, j, c))
    w_window = lambda c: pl.BlockSpec((MIX_W, MXU_WINDOW_W), lambda i, j: (0, c),
                                      pipeline_mode=pl.Buffered(1))
    n_w = D_MODEL // MXU_WINDOW_W
    return pl.pallas_call(
        _out_proj_kernel,
        grid=(b, t // tm),
        in_specs=[
            pl.BlockSpec((1, tm, D_MODEL), row),
            lhs_window(0), lhs_window(1), lhs_window(0), lhs_window(1),
            pl.BlockSpec((N_MOD, MOD_TILE, D_MODEL), lambda i, j: (0, 0, 0)),
            pl.BlockSpec((1, D_MODEL), lambda i, j: (0, 0)),
            *[w_window(c) for c in range(n_w)],
        ],
        out_specs=[pl.BlockSpec((1, tm, D_MODEL), row), pl.BlockSpec((1, tm, D_MODEL), row)],
        out_shape=[jax.ShapeDtypeStruct(x.shape, F32), jax.ShapeDtypeStruct(x.shape, BF16)],
        compiler_params=_params("arbitrary", "arbitrary"),
        name="out_proj",
    )(x, pool_mix, pool_mix, attn, attn, mod3, norm_w, *([w_out_bf] * n_w))


def _ffn_kernel(h_ref, x_ref, gate_ref, wg_ref, wu_ref, wd_ref, fn_ref, o_hbm, y_ref, act_ref, out_sem,
                *, n_f, tf, tn):
    i, j, s = pl.program_id(0), pl.program_id(1), pl.program_id(2)
    n_i, n_j, n_s = pl.num_programs(0), pl.num_programs(1), pl.num_programs(2)
    tm = y_ref.shape[0]
    chunks = wg_ref.shape[1] // tf
    n_up = -(-n_f // chunks)

    def out_copy(bi, tj):
        return pltpu.make_async_copy(y_ref, o_hbm.at[bi, pl.ds(tj * tm, tm), :], out_sem)

    def up_step(first_chunk, n_chunks):
        h = h_ref[0]
        for c in range(n_chunks * tf // FFN_CHUNK_W):
            cols = slice(c * FFN_CHUNK_W, (c + 1) * FFN_CHUNK_W)
            gate = jnp.dot(h, wg_ref[:, cols], preferred_element_type=F32)
            up = jnp.dot(h, wu_ref[:, cols], preferred_element_type=F32)
            half_gate = 0.5 * gate
            act = (half_gate * (1.0 + jnp.tanh(half_gate)) * up).astype(BF16)
            k, lo = divmod(c * FFN_CHUNK_W, tf)
            act_ref[first_chunk + k, :, lo:lo + FFN_CHUNK_W] = act

    @pl.when(s < n_up - 1)
    def _():
        up_step(s * chunks, chunks)

    @pl.when(s == n_up - 1)
    def _():
        up_step((n_up - 1) * chunks, n_f - (n_up - 1) * chunks)

    @pl.when((s == n_up) & ((i > 0) | (j > 0)))
    def _():
        out_copy(jnp.where(j == 0, i - 1, i), jnp.where(j == 0, n_j - 1, j - 1)).wait()

    @pl.when(s >= n_up)
    def _():
        y = jnp.dot(act_ref[0], wd_ref[0:tf, :], preferred_element_type=F32)
        for f in range(1, n_f):
            y += jnp.dot(act_ref[f], wd_ref[f * tf:(f + 1) * tf, :], preferred_element_type=F32)
        cols = pl.ds(pl.multiple_of((s - n_up) * tn, tn), tn)
        y_ref[:, cols] = x_ref[0] + _mod_row(gate_ref, 0, i) * y

    @pl.when(s == n_s - 1)
    def _():
        y_ref[...] = _rms(y_ref[...]) * fn_ref[...]
        out_copy(i, j).start()

        @pl.when((i == n_i - 1) & (j == n_j - 1))
        def _():
            out_copy(i, j).wait()


def _ffn(h, x1, mod3, wg_bf, wu_bf, wd_bf, final_norm, tm=1024, tf=512, up_chunks=2, tn=512):
    b, t, _ = h.shape
    d_ff = wg_bf.shape[1]
    n_f = d_ff // tf
    n_up = -(-n_f // up_chunks)
    n_n = D_MODEL // tn
    up_blk = lambda i, j, s: (0, jnp.minimum(s, n_up - 1))
    down_blk = lambda i, j, s: (0, jnp.maximum(s - n_up, 0))
    return pl.pallas_call(
        functools.partial(_ffn_kernel, n_f=n_f, tf=tf, tn=tn),
        grid=(b, t // tm, n_up + n_n),
        in_specs=[
            pl.BlockSpec((1, tm, D_MODEL), lambda i, j, s: (i, j, 0)),
            pl.BlockSpec((1, tm, tn), lambda i, j, s: (i, j, jnp.maximum(s - n_up, 0))),
            pl.BlockSpec((1, MOD_TILE, tn), lambda i, j, s: (N_MOD - 1, 0, jnp.maximum(s - n_up, 0))),
            pl.BlockSpec((D_MODEL, up_chunks * tf), up_blk),
            pl.BlockSpec((D_MODEL, up_chunks * tf), up_blk),
            pl.BlockSpec((d_ff, tn), down_blk),
            pl.BlockSpec((1, D_MODEL), lambda i, j, s: (0, 0)),
        ],
        out_specs=pl.BlockSpec(memory_space=pl.ANY),
        out_shape=jax.ShapeDtypeStruct(x1.shape, F32),
        scratch_shapes=[pltpu.VMEM((tm, D_MODEL), F32), pltpu.VMEM((n_f, tm, tf), BF16),
                        pltpu.SemaphoreType.DMA(())],
        compiler_params=_params("arbitrary", "arbitrary", "arbitrary"),
        name="ffn",
    )(h, x1, mod3, wg_bf, wu_bf, wd_bf, final_norm)


def _rope_tables(t):
    n_rows = t // GRID_W
    rows = np.repeat(np.arange(n_rows, dtype=np.float32), GRID_W)
    cols = np.tile(np.arange(GRID_W, dtype=np.float32), n_rows)
    freqs = np.float32(ROPE_THETA) ** (-np.arange(0, AXIS_ROT, 2, dtype=np.float32) / np.float32(AXIS_ROT))
    ang = np.concatenate([rows[:, None] * freqs, cols[:, None] * freqs], axis=-1)
    cos_full = np.repeat(np.cos(ang), 2, axis=-1)
    sin = np.sin(ang)
    sin_signed = np.stack([-sin, sin], axis=-1).reshape(t, HEAD_DIM)
    return jnp.asarray(cos_full, F32), jnp.asarray(sin_signed, F32)


def kernel(x, c, ctx, c_ctx, w_ada, b_ada, norm_mix, norm_ffn, w_in, pool_w, pool_scale,
           q_norm, k_norm, w_out, w_gate, w_up, w_down, final_norm):
    depth = w_ada.shape[0]
    assert depth == 1, "context tokens are only updated between layers; one layer is implemented"
    b, t, _ = x.shape
    cos_full, sin_signed = _rope_tables(t)

    mod3 = _adaln_mod(c, c_ctx, w_ada[0], b_ada)

    k_c, v_c = _ctx_kv(ctx, mod3, norm_mix, w_in[0], k_norm)
    p, q, k_x, v_x = _in_proj(x, mod3, norm_mix, w_in[0], q_norm, k_norm, cos_full, sin_signed)
    attn, pooled, (wo_bf, wg_bf, wu_bf, wd_bf) = _attention(
        q, k_x, v_x, k_c, v_c, p, pool_w[0], pool_scale, (w_out[0], w_gate[0], w_up[0], w_down[0]))
    x1, hf = _out_proj(x, pooled, attn, mod3, norm_ffn, wo_bf)
    return _ffn(hf, x1, mod3, wg_bf, wu_bf, wd_bf, final_norm.reshape(1, D_MODEL))
```

```python
import functools
import math

import jax
import jax.numpy as jnp
import numpy as np
from jax import lax
from jax.experimental import pallas as pl
from jax.experimental.pallas import tpu as pltpu

D_MODEL = 2048
GRID_W = 64
HEAD_DIM = 128
N_HEADS = 8
N_KV_HEADS = 2
Q_PER_KV = N_HEADS // N_KV_HEADS
ATTN_W = N_HEADS * HEAD_DIM
KV_W = N_KV_HEADS * HEAD_DIM
POOL_WINDOWS = (2, 4, 8, 16)
N_POOL_GROUPS = len(POOL_WINDOWS)
POOL_W = D_MODEL // 2
POOL_GROUP_W = POOL_W // N_POOL_GROUPS
MIX_W = POOL_W + ATTN_W
PROJ_W = POOL_W + ATTN_W + 2 * KV_W
ROPE_THETA = 10000.0
AXIS_ROT = HEAD_DIM // 2
EPS = 1e-6
N_MOD = 6
MOD_ROWS = 16
MOD_TILE = 8
POOL_PAD = 8
IN_SUB = 512
OUT_SUB = 512
MXU_WINDOW_W = 512
ATTN_SUB = 256
FFN_CHUNK_W = 256
Q_SCALE = math.log2(math.e) / math.sqrt(HEAD_DIM)

F32 = jnp.float32
BF16 = jnp.bfloat16

VMEM_LIMIT_BYTES = 63 * 1024 * 1024


def _params(*sem):
    return pltpu.CompilerParams(dimension_semantics=sem, vmem_limit_bytes=VMEM_LIMIT_BYTES)


def _rms(x):
    return x * lax.rsqrt(jnp.mean(x * x, axis=-1, keepdims=True) + EPS)


def _adaln_kernel(c_ref, cctx_ref, w_ref, b_ref, o_ref):
    c = jnp.concatenate([c_ref[...], jnp.broadcast_to(cctx_ref[...], (MOD_ROWS - MOD_TILE, D_MODEL))], axis=0)
    a = c * jax.nn.sigmoid(c)
    o_ref[0] = jnp.dot(a, w_ref[...], preferred_element_type=F32) + b_ref[...]


def _adaln_mod(c, c_ctx, w_ada, b_ada, tn=1024):
    assert c.shape[0] == MOD_TILE
    per_chunk = D_MODEL // tn
    return pl.pallas_call(
        _adaln_kernel,
        grid=(N_MOD * per_chunk,),
        in_specs=[
            pl.BlockSpec((MOD_TILE, D_MODEL), lambda j: (0, 0)),
            pl.BlockSpec((1, D_MODEL), lambda j: (0, 0)),
            pl.BlockSpec((D_MODEL, tn), lambda j: (0, j)),
            pl.BlockSpec((1, tn), lambda j: (0, j)),
        ],
        out_specs=pl.BlockSpec((1, MOD_ROWS, tn), lambda j: (j // per_chunk, 0, j % per_chunk)),
        out_shape=jax.ShapeDtypeStruct((N_MOD, MOD_ROWS, D_MODEL), F32),
        compiler_params=_params("arbitrary"),
        name="adaln_mod",
    )(c, c_ctx.reshape(1, D_MODEL), w_ada, b_ada)


def _mod_row(mod_ref, chunk, row):
    return mod_ref[chunk, pl.ds(row, 1), :]


def _norm_modulate(x, nw, mod_ref, shift_idx, row):
    sh = _mod_row(mod_ref, shift_idx, row)
    sc = _mod_row(mod_ref, shift_idx + 1, row)
    return (_rms(x) * nw) * (1.0 + sc) + sh


def _rope(x, cos, sin_signed, even_lane):
    partner = jnp.where(even_lane, pltpu.roll(x, HEAD_DIM - 1, axis=1), pltpu.roll(x, 1, axis=1))
    return x * cos + partner * sin_signed


def _ctx_kv_kernel(x_ref, mod_ref, nw_ref, w_ref, kn_ref, k_ref, v_ref):
    nw = nw_ref[...]
    kn = kn_ref[...]
    for r in range(x_ref.shape[0] // IN_SUB):
        rows = slice(r * IN_SUB, (r + 1) * IN_SUB)
        h = _norm_modulate(x_ref[rows, :], nw, mod_ref, 0, 0)
        kv = jnp.dot(h.astype(BF16), w_ref[...].astype(BF16), preferred_element_type=F32)
        for j in range(N_KV_HEADS):
            cols = slice(j * HEAD_DIM, (j + 1) * HEAD_DIM)
            k_ref[rows, cols] = (_rms(kv[:, cols]) * kn).astype(BF16)
        v_ref[rows, :] = kv[:, KV_W:].astype(BF16)


def _ctx_kv(ctx, mod3, norm_w, w_in, k_norm, tm=1024):
    b, s, _ = ctx.shape
    kv_block = (POOL_W + ATTN_W) // (2 * KV_W)
    k, v = pl.pallas_call(
        _ctx_kv_kernel,
        grid=(b * s // tm,),
        in_specs=[
            pl.BlockSpec((tm, D_MODEL), lambda i: (i, 0)),
            pl.BlockSpec((N_MOD, MOD_TILE, D_MODEL), lambda i: (0, 1, 0)),
            pl.BlockSpec((1, D_MODEL), lambda i: (0, 0)),
            pl.BlockSpec((D_MODEL, 2 * KV_W), lambda i: (0, kv_block)),
            pl.BlockSpec((1, HEAD_DIM), lambda i: (0, 0)),
        ],
        out_specs=[pl.BlockSpec((tm, KV_W), lambda i: (i, 0))] * 2,
        out_shape=[jax.ShapeDtypeStruct((b * s, KV_W), BF16)] * 2,
        compiler_params=_params("arbitrary"),
        name="ctx_kv",
    )(ctx.reshape(b * s, D_MODEL), mod3, norm_w, w_in, k_norm)
    return k.reshape(b, s, KV_W), v.reshape(b, s, KV_W)


def _in_proj_kernel(x_ref, mod_ref, nw_ref, w_ref, qn_ref, kn_ref, cos_ref, sin_ref,
                    p_ref, q_ref, k_ref, v_ref):
    nw = nw_ref[...]
    qn = qn_ref[...] * Q_SCALE
    kn = kn_ref[...]
    even_lane = (lax.broadcasted_iota(jnp.int32, (IN_SUB, HEAD_DIM), 1) % 2) == 0
    for r in range(x_ref.shape[1] // IN_SUB):
        rows = slice(r * IN_SUB, (r + 1) * IN_SUB)
        h = _norm_modulate(x_ref[0, rows, :], nw, mod_ref, 0, pl.program_id(0)).astype(BF16)
        qkv = jnp.dot(h, w_ref[:, POOL_W:].astype(BF16), preferred_element_type=F32)
        cos = cos_ref[rows, :]
        sin = sin_ref[rows, :]
        for j in range(N_HEADS):
            cols = slice(j * HEAD_DIM, (j + 1) * HEAD_DIM)
            q_ref[0, rows, cols] = _rope(_rms(qkv[:, cols]) * qn, cos, sin, even_lane).astype(BF16)
        for j in range(N_KV_HEADS):
            lo = ATTN_W + j * HEAD_DIM
            kh = _rms(qkv[:, lo:lo + HEAD_DIM]) * kn
            k_ref[0, rows, j * HEAD_DIM:(j + 1) * HEAD_DIM] = _rope(kh, cos, sin, even_lane).astype(BF16)
        v_ref[0, rows, :] = qkv[:, ATTN_W + KV_W:].astype(BF16)
        p_ref[0, rows, :] = jnp.dot(h, w_ref[:, :POOL_W].astype(BF16), preferred_element_type=F32)


def _in_proj(x, mod3, norm_w, w_in, q_norm, k_norm, cos_full, sin_signed, tm=1024):
    b, t, _ = x.shape
    row = lambda i, j: (i, j, 0)
    const2 = lambda i, j: (0, 0)
    return pl.pallas_call(
        _in_proj_kernel,
        grid=(b, t // tm),
        in_specs=[
            pl.BlockSpec((1, tm, D_MODEL), row),
            pl.BlockSpec((N_MOD, MOD_TILE, D_MODEL), lambda i, j: (0, 0, 0)),
            pl.BlockSpec((1, D_MODEL), const2),
            pl.BlockSpec((D_MODEL, PROJ_W), const2, pipeline_mode=pl.Buffered(1)),
            pl.BlockSpec((1, HEAD_DIM), const2),
            pl.BlockSpec((1, HEAD_DIM), const2),
            pl.BlockSpec((tm, HEAD_DIM), lambda i, j: (j, 0)),
            pl.BlockSpec((tm, HEAD_DIM), lambda i, j: (j, 0)),
        ],
        out_specs=[
            pl.BlockSpec((1, tm, POOL_W), row),
            pl.BlockSpec((1, tm, ATTN_W), row),
            pl.BlockSpec((1, tm, KV_W), row),
            pl.BlockSpec((1, tm, KV_W), row),
        ],
        out_shape=[
            jax.ShapeDtypeStruct((b, t, POOL_W), F32),
            jax.ShapeDtypeStruct((b, t, ATTN_W), BF16),
            jax.ShapeDtypeStruct((b, t, KV_W), BF16),
            jax.ShapeDtypeStruct((b, t, KV_W), BF16),
        ],
        compiler_params=_params("arbitrary", "arbitrary"),
        name="in_proj",
    )(x, mod3, norm_w, w_in, q_norm, k_norm, cos_full, sin_signed)


def _pool_mix_rows(p_ref, prev_ref, next_ref, w_ref, s_ref, o_ref, has_prev, has_next):
    t = p_ref.shape[1]
    n = t + 2 * POOL_PAD
    edge_row = lax.broadcasted_iota(jnp.int32, (POOL_PAD, POOL_GROUP_W), 0)
    for g, win in enumerate(POOL_WINDOWS):
        half = win // 2
        cols = slice(g * POOL_GROUP_W, (g + 1) * POOL_GROUP_W)
        u = p_ref[0, :, cols]
        top = jnp.where(has_prev, prev_ref[0, :, cols], 0.0)
        bottom = jnp.where(has_next, next_ref[0, :, cols], 0.0)
        f = jnp.concatenate([top, u, bottom], axis=0)
        step = 1
        while step < win:
            f = f + pltpu.roll(f, n - step, axis=0)
            step *= 2
        if half != POOL_PAD:
            f = pltpu.roll(f, n - (POOL_PAD - half), axis=0)
        first = jnp.where(edge_row < half, edge_row + half, win).astype(F32)
        last = jnp.where(edge_row >= POOL_PAD - half, POOL_PAD - edge_row + half, win).astype(F32)
        inv_len = jnp.concatenate([
            jnp.where(has_prev, 1.0 / win, 1.0 / first),
            jnp.full((t - 2 * POOL_PAD, POOL_GROUP_W), 1.0 / win, F32),
            jnp.where(has_next, 1.0 / win, 1.0 / last)], axis=0)
        pooled = f[:t] * inv_len - u
        mixed = jnp.dot(pooled.astype(BF16), w_ref[g].astype(BF16), preferred_element_type=F32)
        o_ref[0, :, cols] = (mixed * s_ref[:, cols]).astype(BF16)


def _attn_kernel(q_ref, kx_ref, kc_ref, vx_ref, vc_ref,
                 p_ref, prev_ref, next_ref, pw_ref, ps_ref, *refs, n_parts):
    n_cast = (len(refs) - 2) // 2
    o_ref, pool_ref = refs[n_cast], refs[n_cast + 1]
    kx = kx_ref[0]
    kc = kc_ref[0]
    vx = jnp.concatenate([vx_ref[0], jnp.ones(vx_ref.shape[1:], BF16)], axis=1)
    vc = jnp.concatenate([vc_ref[0], jnp.ones(vc_ref.shape[1:], BF16)], axis=1)
    nt = (((1,), (1,)), ((), ()))
    for r in range(q_ref.shape[1] // ATTN_SUB):
        rows = slice(r * ATTN_SUB, (r + 1) * ATTN_SUB)
        for g in range(Q_PER_KV):
            cols = slice(g * HEAD_DIM, (g + 1) * HEAD_DIM)
            q = q_ref[0, rows, cols]
            sx = lax.dot_general(q, kx, nt, preferred_element_type=F32)
            sc = lax.dot_general(q, kc, nt, preferred_element_type=F32)
            m = jnp.maximum(jnp.max(sx, axis=-1, keepdims=True), jnp.max(sc, axis=-1, keepdims=True))
            px = jnp.exp2(sx - m)
            pc = jnp.exp2(sc - m)
            o = (jnp.dot(px.astype(BF16), vx, preferred_element_type=F32)
                 + jnp.dot(pc.astype(BF16), vc, preferred_element_type=F32))
            o_ref[0, rows, cols] = (o[:, :HEAD_DIM] * (1.0 / o[:, HEAD_DIM:])).astype(BF16)
    part = pl.program_id(1) * pl.num_programs(2) + pl.program_id(2)
    _pool_mix_rows(p_ref, prev_ref, next_ref, pw_ref, ps_ref, pool_ref,
                   part > 0, part < n_parts - 1)
    for src_ref, dst_ref in zip(refs[:n_cast], refs[n_cast + 2:]):
        dst_ref[...] = src_ref[...].astype(BF16)


def _attention(q, k_x, v_x, k_c, v_c, p, pool_w, pool_scale, cast_f32, tq=1024):
    b, t, _ = q.shape
    s = k_c.shape[1]
    qw = Q_PER_KV * HEAD_DIM
    n_t = t // tq
    n_parts = N_KV_HEADS * n_t
    n_steps = b * n_parts
    pool_rows = t // n_parts
    halo_blocks = pool_rows // POOL_PAD
    assert pool_rows % POOL_PAD == 0 and all(a.shape[0] % (16 * n_steps) == 0 for a in cast_f32)
    part = lambda h, j: h * n_t + j
    kv_map = lambda i, h, j: (i, 0, h)
    pool_map = lambda i, h, j: (i, part(h, j), 0)
    slab = lambda a: pl.BlockSpec((a.shape[0] // n_steps, a.shape[1]),
                                  lambda i, h, j: (i * n_parts + part(h, j), 0))
    out = pl.pallas_call(
        functools.partial(_attn_kernel, n_parts=n_parts),
        grid=(b, N_KV_HEADS, n_t),
        in_specs=[
            pl.BlockSpec((1, tq, qw), lambda i, h, j: (i, j, h)),
            pl.BlockSpec((1, t, HEAD_DIM), kv_map),
            pl.BlockSpec((1, s, HEAD_DIM), kv_map),
            pl.BlockSpec((1, t, HEAD_DIM), kv_map),
            pl.BlockSpec((1, s, HEAD_DIM), kv_map),
            pl.BlockSpec((1, pool_rows, POOL_W), pool_map),
            pl.BlockSpec((1, POOL_PAD, POOL_W),
                         lambda i, h, j: (i, jnp.maximum(part(h, j) * halo_blocks - 1, 0), 0)),
            pl.BlockSpec((1, POOL_PAD, POOL_W),
                         lambda i, h, j: (i, jnp.minimum((part(h, j) + 1) * halo_blocks,
                                                         n_parts * halo_blocks - 1), 0)),
            pl.BlockSpec((N_POOL_GROUPS, POOL_GROUP_W, POOL_GROUP_W), lambda i, h, j: (0, 0, 0)),
            pl.BlockSpec((1, POOL_W), lambda i, h, j: (0, 0)),
            *[slab(a) for a in cast_f32],
        ],
        out_specs=[pl.BlockSpec((1, tq, qw), lambda i, h, j: (i, j, h)),
                   pl.BlockSpec((1, pool_rows, POOL_W), pool_map),
                   *[slab(a) for a in cast_f32]],
        out_shape=[jax.ShapeDtypeStruct((b, t, ATTN_W), BF16),
                   jax.ShapeDtypeStruct((b, t, POOL_W), BF16),
                   *[jax.ShapeDtypeStruct(a.shape, BF16) for a in cast_f32]],
        compiler_params=_params("arbitrary", "arbitrary", "arbitrary"),
        name="attention",
    )(q, k_x, k_c, v_x, v_c, p, p, p, pool_w, pool_scale, *cast_f32)
    return out[0], out[1], out[2:]


def _out_proj_kernel(x_ref, *refs):
    lhs_refs, (mod_ref, nw_ref), w_refs, (o_ref, h_ref) = refs[:4], refs[4:6], refs[6:-2], refs[-2:]
    batch = pl.program_id(0)
    gate = _mod_row(mod_ref, 2, batch)
    nw = nw_ref[...]
    for r in range(x_ref.shape[1] // OUT_SUB):
        rows = slice(r * OUT_SUB, (r + 1) * OUT_SUB)
        lhs = jnp.concatenate([ref[0, rows, :] for ref in lhs_refs], axis=1)
        mix = jnp.concatenate([jnp.dot(lhs, w_ref[...], preferred_element_type=F32)
                               for w_ref in w_refs], axis=1)
        x1 = x_ref[0, rows, :] + gate * mix
        o_ref[0, rows, :] = x1
        h_ref[0, rows, :] = _norm_modulate(x1, nw, mod_ref, 3, batch).astype(BF16)


def _out_proj(x, pool_mix, attn, mod3, norm_w, w_out_bf, tm=1024):
    b, t, _ = x.shape
    row = lambda i, j: (i, j, 0)
    lhs_window = lambda c: pl.BlockSpec((1, tm, MXU_WINDOW_W), lambda i, j: (i, j, c))
    w_window = lambda c: pl.BlockSpec((MIX_W, MXU_WINDOW_W), lambda i, j: (0, c),
                                      pipeline_mode=pl.Buffered(1))
    n_w = D_MODEL // MXU_WINDOW_W
    return pl.pallas_call(
        _out_proj_kernel,
        grid=(b, t // tm),
        in_specs=[
            pl.BlockSpec((1, tm, D_MODEL), row),
            lhs_window(0), lhs_window(1), lhs_window(0), lhs_window(1),
            pl.BlockSpec((N_MOD, MOD_TILE, D_MODEL), lambda i, j: (0, 0, 0)),
            pl.BlockSpec((1, D_MODEL), lambda i, j: (0, 0)),
            *[w_window(c) for c in range(n_w)],
        ],
        out_specs=[pl.BlockSpec((1, tm, D_MODEL), row), pl.BlockSpec((1, tm, D_MODEL), row)],
        out_shape=[jax.ShapeDtypeStruct(x.shape, F32), jax.ShapeDtypeStruct(x.shape, BF16)],
        compiler_params=_params("arbitrary", "arbitrary"),
        name="out_proj",
    )(x, pool_mix, pool_mix, attn, attn, mod3, norm_w, *([w_out_bf] * n_w))


def _ffn_kernel(h_ref, x_ref, gate_ref, wg_ref, wu_ref, wd_ref, fn_ref, o_hbm, y_ref, act_ref, out_sem,
                *, n_f, tf, tn):
    i, j, s = pl.program_id(0), pl.program_id(1), pl.program_id(2)
    n_i, n_j, n_s = pl.num_programs(0), pl.num_programs(1), pl.num_programs(2)
    tm = y_ref.shape[0]
    chunks = wg_ref.shape[1] // tf
    n_up = -(-n_f // chunks)

    def out_copy(bi, tj):
        return pltpu.make_async_copy(y_ref, o_hbm.at[bi, pl.ds(tj * tm, tm), :], out_sem)

    def up_step(first_chunk, n_chunks):
        h = h_ref[0]
        for c in range(n_chunks * tf // FFN_CHUNK_W):
            cols = slice(c * FFN_CHUNK_W, (c + 1) * FFN_CHUNK_W)
            gate = jnp.dot(h, wg_ref[:, cols], preferred_element_type=F32)
            up = jnp.dot(h, wu_ref[:, cols], preferred_element_type=F32)
            half_gate = 0.5 * gate
            act = (half_gate * (1.0 + jnp.tanh(half_gate)) * up).astype(BF16)
            k, lo = divmod(c * FFN_CHUNK_W, tf)
            act_ref[first_chunk + k, :, lo:lo + FFN_CHUNK_W] = act

    @pl.when(s < n_up - 1)
    def _():
        up_step(s * chunks, chunks)

    @pl.when(s == n_up - 1)
    def _():
        up_step((n_up - 1) * chunks, n_f - (n_up - 1) * chunks)

    @pl.when((s == n_up) & ((i > 0) | (j > 0)))
    def _():
        out_copy(jnp.where(j == 0, i - 1, i), jnp.where(j == 0, n_j - 1, j - 1)).wait()

    @pl.when(s >= n_up)
    def _():
        y = jnp.dot(act_ref[0], wd_ref[0:tf, :], preferred_element_type=F32)
        for f in range(1, n_f):
            y += jnp.dot(act_ref[f], wd_ref[f * tf:(f + 1) * tf, :], preferred_element_type=F32)
        cols = pl.ds(pl.multiple_of((s - n_up) * tn, tn), tn)
        y_ref[:, cols] = x_ref[0] + _mod_row(gate_ref, 0, i) * y

    @pl.when(s == n_s - 1)
    def _():
        y_ref[...] = _rms(y_ref[...]) * fn_ref[...]
        out_copy(i, j).start()

        @pl.when((i == n_i - 1) & (j == n_j - 1))
        def _():
            out_copy(i, j).wait()


def _ffn(h, x1, mod3, wg_bf, wu_bf, wd_bf, final_norm, tm=1024, tf=512, up_chunks=2, tn=512):
    b, t, _ = h.shape
    d_ff = wg_bf.shape[1]
    n_f = d_ff // tf
    n_up = -(-n_f // up_chunks)
    n_n = D_MODEL // tn
    up_blk = lambda i, j, s: (0, jnp.minimum(s, n_up - 1))
    down_blk = lambda i, j, s: (0, jnp.maximum(s - n_up, 0))
    return pl.pallas_call(
        functools.partial(_ffn_kernel, n_f=n_f, tf=tf, tn=tn),
        grid=(b, t // tm, n_up + n_n),
        in_specs=[
            pl.BlockSpec((1, tm, D_MODEL), lambda i, j, s: (i, j, 0)),
            pl.BlockSpec((1, tm, tn), lambda i, j, s: (i, j, jnp.maximum(s - n_up, 0))),
            pl.BlockSpec((1, MOD_TILE, tn), lambda i, j, s: (N_MOD - 1, 0, jnp.maximum(s - n_up, 0))),
            pl.BlockSpec((D_MODEL, up_chunks * tf), up_blk),
            pl.BlockSpec((D_MODEL, up_chunks * tf), up_blk),
            pl.BlockSpec((d_ff, tn), down_blk),
            pl.BlockSpec((1, D_MODEL), lambda i, j, s: (0, 0)),
        ],
        out_specs=pl.BlockSpec(memory_space=pl.ANY),
        out_shape=jax.ShapeDtypeStruct(x1.shape, F32),
        scratch_shapes=[pltpu.VMEM((tm, D_MODEL), F32), pltpu.VMEM((n_f, tm, tf), BF16),
                        pltpu.SemaphoreType.DMA(())],
        compiler_params=_params("arbitrary", "arbitrary", "arbitrary"),
        name="ffn",
    )(h, x1, mod3, wg_bf, wu_bf, wd_bf, final_norm)


def _rope_tables(t):
    n_rows = t // GRID_W
    rows = np.repeat(np.arange(n_rows, dtype=np.float32), GRID_W)
    cols = np.tile(np.arange(GRID_W, dtype=np.float32), n_rows)
    freqs = np.float32(ROPE_THETA) ** (-np.arange(0, AXIS_ROT, 2, dtype=np.float32) / np.float32(AXIS_ROT))
    ang = np.concatenate([rows[:, None] * freqs, cols[:, None] * freqs], axis=-1)
    cos_full = np.repeat(np.cos(ang), 2, axis=-1)
    sin = np.sin(ang)
    sin_signed = np.stack([-sin, sin], axis=-1).reshape(t, HEAD_DIM)
    return jnp.asarray(cos_full, F32), jnp.asarray(sin_signed, F32)


def kernel(x, c, ctx, c_ctx, w_ada, b_ada, norm_mix, norm_ffn, w_in, pool_w, pool_scale,
           q_norm, k_norm, w_out, w_gate, w_up, w_down, final_norm):
    depth = w_ada.shape[0]
    assert depth == 1, "context tokens are only updated between layers; one layer is implemented"
    b, t, _ = x.shape
    cos_full, sin_signed = _rope_tables(t)

    mod3 = _adaln_mod(c, c_ctx, w_ada[0], b_ada)

    k_c, v_c = _ctx_kv(ctx, mod3, norm_mix, w_in[0], k_norm)
    p, q, k_x, v_x = _in_proj(x, mod3, norm_mix, w_in[0], q_norm, k_norm, cos_full, sin_signed)
    attn, pooled, (wo_bf, wg_bf, wu_bf, wd_bf) = _attention(
        q, k_x, v_x, k_c, v_c, p, pool_w[0], pool_scale, (w_out[0], w_gate[0], w_up[0], w_down[0]))
    x1, hf = _out_proj(x, pooled, attn, mod3, norm_ffn, wo_bf)
    return _ffn(hf, x1, mod3, wg_bf, wu_bf, wd_bf, final_norm.reshape(1, D_MODEL))
```

```python
import functools
import math

import jax
import jax.numpy as jnp
import numpy as np
from jax import lax
from jax.experimental import pallas as pl
from jax.experimental.pallas import tpu as pltpu

D_MODEL = 2048
GRID_W = 64
HEAD_DIM = 128
N_HEADS = 8
N_KV_HEADS = 2
Q_PER_KV = N_HEADS // N_KV_HEADS
ATTN_W = N_HEADS * HEAD_DIM
KV_W = N_KV_HEADS * HEAD_DIM
POOL_WINDOWS = (2, 4, 8, 16)
N_POOL_GROUPS = len(POOL_WINDOWS)
POOL_W = D_MODEL // 2
POOL_GROUP_W = POOL_W // N_POOL_GROUPS
MIX_W = POOL_W + ATTN_W
PROJ_W = POOL_W + ATTN_W + 2 * KV_W
ROPE_THETA = 10000.0
AXIS_ROT = HEAD_DIM // 2
EPS = 1e-6
N_MOD = 6
MOD_ROWS = 16
MOD_TILE = 8
POOL_PAD = 8
IN_SUB = 512
OUT_SUB = 512
MXU_WINDOW_W = 512
ATTN_SUB = 256
FFN_CHUNK_W = 256
Q_SCALE = math.log2(math.e) / math.sqrt(HEAD_DIM)

F32 = jnp.float32
BF16 = jnp.bfloat16

VMEM_LIMIT_BYTES = 63 * 1024 * 1024


def _params(*sem):
    return pltpu.CompilerParams(dimension_semantics=sem, vmem_limit_bytes=VMEM_LIMIT_BYTES)


def _rms(x):
    return x * lax.rsqrt(jnp.mean(x * x, axis=-1, keepdims=True) + EPS)


def _adaln_kernel(c_ref, cctx_ref, w_ref, b_ref, o_ref):
    c = jnp.concatenate([c_ref[...], jnp.broadcast_to(cctx_ref[...], (MOD_ROWS - MOD_TILE, D_MODEL))], axis=0)
    a = c * jax.nn.sigmoid(c)
    o_ref[0] = jnp.dot(a, w_ref[...], preferred_element_type=F32) + b_ref[...]


def _adaln_mod(c, c_ctx, w_ada, b_ada, tn=1024):
    assert c.shape[0] == MOD_TILE
    per_chunk = D_MODEL // tn
    return pl.pallas_call(
        _adaln_kernel,
        grid=(N_MOD * per_chunk,),
        in_specs=[
            pl.BlockSpec((MOD_TILE, D_MODEL), lambda j: (0, 0)),
            pl.BlockSpec((1, D_MODEL), lambda j: (0, 0)),
            pl.BlockSpec((D_MODEL, tn), lambda j: (0, j)),
            pl.BlockSpec((1, tn), lambda j: (0, j)),
        ],
        out_specs=pl.BlockSpec((1, MOD_ROWS, tn), lambda j: (j // per_chunk, 0, j % per_chunk)),
        out_shape=jax.ShapeDtypeStruct((N_MOD, MOD_ROWS, D_MODEL), F32),
        compiler_params=_params("arbitrary"),
        name="adaln_mod",
    )(c, c_ctx.reshape(1, D_MODEL), w_ada, b_ada)


def _mod_row(mod_ref, chunk, row):
    return mod_ref[chunk, pl.ds(row, 1), :]


def _norm_modulate(x, nw, mod_ref, shift_idx, row):
    sh = _mod_row(mod_ref, shift_idx, row)
    sc = _mod_row(mod_ref, shift_idx + 1, row)
    return (_rms(x) * nw) * (1.0 + sc) + sh


def _rope(x, cos, sin_signed, even_lane):
    partner = jnp.where(even_lane, pltpu.roll(x, HEAD_DIM - 1, axis=1), pltpu.roll(x, 1, axis=1))
    return x * cos + partner * sin_signed


def _ctx_kv_kernel(x_ref, mod_ref, nw_ref, w_ref, kn_ref, k_ref, v_ref):
    nw = nw_ref[...]
    kn = kn_ref[...]
    for r in range(x_ref.shape[0] // IN_SUB):
        rows = slice(r * IN_SUB, (r + 1) * IN_SUB)
        h = _norm_modulate(x_ref[rows, :], nw, mod_ref, 0, 0)
        kv = jnp.dot(h.astype(BF16), w_ref[...].astype(BF16), preferred_element_type=F32)
        for j in range(N_KV_HEADS):
            cols = slice(j * HEAD_DIM, (j + 1) * HEAD_DIM)
            k_ref[rows, cols] = (_rms(kv[:, cols]) * kn).astype(BF16)
        v_ref[rows, :] = kv[:, KV_W:].astype(BF16)


def _ctx_kv(ctx, mod3, norm_w, w_in, k_norm, tm=1024):
    b, s, _ = ctx.shape
    kv_block = (POOL_W + ATTN_W) // (2 * KV_W)
    k, v = pl.pallas_call(
        _ctx_kv_kernel,
        grid=(b * s // tm,),
        in_specs=[
            pl.BlockSpec((tm, D_MODEL), lambda i: (i, 0)),
            pl.BlockSpec((N_MOD, MOD_TILE, D_MODEL), lambda i: (0, 1, 0)),
            pl.BlockSpec((1, D_MODEL), lambda i: (0, 0)),
            pl.BlockSpec((D_MODEL, 2 * KV_W), lambda i: (0, kv_block)),
            pl.BlockSpec((1, HEAD_DIM), lambda i: (0, 0)),
        ],
        out_specs=[pl.BlockSpec((tm, KV_W), lambda i: (i, 0))] * 2,
        out_shape=[jax.ShapeDtypeStruct((b * s, KV_W), BF16)] * 2,
        compiler_params=_params("arbitrary"),
        name="ctx_kv",
    )(ctx.reshape(b * s, D_MODEL), mod3, norm_w, w_in, k_norm)
    return k.reshape(b, s, KV_W), v.reshape(b, s, KV_W)


def _in_proj_kernel(x_ref, mod_ref, nw_ref, w_ref, qn_ref, kn_ref, cos_ref, sin_ref,
                    p_ref, q_ref, k_ref, v_ref):
    nw = nw_ref[...]
    qn = qn_ref[...] * Q_SCALE
    kn = kn_ref[...]
    even_lane = (lax.broadcasted_iota(jnp.int32, (IN_SUB, HEAD_DIM), 1) % 2) == 0
    for r in range(x_ref.shape[1] // IN_SUB):
        rows = slice(r * IN_SUB, (r + 1) * IN_SUB)
        h = _norm_modulate(x_ref[0, rows, :], nw, mod_ref, 0, pl.program_id(0)).astype(BF16)
        qkv = jnp.dot(h, w_ref[:, POOL_W:].astype(BF16), preferred_element_type=F32)
        cos = cos_ref[rows, :]
        sin = sin_ref[rows, :]
        for j in range(N_HEADS):
            cols = slice(j * HEAD_DIM, (j + 1) * HEAD_DIM)
            q_ref[0, rows, cols] = _rope(_rms(qkv[:, cols]) * qn, cos, sin, even_lane).astype(BF16)
        for j in range(N_KV_HEADS):
            lo = ATTN_W + j * HEAD_DIM
            kh = _rms(qkv[:, lo:lo + HEAD_DIM]) * kn
            k_ref[0, rows, j * HEAD_DIM:(j + 1) * HEAD_DIM] = _rope(kh, cos, sin, even_lane).astype(BF16)
        v_ref[0, rows, :] = qkv[:, ATTN_W + KV_W:].astype(BF16)
        p_ref[0, rows, :] = jnp.dot(h, w_ref[:, :POOL_W].astype(BF16), preferred_element_type=F32)


def _in_proj(x, mod3, norm_w, w_in, q_norm, k_norm, cos_full, sin_signed, tm=1024):
    b, t, _ = x.shape
    row = lambda i, j: (i, j, 0)
    const2 = lambda i, j: (0, 0)
    return pl.pallas_call(
        _in_proj_kernel,
        grid=(b, t // tm),
        in_specs=[
            pl.BlockSpec((1, tm, D_MODEL), row),
            pl.BlockSpec((N_MOD, MOD_TILE, D_MODEL), lambda i, j: (0, 0, 0)),
            pl.BlockSpec((1, D_MODEL), const2),
            pl.BlockSpec((D_MODEL, PROJ_W), const2, pipeline_mode=pl.Buffered(1)),
            pl.BlockSpec((1, HEAD_DIM), const2),
            pl.BlockSpec((1, HEAD_DIM), const2),
            pl.BlockSpec((tm, HEAD_DIM), lambda i, j: (j, 0)),
            pl.BlockSpec((tm, HEAD_DIM), lambda i, j: (j, 0)),
        ],
        out_specs=[
            pl.BlockSpec((1, tm, POOL_W), row),
            pl.BlockSpec((1, tm, ATTN_W), row),
            pl.BlockSpec((1, tm, KV_W), row),
            pl.BlockSpec((1, tm, KV_W), row),
        ],
        out_shape=[
            jax.ShapeDtypeStruct((b, t, POOL_W), F32),
            jax.ShapeDtypeStruct((b, t, ATTN_W), BF16),
            jax.ShapeDtypeStruct((b, t, KV_W), BF16),
            jax.ShapeDtypeStruct((b, t, KV_W), BF16),
        ],
        compiler_params=_params("arbitrary", "arbitrary"),
        name="in_proj",
    )(x, mod3, norm_w, w_in, q_norm, k_norm, cos_full, sin_signed)


def _pool_mix_rows(p_ref, prev_ref, next_ref, w_ref, s_ref, o_ref, has_prev, has_next):
    t = p_ref.shape[1]
    n = t + 2 * POOL_PAD
    edge_row = lax.broadcasted_iota(jnp.int32, (POOL_PAD, POOL_GROUP_W), 0)
    for g, win in enumerate(POOL_WINDOWS):
        half = win // 2
        cols = slice(g * POOL_GROUP_W, (g + 1) * POOL_GROUP_W)
        u = p_ref[0, :, cols]
        top = jnp.where(has_prev, prev_ref[0, :, cols], 0.0)
        bottom = jnp.where(has_next, next_ref[0, :, cols], 0.0)
        f = jnp.concatenate([top, u, bottom], axis=0)
        step = 1
        while step < win:
            f = f + pltpu.roll(f, n - step, axis=0)
            step *= 2
        if half != POOL_PAD:
            f = pltpu.roll(f, n - (POOL_PAD - half), axis=0)
        first = jnp.where(edge_row < half, edge_row + half, win).astype(F32)
        last = jnp.where(edge_row >= POOL_PAD - half, POOL_PAD - edge_row + half, win).astype(F32)
        inv_len = jnp.concatenate([
            jnp.where(has_prev, 1.0 / win, 1.0 / first),
            jnp.full((t - 2 * POOL_PAD, POOL_GROUP_W), 1.0 / win, F32),
            jnp.where(has_next, 1.0 / win, 1.0 / last)], axis=0)
        pooled = f[:t] * inv_len - u
        mixed = jnp.dot(pooled.astype(BF16), w_ref[g].astype(BF16), preferred_element_type=F32)
        o_ref[0, :, cols] = (mixed * s_ref[:, cols]).astype(BF16)


def _attn_kernel(q_ref, kx_ref, kc_ref, vx_ref, vc_ref,
                 p_ref, prev_ref, next_ref, pw_ref, ps_ref, *refs, n_parts):
    n_cast = (len(refs) - 2) // 2
    o_ref, pool_ref = refs[n_cast], refs[n_cast + 1]
    kx = kx_ref[0]
    kc = kc_ref[0]
    vx = jnp.concatenate([vx_ref[0], jnp.ones(vx_ref.shape[1:], BF16)], axis=1)
    vc = jnp.concatenate([vc_ref[0], jnp.ones(vc_ref.shape[1:], BF16)], axis=1)
    nt = (((1,), (1,)), ((), ()))
    for r in range(q_ref.shape[1] // ATTN_SUB):
        rows = slice(r * ATTN_SUB, (r + 1) * ATTN_SUB)
        for g in range(Q_PER_KV):
            cols = slice(g * HEAD_DIM, (g + 1) * HEAD_DIM)
            q = q_ref[0, rows, cols]
            sx = lax.dot_general(q, kx, nt, preferred_element_type=F32)
            sc = lax.dot_general(q, kc, nt, preferred_element_type=F32)
            m = jnp.maximum(jnp.max(sx, axis=-1, keepdims=True), jnp.max(sc, axis=-1, keepdims=True))
            px = jnp.exp2(sx - m)
            pc = jnp.exp2(sc - m)
            o = (jnp.dot(px.astype(BF16), vx, preferred_element_type=F32)
                 + jnp.dot(pc.astype(BF16), vc, preferred_element_type=F32))
            o_ref[0, rows, cols] = (o[:, :HEAD_DIM] * (1.0 / o[:, HEAD_DIM:])).astype(BF16)
    part = pl.program_id(1) * pl.num_programs(2) + pl.program_id(2)
    _pool_mix_rows(p_ref, prev_ref, next_ref, pw_ref, ps_ref, pool_ref,
                   part > 0, part < n_parts - 1)
    for src_ref, dst_ref in zip(refs[:n_cast], refs[n_cast + 2:]):
        dst_ref[...] = src_ref[...].astype(BF16)


def _attention(q, k_x, v_x, k_c, v_c, p, pool_w, pool_scale, cast_f32, tq=2048):
    b, t, _ = q.shape
    s = k_c.shape[1]
    qw = Q_PER_KV * HEAD_DIM
    n_t = t // tq
    n_parts = N_KV_HEADS * n_t
    n_steps = b * n_parts
    pool_rows = t // n_parts
    halo_blocks = pool_rows // POOL_PAD
    assert pool_rows % POOL_PAD == 0 and all(a.shape[0] % (16 * n_steps) == 0 for a in cast_f32)
    part = lambda h, j: h * n_t + j
    kv_map = lambda i, h, j: (i, 0, h)
    pool_map = lambda i, h, j: (i, part(h, j), 0)
    slab = lambda a: pl.BlockSpec((a.shape[0] // n_steps, a.shape[1]),
                                  lambda i, h, j: (i * n_parts + part(h, j), 0))
    out = pl.pallas_call(
        functools.partial(_attn_kernel, n_parts=n_parts),
        grid=(b, N_KV_HEADS, n_t),
        in_specs=[
            pl.BlockSpec((1, tq, qw), lambda i, h, j: (i, j, h)),
            pl.BlockSpec((1, t, HEAD_DIM), kv_map),
            pl.BlockSpec((1, s, HEAD_DIM), kv_map),
            pl.BlockSpec((1, t, HEAD_DIM), kv_map),
            pl.BlockSpec((1, s, HEAD_DIM), kv_map),
            pl.BlockSpec((1, pool_rows, POOL_W), pool_map),
            pl.BlockSpec((1, POOL_PAD, POOL_W),
                         lambda i, h, j: (i, jnp.maximum(part(h, j) * halo_blocks - 1, 0), 0)),
            pl.BlockSpec((1, POOL_PAD, POOL_W),
                         lambda i, h, j: (i, jnp.minimum((part(h, j) + 1) * halo_blocks,
                                                         n_parts * halo_blocks - 1), 0)),
            pl.BlockSpec((N_POOL_GROUPS, POOL_GROUP_W, POOL_GROUP_W), lambda i, h, j: (0, 0, 0)),
            pl.BlockSpec((1, POOL_W), lambda i, h, j: (0, 0)),
            *[slab(a) for a in cast_f32],
        ],
        out_specs=[pl.BlockSpec((1, tq, qw), lambda i, h, j: (i, j, h)),
                   pl.BlockSpec((1, pool_rows, POOL_W), pool_map),
                   *[slab(a) for a in cast_f32]],
        out_shape=[jax.ShapeDtypeStruct((b, t, ATTN_W), BF16),
                   jax.ShapeDtypeStruct((b, t, POOL_W), BF16),
                   *[jax.ShapeDtypeStruct(a.shape, BF16) for a in cast_f32]],
        compiler_params=_params("arbitrary", "arbitrary", "arbitrary"),
        name="attention",
    )(q, k_x, k_c, v_x, v_c, p, p, p, pool_w, pool_scale, *cast_f32)
    return out[0], out[1], out[2:]


def _out_proj_kernel(x_ref, *refs):
    lhs_refs, (mod_ref, nw_ref), w_refs, (o_ref, h_ref) = refs[:4], refs[4:6], refs[6:-2], refs[-2:]
    batch = pl.program_id(0)
    gate = _mod_row(mod_ref, 2, batch)
    nw = nw_ref[...]
    for r in range(x_ref.shape[1] // OUT_SUB):
        rows = slice(r * OUT_SUB, (r + 1) * OUT_SUB)
        lhs = jnp.concatenate([ref[0, rows, :] for ref in lhs_refs], axis=1)
        mix = jnp.concatenate([jnp.dot(lhs, w_ref[...], preferred_element_type=F32)
                               for w_ref in w_refs], axis=1)
        x1 = x_ref[0, rows, :] + gate * mix
        o_ref[0, rows, :] = x1
        h_ref[0, rows, :] = _norm_modulate(x1, nw, mod_ref, 3, batch).astype(BF16)


def _out_proj(x, pool_mix, attn, mod3, norm_w, w_out_bf, tm=1024):
    b, t, _ = x.shape
    row = lambda i, j: (i, j, 0)
    lhs_window = lambda c: pl.BlockSpec((1, tm, MXU_WINDOW_W), lambda i, j: (i, j, c))
    w_window = lambda c: pl.BlockSpec((MIX_W, MXU_WINDOW_W), lambda i, j: (0, c),
                                      pipeline_mode=pl.Buffered(1))
    n_w = D_MODEL // MXU_WINDOW_W
    return pl.pallas_call(
        _out_proj_kernel,
        grid=(b, t // tm),
        in_specs=[
            pl.BlockSpec((1, tm, D_MODEL), row),
            lhs_window(0), lhs_window(1), lhs_window(0), lhs_window(1),
            pl.BlockSpec((N_MOD, MOD_TILE, D_MODEL), lambda i, j: (0, 0, 0)),
            pl.BlockSpec((1, D_MODEL), lambda i, j: (0, 0)),
            *[w_window(c) for c in range(n_w)],
        ],
        out_specs=[pl.BlockSpec((1, tm, D_MODEL), row), pl.BlockSpec((1, tm, D_MODEL), row)],
        out_shape=[jax.ShapeDtypeStruct(x.shape, F32), jax.ShapeDtypeStruct(x.shape, BF16)],
        compiler_params=_params("arbitrary", "arbitrary"),
        name="out_proj",
    )(x, pool_mix, pool_mix, attn, attn, mod3, norm_w, *([w_out_bf] * n_w))


def _ffn_kernel(h_ref, x_ref, gate_ref, wg_ref, wu_ref, wd_ref, fn_ref, o_hbm, y_ref, act_ref, out_sem,
                *, n_f, tf, tn):
    i, j, s = pl.program_id(0), pl.program_id(1), pl.program_id(2)
    n_i, n_j, n_s = pl.num_programs(0), pl.num_programs(1), pl.num_programs(2)
    tm = y_ref.shape[0]
    chunks = wg_ref.shape[1] // tf
    n_up = -(-n_f // chunks)

    def out_copy(bi, tj):
        return pltpu.make_async_copy(y_ref, o_hbm.at[bi, pl.ds(tj * tm, tm), :], out_sem)

    def up_step(first_chunk, n_chunks):
        h = h_ref[0]
        for c in range(n_chunks * tf // FFN_CHUNK_W):
            cols = slice(c * FFN_CHUNK_W, (c + 1) * FFN_CHUNK_W)
            gate = jnp.dot(h, wg_ref[:, cols], preferred_element_type=F32)
            up = jnp.dot(h, wu_ref[:, cols], preferred_element_type=F32)
            half_gate = 0.5 * gate
            act = (half_gate * (1.0 + jnp.tanh(half_gate)) * up).astype(BF16)
            k, lo = divmod(c * FFN_CHUNK_W, tf)
            act_ref[first_chunk + k, :, lo:lo + FFN_CHUNK_W] = act

    @pl.when(s < n_up - 1)
    def _():
        up_step(s * chunks, chunks)

    @pl.when(s == n_up - 1)
    def _():
        up_step((n_up - 1) * chunks, n_f - (n_up - 1) * chunks)

    @pl.when((s == n_up) & ((i > 0) | (j > 0)))
    def _():
        out_copy(jnp.where(j == 0, i - 1, i), jnp.where(j == 0, n_j - 1, j - 1)).wait()

    @pl.when(s >= n_up)
    def _():
        y = jnp.dot(act_ref[0], wd_ref[0:tf, :], preferred_element_type=F32)
        for f in range(1, n_f):
            y += jnp.dot(act_ref[f], wd_ref[f * tf:(f + 1) * tf, :], preferred_element_type=F32)
        cols = pl.ds(pl.multiple_of((s - n_up) * tn, tn), tn)
        y_ref[:, cols] = x_ref[0] + _mod_row(gate_ref, 0, i) * y

    @pl.when(s == n_s - 1)
    def _():
        y_ref[...] = _rms(y_ref[...]) * fn_ref[...]
        out_copy(i, j).start()

        @pl.when((i == n_i - 1) & (j == n_j - 1))
        def _():
            out_copy(i, j).wait()


def _ffn(h, x1, mod3, wg_bf, wu_bf, wd_bf, final_norm, tm=1024, tf=512, up_chunks=2, tn=512):
    b, t, _ = h.shape
    d_ff = wg_bf.shape[1]
    n_f = d_ff // tf
    n_up = -(-n_f // up_chunks)
    n_n = D_MODEL // tn
    up_blk = lambda i, j, s: (0, jnp.minimum(s, n_up - 1))
    down_blk = lambda i, j, s: (0, jnp.maximum(s - n_up, 0))
    return pl.pallas_call(
        functools.partial(_ffn_kernel, n_f=n_f, tf=tf, tn=tn),
        grid=(b, t // tm, n_up + n_n),
        in_specs=[
            pl.BlockSpec((1, tm, D_MODEL), lambda i, j, s: (i, j, 0)),
            pl.BlockSpec((1, tm, tn), lambda i, j, s: (i, j, jnp.maximum(s - n_up, 0))),
            pl.BlockSpec((1, MOD_TILE, tn), lambda i, j, s: (N_MOD - 1, 0, jnp.maximum(s - n_up, 0))),
            pl.BlockSpec((D_MODEL, up_chunks * tf), up_blk),
            pl.BlockSpec((D_MODEL, up_chunks * tf), up_blk),
            pl.BlockSpec((d_ff, tn), down_blk),
            pl.BlockSpec((1, D_MODEL), lambda i, j, s: (0, 0)),
        ],
        out_specs=pl.BlockSpec(memory_space=pl.ANY),
        out_shape=jax.ShapeDtypeStruct(x1.shape, F32),
        scratch_shapes=[pltpu.VMEM((tm, D_MODEL), F32), pltpu.VMEM((n_f, tm, tf), BF16),
                        pltpu.SemaphoreType.DMA(())],
        compiler_params=_params("arbitrary", "arbitrary", "arbitrary"),
        name="ffn",
    )(h, x1, mod3, wg_bf, wu_bf, wd_bf, final_norm)


def _rope_tables(t):
    n_rows = t // GRID_W
    rows = np.repeat(np.arange(n_rows, dtype=np.float32), GRID_W)
    cols = np.tile(np.arange(GRID_W, dtype=np.float32), n_rows)
    freqs = np.float32(ROPE_THETA) ** (-np.arange(0, AXIS_ROT, 2, dtype=np.float32) / np.float32(AXIS_ROT))
    ang = np.concatenate([rows[:, None] * freqs, cols[:, None] * freqs], axis=-1)
    cos_full = np.repeat(np.cos(ang), 2, axis=-1)
    sin = np.sin(ang)
    sin_signed = np.stack([-sin, sin], axis=-1).reshape(t, HEAD_DIM)
    return jnp.asarray(cos_full, F32), jnp.asarray(sin_signed, F32)


def kernel(x, c, ctx, c_ctx, w_ada, b_ada, norm_mix, norm_ffn, w_in, pool_w, pool_scale,
           q_norm, k_norm, w_out, w_gate, w_up, w_down, final_norm):
    depth = w_ada.shape[0]
    assert depth == 1, "context tokens are only updated between layers; one layer is implemented"
    b, t, _ = x.shape
    cos_full, sin_signed = _rope_tables(t)

    mod3 = _adaln_mod(c, c_ctx, w_ada[0], b_ada)

    k_c, v_c = _ctx_kv(ctx, mod3, norm_mix, w_in[0], k_norm)
    p, q, k_x, v_x = _in_proj(x, mod3, norm_mix, w_in[0], q_norm, k_norm, cos_full, sin_signed)
    attn, pooled, (wo_bf, wg_bf, wu_bf, wd_bf) = _attention(
        q, k_x, v_x, k_c, v_c, p, pool_w[0], pool_scale, (w_out[0], w_gate[0], w_up[0], w_down[0]))
    x1, hf = _out_proj(x, pooled, attn, mod3, norm_ffn, wo_bf)
    return _ffn(hf, x1, mod3, wg_bf, wu_bf, wd_bf, final_norm.reshape(1, D_MODEL))
```

```python
import functools
import math

import jax
import jax.numpy as jnp
import numpy as np
from jax import lax
from jax.experimental import pallas as pl
from jax.experimental.pallas import tpu as pltpu

D_MODEL = 2048
GRID_W = 64
HEAD_DIM = 128
N_HEADS = 8
N_KV_HEADS = 2
Q_PER_KV = N_HEADS // N_KV_HEADS
ATTN_W = N_HEADS * HEAD_DIM
KV_W = N_KV_HEADS * HEAD_DIM
POOL_WINDOWS = (2, 4, 8, 16)
N_POOL_GROUPS = len(POOL_WINDOWS)
POOL_W = D_MODEL // 2
POOL_GROUP_W = POOL_W // N_POOL_GROUPS
MIX_W = POOL_W + ATTN_W
PROJ_W = POOL_W + ATTN_W + 2 * KV_W
ROPE_THETA = 10000.0
AXIS_ROT = HEAD_DIM // 2
EPS = 1e-6
N_MOD = 6
MOD_ROWS = 16
MOD_TILE = 8
POOL_PAD = 8
IN_SUB = 512
OUT_SUBTILES = ((0, 384), (384, 768), (768, 1024))
MXU_WINDOW_W = 512
ATTN_SUB = 256
FFN_CHUNK_W = 256
Q_SCALE = math.log2(math.e) / math.sqrt(HEAD_DIM)

F32 = jnp.float32
BF16 = jnp.bfloat16

VMEM_LIMIT_BYTES = 63 * 1024 * 1024


def _params(*sem):
    return pltpu.CompilerParams(dimension_semantics=sem, vmem_limit_bytes=VMEM_LIMIT_BYTES)


def _rms(x):
    return x * lax.rsqrt(jnp.mean(x * x, axis=-1, keepdims=True) + EPS)


def _adaln_kernel(c_ref, cctx_ref, w_ref, b_ref, o_ref):
    c = jnp.concatenate([c_ref[...], jnp.broadcast_to(cctx_ref[...], (MOD_ROWS - MOD_TILE, D_MODEL))], axis=0)
    a = c * jax.nn.sigmoid(c)
    o_ref[0] = jnp.dot(a, w_ref[...], preferred_element_type=F32) + b_ref[...]


def _adaln_mod(c, c_ctx, w_ada, b_ada, tn=1024):
    assert c.shape[0] == MOD_TILE
    per_chunk = D_MODEL // tn
    return pl.pallas_call(
        _adaln_kernel,
        grid=(N_MOD * per_chunk,),
        in_specs=[
            pl.BlockSpec((MOD_TILE, D_MODEL), lambda j: (0, 0)),
            pl.BlockSpec((1, D_MODEL), lambda j: (0, 0)),
            pl.BlockSpec((D_MODEL, tn), lambda j: (0, j)),
            pl.BlockSpec((1, tn), lambda j: (0, j)),
        ],
        out_specs=pl.BlockSpec((1, MOD_ROWS, tn), lambda j: (j // per_chunk, 0, j % per_chunk)),
        out_shape=jax.ShapeDtypeStruct((N_MOD, MOD_ROWS, D_MODEL), F32),
        compiler_params=_params("arbitrary"),
        name="adaln_mod",
    )(c, c_ctx.reshape(1, D_MODEL), w_ada, b_ada)


def _mod_row(mod_ref, chunk, row):
    return mod_ref[chunk, pl.ds(row, 1), :]


def _norm_modulate(x, nw, mod_ref, shift_idx, row):
    sh = _mod_row(mod_ref, shift_idx, row)
    sc = _mod_row(mod_ref, shift_idx + 1, row)
    return (_rms(x) * nw) * (1.0 + sc) + sh


def _rope(x, cos, sin_signed, even_lane):
    partner = jnp.where(even_lane, pltpu.roll(x, HEAD_DIM - 1, axis=1), pltpu.roll(x, 1, axis=1))
    return x * cos + partner * sin_signed


def _ctx_kv_kernel(x_ref, mod_ref, nw_ref, w_ref, kn_ref, k_ref, v_ref):
    nw = nw_ref[...]
    kn = kn_ref[...]
    for r in range(x_ref.shape[0] // IN_SUB):
        rows = slice(r * IN_SUB, (r + 1) * IN_SUB)
        h = _norm_modulate(x_ref[rows, :], nw, mod_ref, 0, 0)
        kv = jnp.dot(h.astype(BF16), w_ref[...].astype(BF16), preferred_element_type=F32)
        for j in range(N_KV_HEADS):
            cols = slice(j * HEAD_DIM, (j + 1) * HEAD_DIM)
            k_ref[rows, cols] = (_rms(kv[:, cols]) * kn).astype(BF16)
        v_ref[rows, :] = kv[:, KV_W:].astype(BF16)


def _ctx_kv(ctx, mod3, norm_w, w_in, k_norm, tm=1024):
    b, s, _ = ctx.shape
    kv_block = (POOL_W + ATTN_W) // (2 * KV_W)
    k, v = pl.pallas_call(
        _ctx_kv_kernel,
        grid=(b * s // tm,),
        in_specs=[
            pl.BlockSpec((tm, D_MODEL), lambda i: (i, 0)),
            pl.BlockSpec((N_MOD, MOD_TILE, D_MODEL), lambda i: (0, 1, 0)),
            pl.BlockSpec((1, D_MODEL), lambda i: (0, 0)),
            pl.BlockSpec((D_MODEL, 2 * KV_W), lambda i: (0, kv_block)),
            pl.BlockSpec((1, HEAD_DIM), lambda i: (0, 0)),
        ],
        out_specs=[pl.BlockSpec((tm, KV_W), lambda i: (i, 0))] * 2,
        out_shape=[jax.ShapeDtypeStruct((b * s, KV_W), BF16)] * 2,
        compiler_params=_params("arbitrary"),
        name="ctx_kv",
    )(ctx.reshape(b * s, D_MODEL), mod3, norm_w, w_in, k_norm)
    return k.reshape(b, s, KV_W), v.reshape(b, s, KV_W)


def _in_proj_kernel(x_ref, mod_ref, nw_ref, w_ref, qn_ref, kn_ref, cos_ref, sin_ref,
                    p_ref, q_ref, k_ref, v_ref):
    nw = nw_ref[...]
    qn = qn_ref[...] * Q_SCALE
    kn = kn_ref[...]
    even_lane = (lax.broadcasted_iota(jnp.int32, (IN_SUB, HEAD_DIM), 1) % 2) == 0
    for r in range(x_ref.shape[1] // IN_SUB):
        rows = slice(r * IN_SUB, (r + 1) * IN_SUB)
        h = _norm_modulate(x_ref[0, rows, :], nw, mod_ref, 0, pl.program_id(0)).astype(BF16)
        qkv = jnp.dot(h, w_ref[:, POOL_W:].astype(BF16), preferred_element_type=F32)
        cos = cos_ref[rows, :]
        sin = sin_ref[rows, :]
        for j in range(N_HEADS):
            cols = slice(j * HEAD_DIM, (j + 1) * HEAD_DIM)
            q_ref[0, rows, cols] = _rope(_rms(qkv[:, cols]) * qn, cos, sin, even_lane).astype(BF16)
        for j in range(N_KV_HEADS):
            lo = ATTN_W + j * HEAD_DIM
            kh = _rms(qkv[:, lo:lo + HEAD_DIM]) * kn
            k_ref[0, rows, j * HEAD_DIM:(j + 1) * HEAD_DIM] = _rope(kh, cos, sin, even_lane).astype(BF16)
        v_ref[0, rows, :] = qkv[:, ATTN_W + KV_W:].astype(BF16)
        p_ref[0, rows, :] = jnp.dot(h, w_ref[:, :POOL_W].astype(BF16), preferred_element_type=F32)


def _in_proj(x, mod3, norm_w, w_in, q_norm, k_norm, cos_full, sin_signed, tm=1024):
    b, t, _ = x.shape
    row = lambda i, j: (i, j, 0)
    const2 = lambda i, j: (0, 0)
    return pl.pallas_call(
        _in_proj_kernel,
        grid=(b, t // tm),
        in_specs=[
            pl.BlockSpec((1, tm, D_MODEL), row),
            pl.BlockSpec((N_MOD, MOD_TILE, D_MODEL), lambda i, j: (0, 0, 0)),
            pl.BlockSpec((1, D_MODEL), const2),
            pl.BlockSpec((D_MODEL, PROJ_W), const2, pipeline_mode=pl.Buffered(1)),
            pl.BlockSpec((1, HEAD_DIM), const2),
            pl.BlockSpec((1, HEAD_DIM), const2),
            pl.BlockSpec((tm, HEAD_DIM), lambda i, j: (j, 0)),
            pl.BlockSpec((tm, HEAD_DIM), lambda i, j: (j, 0)),
        ],
        out_specs=[
            pl.BlockSpec((1, tm, POOL_W), row),
            pl.BlockSpec((1, tm, ATTN_W), row),
            pl.BlockSpec((1, tm, KV_W), row),
            pl.BlockSpec((1, tm, KV_W), row),
        ],
        out_shape=[
            jax.ShapeDtypeStruct((b, t, POOL_W), F32),
            jax.ShapeDtypeStruct((b, t, ATTN_W), BF16),
            jax.ShapeDtypeStruct((b, t, KV_W), BF16),
            jax.ShapeDtypeStruct((b, t, KV_W), BF16),
        ],
        compiler_params=_params("arbitrary", "arbitrary"),
        name="in_proj",
    )(x, mod3, norm_w, w_in, q_norm, k_norm, cos_full, sin_signed)


def _pool_mix_rows(p_ref, prev_ref, next_ref, w_ref, s_ref, o_ref, has_prev, has_next):
    t = p_ref.shape[1]
    n = t + 2 * POOL_PAD
    edge_row = lax.broadcasted_iota(jnp.int32, (POOL_PAD, POOL_GROUP_W), 0)
    for g, win in enumerate(POOL_WINDOWS):
        half = win // 2
        cols = slice(g * POOL_GROUP_W, (g + 1) * POOL_GROUP_W)
        u = p_ref[0, :, cols]
        top = jnp.where(has_prev, prev_ref[0, :, cols], 0.0)
        bottom = jnp.where(has_next, next_ref[0, :, cols], 0.0)
        f = jnp.concatenate([top, u, bottom], axis=0)
        step = 1
        while step < win:
            f = f + pltpu.roll(f, n - step, axis=0)
            step *= 2
        if half != POOL_PAD:
            f = pltpu.roll(f, n - (POOL_PAD - half), axis=0)
        first = jnp.where(edge_row < half, edge_row + half, win).astype(F32)
        last = jnp.where(edge_row >= POOL_PAD - half, POOL_PAD - edge_row + half, win).astype(F32)
        inv_len = jnp.concatenate([
            jnp.where(has_prev, 1.0 / win, 1.0 / first),
            jnp.full((t - 2 * POOL_PAD, POOL_GROUP_W), 1.0 / win, F32),
            jnp.where(has_next, 1.0 / win, 1.0 / last)], axis=0)
        pooled = f[:t] * inv_len - u
        mixed = jnp.dot(pooled.astype(BF16), w_ref[g].astype(BF16), preferred_element_type=F32)
        o_ref[0, :, cols] = (mixed * s_ref[:, cols]).astype(BF16)


def _attn_kernel(q_ref, kx_ref, kc_ref, vx_ref, vc_ref,
                 p_ref, prev_ref, next_ref, pw_ref, ps_ref, *refs, n_parts):
    n_cast = (len(refs) - 2) // 2
    o_ref, pool_ref = refs[n_cast], refs[n_cast + 1]
    kx = kx_ref[0]
    kc = kc_ref[0]
    vx = jnp.concatenate([vx_ref[0], jnp.ones(vx_ref.shape[1:], BF16)], axis=1)
    vc = jnp.concatenate([vc_ref[0], jnp.ones(vc_ref.shape[1:], BF16)], axis=1)
    nt = (((1,), (1,)), ((), ()))
    for r in range(q_ref.shape[1] // ATTN_SUB):
        rows = slice(r * ATTN_SUB, (r + 1) * ATTN_SUB)
        for g in range(Q_PER_KV):
            cols = slice(g * HEAD_DIM, (g + 1) * HEAD_DIM)
            q = q_ref[0, rows, cols]
            sx = lax.dot_general(q, kx, nt, preferred_element_type=F32)
            sc = lax.dot_general(q, kc, nt, preferred_element_type=F32)
            m = jnp.maximum(jnp.max(sx, axis=-1, keepdims=True), jnp.max(sc, axis=-1, keepdims=True))
            px = jnp.exp2(sx - m)
            pc = jnp.exp2(sc - m)
            o = (jnp.dot(px.astype(BF16), vx, preferred_element_type=F32)
                 + jnp.dot(pc.astype(BF16), vc, preferred_element_type=F32))
            o_ref[0, rows, cols] = (o[:, :HEAD_DIM] * (1.0 / o[:, HEAD_DIM:])).astype(BF16)
    part = pl.program_id(1) * pl.num_programs(2) + pl.program_id(2)
    _pool_mix_rows(p_ref, prev_ref, next_ref, pw_ref, ps_ref, pool_ref,
                   part > 0, part < n_parts - 1)
    for src_ref, dst_ref in zip(refs[:n_cast], refs[n_cast + 2:]):
        dst_ref[...] = src_ref[...].astype(BF16)


def _attention(q, k_x, v_x, k_c, v_c, p, pool_w, pool_scale, cast_f32, tq=1024):
    b, t, _ = q.shape
    s = k_c.shape[1]
    qw = Q_PER_KV * HEAD_DIM
    n_t = t // tq
    n_parts = N_KV_HEADS * n_t
    n_steps = b * n_parts
    pool_rows = t // n_parts
    halo_blocks = pool_rows // POOL_PAD
    assert pool_rows % POOL_PAD == 0 and all(a.shape[0] % (16 * n_steps) == 0 for a in cast_f32)
    part = lambda h, j: h * n_t + j
    kv_map = lambda i, h, j: (i, 0, h)
    pool_map = lambda i, h, j: (i, part(h, j), 0)
    slab = lambda a: pl.BlockSpec((a.shape[0] // n_steps, a.shape[1]),
                                  lambda i, h, j: (i * n_parts + part(h, j), 0))
    out = pl.pallas_call(
        functools.partial(_attn_kernel, n_parts=n_parts),
        grid=(b, N_KV_HEADS, n_t),
        in_specs=[
            pl.BlockSpec((1, tq, qw), lambda i, h, j: (i, j, h)),
            pl.BlockSpec((1, t, HEAD_DIM), kv_map),
            pl.BlockSpec((1, s, HEAD_DIM), kv_map),
            pl.BlockSpec((1, t, HEAD_DIM), kv_map),
            pl.BlockSpec((1, s, HEAD_DIM), kv_map),
            pl.BlockSpec((1, pool_rows, POOL_W), pool_map),
            pl.BlockSpec((1, POOL_PAD, POOL_W),
                         lambda i, h, j: (i, jnp.maximum(part(h, j) * halo_blocks - 1, 0), 0)),
            pl.BlockSpec((1, POOL_PAD, POOL_W),
                         lambda i, h, j: (i, jnp.minimum((part(h, j) + 1) * halo_blocks,
                                                         n_parts * halo_blocks - 1), 0)),
            pl.BlockSpec((N_POOL_GROUPS, POOL_GROUP_W, POOL_GROUP_W), lambda i, h, j: (0, 0, 0)),
            pl.BlockSpec((1, POOL_W), lambda i, h, j: (0, 0)),
            *[slab(a) for a in cast_f32],
        ],
        out_specs=[pl.BlockSpec((1, tq, qw), lambda i, h, j: (i, j, h)),
                   pl.BlockSpec((1, pool_rows, POOL_W), pool_map),
                   *[slab(a) for a in cast_f32]],
        out_shape=[jax.ShapeDtypeStruct((b, t, ATTN_W), BF16),
                   jax.ShapeDtypeStruct((b, t, POOL_W), BF16),
                   *[jax.ShapeDtypeStruct(a.shape, BF16) for a in cast_f32]],
        compiler_params=_params("arbitrary", "arbitrary", "arbitrary"),
        name="attention",
    )(q, k_x, k_c, v_x, v_c, p, p, p, pool_w, pool_scale, *cast_f32)
    return out[0], out[1], out[2:]


def _out_proj_kernel(x_ref, *refs):
    lhs_refs, (mod_ref, nw_ref), w_refs, (o_ref, h_ref) = refs[:4], refs[4:6], refs[6:-2], refs[-2:]
    batch = pl.program_id(0)
    gate = _mod_row(mod_ref, 2, batch)
    nw = nw_ref[...]
    for lo, hi in OUT_SUBTILES:
        rows = slice(lo, hi)
        lhs = jnp.concatenate([ref[0, rows, :] for ref in lhs_refs], axis=1)
        mix = jnp.concatenate([jnp.dot(lhs, w_ref[...], preferred_element_type=F32)
                               for w_ref in w_refs], axis=1)
        x1 = x_ref[0, rows, :] + gate * mix
        o_ref[0, rows, :] = x1
        h_ref[0, rows, :] = _norm_modulate(x1, nw, mod_ref, 3, batch).astype(BF16)


def _out_proj(x, pool_mix, attn, mod3, norm_w, w_out_bf):
    b, t, _ = x.shape
    tm = OUT_SUBTILES[-1][1]
    row = lambda i, j: (i, j, 0)
    lhs_window = lambda c: pl.BlockSpec((1, tm, MXU_WINDOW_W), lambda i, j: (i, j, c))
    w_window = lambda c: pl.BlockSpec((MIX_W, MXU_WINDOW_W), lambda i, j: (0, c),
                                      pipeline_mode=pl.Buffered(1))
    n_w = D_MODEL // MXU_WINDOW_W
    return pl.pallas_call(
        _out_proj_kernel,
        grid=(b, t // tm),
        in_specs=[
            pl.BlockSpec((1, tm, D_MODEL), row),
            lhs_window(0), lhs_window(1), lhs_window(0), lhs_window(1),
            pl.BlockSpec((N_MOD, MOD_TILE, D_MODEL), lambda i, j: (0, 0, 0)),
            pl.BlockSpec((1, D_MODEL), lambda i, j: (0, 0)),
            *[w_window(c) for c in range(n_w)],
        ],
        out_specs=[pl.BlockSpec((1, tm, D_MODEL), row), pl.BlockSpec((1, tm, D_MODEL), row)],
        out_shape=[jax.ShapeDtypeStruct(x.shape, F32), jax.ShapeDtypeStruct(x.shape, BF16)],
        compiler_params=_params("arbitrary", "arbitrary"),
        name="out_proj",
    )(x, pool_mix, pool_mix, attn, attn, mod3, norm_w, *([w_out_bf] * n_w))


def _ffn_kernel(h_ref, x_ref, gate_ref, wg_ref, wu_ref, wd_ref, fn_ref, o_hbm, y_ref, act_ref, out_sem,
                *, n_f, tf, tn):
    i, j, s = pl.program_id(0), pl.program_id(1), pl.program_id(2)
    n_i, n_j, n_s = pl.num_programs(0), pl.num_programs(1), pl.num_programs(2)
    tm = y_ref.shape[0]
    chunks = wg_ref.shape[1] // tf
    n_up = -(-n_f // chunks)

    def out_copy(bi, tj):
        return pltpu.make_async_copy(y_ref, o_hbm.at[bi, pl.ds(tj * tm, tm), :], out_sem)

    def up_step(first_chunk, n_chunks):
        h = h_ref[0]
        for c in range(n_chunks * tf // FFN_CHUNK_W):
            cols = slice(c * FFN_CHUNK_W, (c + 1) * FFN_CHUNK_W)
            gate = jnp.dot(h, wg_ref[:, cols], preferred_element_type=F32)
            up = jnp.dot(h, wu_ref[:, cols], preferred_element_type=F32)
            half_gate = 0.5 * gate
            act = (half_gate * (1.0 + jnp.tanh(half_gate)) * up).astype(BF16)
            k, lo = divmod(c * FFN_CHUNK_W, tf)
            act_ref[first_chunk + k, :, lo:lo + FFN_CHUNK_W] = act

    @pl.when(s < n_up - 1)
    def _():
        up_step(s * chunks, chunks)

    @pl.when(s == n_up - 1)
    def _():
        up_step((n_up - 1) * chunks, n_f - (n_up - 1) * chunks)

    @pl.when((s == n_up) & ((i > 0) | (j > 0)))
    def _():
        out_copy(jnp.where(j == 0, i - 1, i), jnp.where(j == 0, n_j - 1, j - 1)).wait()

    @pl.when(s >= n_up)
    def _():
        y = jnp.dot(act_ref[0], wd_ref[0:tf, :], preferred_element_type=F32)
        for f in range(1, n_f):
            y += jnp.dot(act_ref[f], wd_ref[f * tf:(f + 1) * tf, :], preferred_element_type=F32)
        cols = pl.ds(pl.multiple_of((s - n_up) * tn, tn), tn)
        y_ref[:, cols] = x_ref[0] + gate_ref[0, pl.ds(i, 1), cols] * y

    @pl.when(s == n_s - 1)
    def _():
        y_ref[...] = _rms(y_ref[...]) * fn_ref[...]
        out_copy(i, j).start()

        @pl.when((i == n_i - 1) & (j == n_j - 1))
        def _():
            out_copy(i, j).wait()


def _ffn(h, x1, mod3, wg_bf, wu_bf, wd_bf, final_norm, tm=1024, tf=512, up_chunks=2, tn=512):
    b, t, _ = h.shape
    d_ff = wg_bf.shape[1]
    n_f = d_ff // tf
    n_up = -(-n_f // up_chunks)
    n_n = D_MODEL // tn
    up_blk = lambda i, j, s: (0, jnp.minimum(s, n_up - 1))
    down_blk = lambda i, j, s: (0, jnp.maximum(s - n_up, 0))
    return pl.pallas_call(
        functools.partial(_ffn_kernel, n_f=n_f, tf=tf, tn=tn),
        grid=(b, t // tm, n_up + n_n),
        in_specs=[
            pl.BlockSpec((1, tm, D_MODEL), lambda i, j, s: (i, j, 0)),
            pl.BlockSpec((1, tm, tn), lambda i, j, s: (i, j, jnp.maximum(s - n_up, 0))),
            pl.BlockSpec((1, MOD_TILE, D_MODEL), lambda i, j, s: (N_MOD - 1, 0, 0)),
            pl.BlockSpec((D_MODEL, up_chunks * tf), up_blk),
            pl.BlockSpec((D_MODEL, up_chunks * tf), up_blk),
            pl.BlockSpec((d_ff, tn), down_blk),
            pl.BlockSpec((1, D_MODEL), lambda i, j, s: (0, 0)),
        ],
        out_specs=pl.BlockSpec(memory_space=pl.ANY),
        out_shape=jax.ShapeDtypeStruct(x1.shape, F32),
        scratch_shapes=[pltpu.VMEM((tm, D_MODEL), F32), pltpu.VMEM((n_f, tm, tf), BF16),
                        pltpu.SemaphoreType.DMA(())],
        compiler_params=_params("arbitrary", "arbitrary", "arbitrary"),
        name="ffn",
    )(h, x1, mod3, wg_bf, wu_bf, wd_bf, final_norm)


def _rope_tables(t):
    n_rows = t // GRID_W
    rows = np.repeat(np.arange(n_rows, dtype=np.float32), GRID_W)
    cols = np.tile(np.arange(GRID_W, dtype=np.float32), n_rows)
    freqs = np.float32(ROPE_THETA) ** (-np.arange(0, AXIS_ROT, 2, dtype=np.float32) / np.float32(AXIS_ROT))
    ang = np.concatenate([rows[:, None] * freqs, cols[:, None] * freqs], axis=-1)
    cos_full = np.repeat(np.cos(ang), 2, axis=-1)
    sin = np.sin(ang)
    sin_signed = np.stack([-sin, sin], axis=-1).reshape(t, HEAD_DIM)
    return jnp.asarray(cos_full, F32), jnp.asarray(sin_signed, F32)


def kernel(x, c, ctx, c_ctx, w_ada, b_ada, norm_mix, norm_ffn, w_in, pool_w, pool_scale,
           q_norm, k_norm, w_out, w_gate, w_up, w_down, final_norm):
    depth = w_ada.shape[0]
    assert depth == 1, "context tokens are only updated between layers; one layer is implemented"
    b, t, _ = x.shape
    cos_full, sin_signed = _rope_tables(t)

    mod3 = _adaln_mod(c, c_ctx, w_ada[0], b_ada)

    k_c, v_c = _ctx_kv(ctx, mod3, norm_mix, w_in[0], k_norm)
    p, q, k_x, v_x = _in_proj(x, mod3, norm_mix, w_in[0], q_norm, k_norm, cos_full, sin_signed)
    attn, pooled, (wo_bf, wg_bf, wu_bf, wd_bf) = _attention(
        q, k_x, v_x, k_c, v_c, p, pool_w[0], pool_scale, (w_out[0], w_gate[0], w_up[0], w_down[0]))
    x1, hf = _out_proj(x, pooled, attn, mod3, norm_ffn, wo_bf)
    return _ffn(hf, x1, mod3, wg_bf, wu_bf, wd_bf, final_norm.reshape(1, D_MODEL))
```
